```python
import jax, jax.numpy as jnp
from jax import lax
import numpy as np

D_MODEL = 1024
BATCH = 8
SEQ = 4096
DEPTH = 4

GRID_W = 64
CTX_LEN = 256
HG_HEADS = 4
HG_DK = 128
HG_DV = 128
HG_KW = HG_HEADS * HG_DK
HG_W = HG_HEADS * HG_DV
HG_CHUNK = 64
NA_HEADS = 8
NA_HD = 64
NA_W = NA_HEADS * NA_HD
NA_WIN_R = 8
NA_WIN_C = 16
ROPE_THETA = 10000.0
SC_W = 512
SC_K = 3
D_FF = 4 * D_MODEL
N_MOD = 6
EPS = 1e-6
SPLIT_SIZES = (HG_KW, HG_KW, HG_KW, HG_W, HG_W, NA_W, NA_W, NA_W, SC_W, SC_W, SC_W, D_MODEL, D_MODEL, D_MODEL)
PROJ_W = sum(SPLIT_SIZES)
SPLIT_POINTS = tuple(sum(SPLIT_SIZES[:i + 1]) for i in range(len(SPLIT_SIZES) - 1))

kernel_name = "hybrid_hgrn2_natten_shortconv_dit_block"


def _rms_norm(x, w):
    xf = x.astype(jnp.float32)
    y = xf * lax.rsqrt(jnp.mean(jnp.square(xf), axis=-1, keepdims=True) + EPS)
    return (y * w.astype(jnp.float32)).astype(x.dtype)


def _modulate(x, w, shift, scale):
    return _rms_norm(x, w) * (1.0 + scale) + shift


def _axial_rope_tables(n_tokens):
    t = jnp.arange(n_tokens)
    pos = jnp.stack([t // GRID_W, t % GRID_W], axis=-1).astype(jnp.float32)
    half = NA_HD // 2
    inv = ROPE_THETA ** (-jnp.arange(0, half, 2, dtype=jnp.float32) / half)
    ang = pos[:, :, None] * inv
    return jnp.cos(ang), jnp.sin(ang)


def _apply_axial_rope(x, cos, sin):
    B_, S_, H, d = x.shape
    xa = x.astype(jnp.float32).reshape(B_, S_, H, 2, 2, d // 4)
    x1, x2 = xa[..., 0, :], xa[..., 1, :]
    cs, sn = cos[:, None], sin[:, None]
    out = jnp.stack([x1 * cs - x2 * sn, x2 * cs + x1 * sn], axis=-2)
    return out.reshape(B_, S_, H, d).astype(x.dtype)


def _gla_chunk_scan(q, k, v, log_f, s0):
    B_, L, H, _ = q.shape
    n = L // HG_CHUNK

    def to_chunks(a):
        return a.reshape(B_, n, HG_CHUNK, H, a.shape[-1]).transpose(1, 0, 3, 2, 4)

    qc, kc, vc, gc = (to_chunks(a) for a in (q, k, v, log_f))
    incl = jnp.tril(jnp.ones((HG_CHUNK, HG_CHUNK), dtype=bool))

    def step(s, inp):
        qi, ki, vi, gi = inp
        G = jnp.cumsum(gi, axis=2)
        inter = jnp.einsum('bhtk,bhkv->bhtv', qi * jnp.exp(G), s)
        diff = G[:, :, :, None, :] - G[:, :, None, :, :]
        decay = jnp.exp(jnp.where(incl[:, :, None], diff, -jnp.inf))
        attn = jnp.einsum('bhtk,bhtsk,bhsk->bhts', qi, decay, ki)
        intra = jnp.einsum('bhts,bhsv->bhtv', attn, vi)
        G_last = G[:, :, -1:, :]
        s_new = jnp.exp(G_last[:, :, 0, :])[..., None] * s + jnp.einsum(
            'bhsk,bhsv->bhkv', ki * jnp.exp(G_last - G), vi)
        return s_new, inter + intra

    s_fin, out = lax.scan(step, s0, (qc, kc, vc, gc))
    out = out.transpose(1, 0, 3, 2, 4).reshape(B_, L, H, v.shape[-1])
    return out, s_fin


def _hgrn2_direction(q_raw, f_raw, i_raw, lb, s0, reverse):
    B_, L, _ = q_raw.shape

    def heads(a):
        return a.astype(jnp.float32).reshape(B_, L, HG_HEADS, -1)

    q = heads(jax.nn.silu(q_raw.astype(jnp.float32)))
    g = lb + (1.0 - lb) * jax.nn.sigmoid(f_raw.astype(jnp.float32))
    k = heads(1.0 - g)
    log_f = heads(jnp.log(g))
    v = heads(i_raw)
    if reverse:
        q, k, v, log_f = (jnp.flip(a, axis=1) for a in (q, k, v, log_f))
    o, s = _gla_chunk_scan(q, k, v, log_f, s0)
    if reverse:
        o = jnp.flip(o, axis=1)
    return o, s


def _hgrn2_readout(o, g_raw, w):
    B_, L, _ = g_raw.shape
    y = _rms_norm(o, w).reshape(B_, L, HG_W) * jax.nn.silu(g_raw.astype(jnp.float32))
    return y.astype(g_raw.dtype)


def _hgrn2_mixer(z_lat, z_ctx, lb_f, lb_b, onorm_w, need_ctx):
    q_l, ff_l, fb_l, i_l, g_l = z_lat
    q_c, ff_c, fb_c, i_c, g_c = z_ctx
    s0 = jnp.zeros((q_l.shape[0], HG_HEADS, HG_DK, HG_DV), jnp.float32)
    oc_f, s_f = _hgrn2_direction(q_c, ff_c, i_c, lb_f, s0, False)
    ol_f, _ = _hgrn2_direction(q_l, ff_l, i_l, lb_f, s_f, False)
    oc_b, s_b = _hgrn2_direction(q_c, fb_c, i_c, lb_b, s0, True)
    ol_b, _ = _hgrn2_direction(q_l, fb_l, i_l, lb_b, s_b, True)
    y_lat = _hgrn2_readout(ol_f + ol_b, g_l, onorm_w)
    y_ctx = _hgrn2_readout(oc_f + oc_b, g_c, onorm_w) if need_ctx else None
    return y_lat, y_ctx


def _natten_mixer(z_lat, z_ctx, qn_w, kn_w, rpb, rope, need_ctx):
    q_l, k_l, v_l = z_lat
    q_c, k_c, v_c = z_ctx
    B_, S_, _ = q_l.shape
    rows = S_ // GRID_W
    wr = min(NA_WIN_R, rows)
    scale = NA_HD ** -0.5
    cos, sin = rope

    def heads(a):
        return a.reshape(a.shape[0], a.shape[1], NA_HEADS, NA_HD)

    ql = _apply_axial_rope(_rms_norm(heads(q_l), qn_w), cos, sin)
    kl = _apply_axial_rope(_rms_norm(heads(k_l), kn_w), cos, sin)
    vl = heads(v_l)
    kc_h = _rms_norm(heads(k_c), kn_w).transpose(0, 2, 1, 3)
    vc_h = heads(v_c).transpose(0, 2, 1, 3)

    q_rows = ql.reshape(B_, rows, GRID_W, NA_HEADS, NA_HD).transpose(1, 0, 3, 2, 4)
    k_grid = kl.reshape(B_, rows, GRID_W, NA_HEADS, NA_HD).transpose(0, 3, 1, 2, 4)
    v_grid = vl.reshape(B_, rows, GRID_W, NA_HEADS, NA_HD).transpose(0, 3, 1, 2, 4)

    cols = np.arange(GRID_W)
    col_start = np.clip(cols - NA_WIN_C // 2, 0, GRID_W - NA_WIN_C)
    col_idx = col_start[:, None] + np.arange(NA_WIN_C)
    col_bias_ix = col_idx - cols[:, None] + (NA_WIN_C - 1)
    n_loc = wr * NA_WIN_C

    def row_attn(args):
        r, q_r = args
        r0 = jnp.clip(r - wr // 2, 0, rows - wr)
        k_band = lax.dynamic_slice_in_dim(k_grid, r0, wr, axis=2)
        v_band = lax.dynamic_slice_in_dim(v_grid, r0, wr, axis=2)
        k_win = k_band[:, :, :, col_idx]
        v_win = v_band[:, :, :, col_idx]
        s_loc = jnp.einsum('bhqd,bhrqcd->bhqrc', q_r, k_win) * scale
        row_bias_ix = r0 + jnp.arange(wr) - r + (NA_WIN_R - 1)
        bias = rpb[:, row_bias_ix[None, :, None], col_bias_ix[:, None, :]]
        s_loc = s_loc + bias[None]
        s_ctx = jnp.einsum('bhqd,bhkd->bhqk', q_r, kc_h) * scale
        logits = jnp.concatenate([s_loc.reshape(B_, NA_HEADS, GRID_W, n_loc), s_ctx], axis=-1)
        p = jax.nn.softmax(logits.astype(jnp.float32), axis=-1).astype(v_win.dtype)
        p_loc = p[..., :n_loc].reshape(B_, NA_HEADS, GRID_W, wr, NA_WIN_C)
        p_ctx = p[..., n_loc:]
        return (jnp.einsum('bhqrc,bhrqcd->bhqd', p_loc, v_win)
                + jnp.einsum('bhqk,bhkd->bhqd', p_ctx, vc_h))

    o = lax.map(row_attn, (jnp.arange(rows), q_rows))
    y_lat = o.transpose(1, 0, 3, 2, 4).reshape(B_, S_, NA_W)

    y_ctx = None
    if need_ctx:
        qc_h = _rms_norm(heads(q_c), qn_w).transpose(0, 2, 1, 3)
        sc = jnp.einsum('bhqd,bhkd->bhqk', qc_h, kc_h) * scale
        pc = jax.nn.softmax(sc.astype(jnp.float32), axis=-1).astype(vc_h.dtype)
        oc = jnp.einsum('bhqk,bhkd->bhqd', pc, vc_h)
        y_ctx = oc.transpose(0, 2, 1, 3).reshape(q_c.shape[0], q_c.shape[1], NA_W)
    return y_lat, y_ctx


def _short_conv(b_gate, c_gate, x_in, w):
    u = c_gate * x_in
    up = jnp.pad(u, ((0, 0), (1, 1), (0, 0)))
    y = up[:, :-2] * w[0] + up[:, 1:-1] * w[1] + up[:, 2:] * w[2]
    return b_gate * y


def _branch_merge(ya, yb, yc, gates, wa, wb, wc, wo):
    ga, gb, gc = (jax.nn.sigmoid(g) for g in gates)
    mix = ga * (ya @ wa) + gb * (yb @ wb) + gc * (yc @ wc)
    return mix @ wo


def _sq_relu_mlp(h, w1, w2):
    return jnp.square(jax.nn.relu(h @ w1)) @ w2


def setup_inputs(seed: int = 0) -> dict:
    key = jax.random.key(seed)
    ks = jax.random.split(key, 24)

    def nrm(k, shape, scale):
        return jax.random.normal(k, shape, jnp.float32) * scale

    D = D_MODEL
    return {
        "x": nrm(ks[0], (BATCH, SEQ, D), 1.0),
        "c": nrm(ks[1], (BATCH, D), 1.0),
        "ctx": nrm(ks[2], (BATCH, CTX_LEN, D), 1.0),
        "c_ctx": nrm(ks[3], (D,), 1.0),
        "ada_w": nrm(ks[4], (DEPTH, D, N_MOD * D), 0.25 * D ** -0.5),
        "ada_b": nrm(ks[5], (DEPTH, N_MOD * D), 0.02),
        "norm1_w": 1.0 + nrm(ks[6], (DEPTH, D), 0.02),
        "norm2_w": 1.0 + nrm(ks[7], (DEPTH, D), 0.02),
        "w_in": nrm(ks[8], (DEPTH, D, PROJ_W), D ** -0.5),
        "hgrn_lb_logits": nrm(ks[9], (2, DEPTH, HG_KW), 0.5),
        "hgrn_onorm_w": 1.0 + nrm(ks[10], (DEPTH, HG_DV), 0.02),
        "q_norm_w": 1.0 + nrm(ks[11], (DEPTH, NA_HD), 0.02),
        "k_norm_w": 1.0 + nrm(ks[12], (DEPTH, NA_HD), 0.02),
        "natten_rpb": nrm(ks[13], (DEPTH, NA_HEADS, 2 * NA_WIN_R - 1, 2 * NA_WIN_C - 1), 0.1),
        "conv_w": nrm(ks[14], (DEPTH, SC_K, SC_W), SC_K ** -0.5),
        "w_branch_a": nrm(ks[15], (DEPTH, HG_W, D), HG_W ** -0.5),
        "w_branch_b": nrm(ks[16], (DEPTH, NA_W, D), NA_W ** -0.5),
        "w_branch_c": nrm(ks[17], (DEPTH, SC_W, D), SC_W ** -0.5),
        "w_out": nrm(ks[18], (DEPTH, D, D), D ** -0.5),
        "mlp_w1": nrm(ks[19], (DEPTH, D, D_FF), D ** -0.5),
        "mlp_w2": nrm(ks[20], (DEPTH, D_FF, D), D_FF ** -0.5),
    }


def reference(x, c, ctx, c_ctx, ada_w, ada_b, norm1_w, norm2_w, w_in, hgrn_lb_logits,
              hgrn_onorm_w, q_norm_w, k_norm_w, natten_rpb, conv_w, w_branch_a, w_branch_b,
              w_branch_c, w_out, mlp_w1, mlp_w2):
    S_ = x.shape[1]
    rope = _axial_rope_tables(S_)
    lb_p = jax.nn.softmax(hgrn_lb_logits.astype(jnp.float32), axis=1)
    lower_bounds = jnp.cumsum(lb_p, axis=1) - lb_p[:, :1]
    silu_c = jax.nn.silu(c)
    silu_cc = jax.nn.silu(c_ctx)
    h_ctx = ctx
    for l in range(DEPTH):
        need_ctx = l < DEPTH - 1
        mod_lat = jnp.split((silu_c @ ada_w[l] + ada_b[l])[:, None, :], N_MOD, axis=-1)
        mod_ctx = jnp.split(silu_cc @ ada_w[l] + ada_b[l], N_MOD, axis=-1)
        z_lat = jnp.split(_modulate(x, norm1_w[l], mod_lat[0], mod_lat[1]) @ w_in[l], SPLIT_POINTS, axis=-1)
        z_ctx = jnp.split(_modulate(h_ctx, norm1_w[l], mod_ctx[0], mod_ctx[1]) @ w_in[l], SPLIT_POINTS, axis=-1)

        ya_lat, ya_ctx = _hgrn2_mixer(z_lat[0:5], z_ctx[0:5], lower_bounds[0, l], lower_bounds[1, l],
                                      hgrn_onorm_w[l], need_ctx)
        yb_lat, yb_ctx = _natten_mixer(z_lat[5:8], z_ctx[5:8], q_norm_w[l], k_norm_w[l],
                                       natten_rpb[l], rope, need_ctx)
        yc_lat = _short_conv(z_lat[8], z_lat[9], z_lat[10], conv_w[l])

        x = x + mod_lat[2] * _branch_merge(ya_lat, yb_lat, yc_lat, z_lat[11:14], w_branch_a[l],
                                           w_branch_b[l], w_branch_c[l], w_out[l])
        x = x + mod_lat[5] * _sq_relu_mlp(_modulate(x, norm2_w[l], mod_lat[3], mod_lat[4]),
                                          mlp_w1[l], mlp_w2[l])
        if need_ctx:
            yc_ctx = _short_conv(z_ctx[8], z_ctx[9], z_ctx[10], conv_w[l])
            h_ctx = h_ctx + mod_ctx[2] * _branch_merge(ya_ctx, yb_ctx, yc_ctx, z_ctx[11:14], w_branch_a[l],
                                                       w_branch_b[l], w_branch_c[l], w_out[l])
            h_ctx = h_ctx + mod_ctx[5] * _sq_relu_mlp(_modulate(h_ctx, norm2_w[l], mod_ctx[3], mod_ctx[4]),
                                                      mlp_w1[l], mlp_w2[l])
    return x
```

```python
import functools

import numpy as np
import jax
import jax.numpy as jnp
from jax import lax
from jax.experimental import pallas as pl
from jax.experimental.pallas import tpu as pltpu

GRID_W = 64
HG_HEADS = 4
HG_DK = 128
HG_W = HG_HEADS * HG_DK
HG_CHUNK = 64
HG_LEVELS = 6
NA_HEADS = 8
NA_HD = 64
NA_W = NA_HEADS * NA_HD
NA_WIN_R = 8
NA_WIN_C = 16
ROPE_THETA = 10000.0
SC_W = 512
N_MOD = 6
EPS = 1e-6
NEG_BIG = -1e30

LANES = 128
SUBLANES = 8
BF16_ROWS = 16
VMEM_LIMIT = 56 * 1024 * 1024

UNIT = 512
COL_GA, COL_GB, COL_GC = 0, 2, 4
COL_HQ, COL_HFF, COL_HFB, COL_HI, COL_HG = 6, 7, 8, 9, 10
COL_NQ, COL_NK, COL_NV = 11, 12, 13
COL_CB, COL_CC, COL_CX = 14, 15, 16
PROJ_UNITS = 17

F32 = jnp.float32
BF16 = jnp.bfloat16


def _pick_tile(n, target, mult):
    best = None
    for t in range(mult, min(n, target) + 1, mult):
        if n % t == 0:
            best = t
    if best is None:
        raise ValueError(f"no tile for {n} (target {target}, multiple {mult})")
    return best


def _sigmoid(x):
    return 1.0 / (1.0 + jnp.exp(-x))


def _params(*sem):
    return pltpu.CompilerParams(dimension_semantics=sem, vmem_limit_bytes=VMEM_LIMIT)


def _ada_kernel(c_ref, w_ref, b_ref, o_ref):
    c = c_ref[...]
    s = c * _sigmoid(c)
    o_ref[0] = jnp.dot(s, w_ref[0], precision=lax.Precision.HIGHEST,
                       preferred_element_type=F32) + b_ref[0]


def _ada_call(cvec, ada_w, ada_b):
    depth, d, n = ada_w.shape
    rows = cvec.shape[0]
    tn = _pick_tile(n, 1024, LANES)
    return pl.pallas_call(
        _ada_kernel,
        grid=(depth, n // tn),
        in_specs=[
            pl.BlockSpec((rows, d), lambda l, j: (0, 0)),
            pl.BlockSpec((1, d, tn), lambda l, j: (l, 0, j)),
            pl.BlockSpec((1, 1, tn), lambda l, j: (l, 0, j)),
        ],
        out_specs=pl.BlockSpec((1, rows, tn), lambda l, j: (l, 0, j)),
        out_shape=jax.ShapeDtypeStruct((depth, rows, n), F32),
        compiler_params=_params("arbitrary", "arbitrary"),
    )(cvec, ada_w, ada_b.reshape(depth, 1, n))


def _row_mod(mod_ref, idx, is_ctx, d):
    mc = mod_ref[0, 0:1, idx * d:(idx + 1) * d]
    ml = mod_ref[0, 1:2, idx * d:(idx + 1) * d]
    return jnp.where(is_ctx, mc, ml)


def _modulated_norm(x, nw, shift, scale):
    ms = jnp.mean(x * x, axis=-1, keepdims=True)
    return (x * lax.rsqrt(ms + EPS) * nw) * (1.0 + scale) + shift


def _inproj_kernel(x_ref, mod_ref, nw_ref, w_ref, z_ref, xn_ref, *, ctx_len, tm, d):
    ti = pl.program_id(1)

    @pl.when(pl.program_id(2) == 0)
    def _():
        row = ti * tm + lax.broadcasted_iota(jnp.int32, (tm, 1), 0)
        is_ctx = row < ctx_len
        shift = _row_mod(mod_ref, 0, is_ctx, d)
        scale = _row_mod(mod_ref, 1, is_ctx, d)
        xn_ref[...] = _modulated_norm(x_ref[0], nw_ref[...], shift, scale).astype(BF16)

    z_ref[0] = jnp.dot(xn_ref[...], w_ref[...], preferred_element_type=F32).astype(z_ref.dtype)


def _inproj_call(xs, mod, nw, w, ctx_len):
    b, l, d = xs.shape
    n = w.shape[1]
    tm = _pick_tile(l, 544, BF16_ROWS)
    tn = _pick_tile(n, 2176, LANES)
    return pl.pallas_call(
        functools.partial(_inproj_kernel, ctx_len=ctx_len, tm=tm, d=d),
        grid=(b, l // tm, n // tn),
        in_specs=[
            pl.BlockSpec((1, tm, d), lambda bi, ti, ni: (bi, ti, 0)),
            pl.BlockSpec((1, 2, N_MOD * d), lambda bi, ti, ni: (bi, 0, 0)),
            pl.BlockSpec((1, d), lambda bi, ti, ni: (0, 0)),
            pl.BlockSpec((d, tn), lambda bi, ti, ni: (0, ni)),
        ],
        out_specs=pl.BlockSpec((1, tm, tn), lambda bi, ti, ni: (bi, ti, ni)),
        out_shape=jax.ShapeDtypeStruct((b, l, n), BF16),
        scratch_shapes=[pltpu.VMEM((tm, d), BF16)],
        compiler_params=_params("arbitrary", "arbitrary", "arbitrary"),
    )(xs, mod, nw, w)


def _boundary_rows(gc, h, reverse):
    t, w = gc.shape
    off = h if reverse else h - 1
    if 2 * h >= SUBLANES:
        g3 = gc.reshape(t // (2 * h), 2 * h, w)
        return jnp.broadcast_to(g3[:, off:off + 1, :], g3.shape).reshape(t, w)
    g3 = gc.reshape(t // SUBLANES, SUBLANES, w)
    sub = lax.broadcasted_iota(jnp.int32, (1, SUBLANES, 1), 1)
    out = None
    for start in range(0, SUBLANES, 2 * h):
        piece = jnp.broadcast_to(g3[:, start + off:start + off + 1, :], g3.shape)
        out = piece if out is None else jnp.where(sub >= start, piece, out)
    return out.reshape(t, w)


def _hgrn_kernel(*refs, reverse, readout):
    if readout:
        q_ref, f_ref, i_ref, lb_ref, of_ref, g_ref, onw_ref, o_ref, st_ref = refs
    else:
        q_ref, f_ref, i_ref, lb_ref, o_ref, st_ref = refs
    t = HG_CHUNK

    @pl.when(pl.program_id(1) == 0)
    def _():
        st_ref[...] = jnp.zeros_like(st_ref)

    q = q_ref[0].astype(F32)
    f = f_ref[0].astype(F32)
    v = i_ref[0].astype(F32)
    lb = lb_ref[...]
    g = lb + (1.0 - lb) * _sigmoid(f)
    kk = 1.0 - g
    lf = jnp.log(g)
    qs = q * _sigmoid(q)

    row = lax.broadcasted_iota(jnp.int32, (t, 1), 0)
    gc = lf
    for lvl in range(HG_LEVELS):
        dist = 1 << lvl
        if reverse:
            gc = gc + jnp.where(row < t - dist, pltpu.roll(gc, t - dist, 0), 0.0)
        else:
            gc = gc + jnp.where(row >= dist, pltpu.roll(gc, dist, 0), 0.0)

    rt = lax.broadcasted_iota(jnp.int32, (t, t), 0)
    rs = lax.broadcasted_iota(jnp.int32, (t, t), 1)
    qbit = 0 if reverse else 1

    def nt_dot(a, b):
        return lax.dot_general(a, b, (((1,), (1,)), ((), ())), preferred_element_type=F32)

    def head(a, hh):
        return a[:, hh * HG_DK:(hh + 1) * HG_DK]

    qs_b = qs.astype(BF16)
    kk_b = kk.astype(BF16)
    attn = [jnp.where(rt == rs, nt_dot(head(qs_b, hh), head(kk_b, hh)), 0.0)
            for hh in range(HG_HEADS)]
    for lvl in range(HG_LEVELS):
        h = 1 << lvl
        is_q = ((row >> lvl) & 1) == qbit
        gb = _boundary_rows(gc, h, reverse)
        e = jnp.exp(jnp.where(is_q, gc - gb, gb - gc))
        qh = jnp.where(is_q, qs * e, 0.0).astype(BF16)
        kh = jnp.where(is_q, 0.0, kk * e).astype(BF16)
        mask = ((rt >> (lvl + 1)) == (rs >> (lvl + 1))) & (((rt >> lvl) & 1) == qbit) \
            & (((rs >> lvl) & 1) != qbit)
        for hh in range(HG_HEADS):
            attn[hh] = attn[hh] + jnp.where(mask, nt_dot(head(qh, hh), head(kh, hh)), 0.0)

    g_end = gc[0:1, :] if reverse else gc[t - 1:t, :]
    qd = (qs * jnp.exp(gc)).astype(BF16)
    kd = (kk * jnp.exp(g_end - gc)).astype(BF16)
    d_end = jnp.exp(g_end)
    v_b = v.astype(BF16)

    outs = []
    for hh in range(HG_HEADS):
        st = st_ref[hh]
        vh = head(v_b, hh)
        o_h = nt_dot(head(qd, hh), st.astype(BF16))
        o_h = o_h + jnp.dot(attn[hh].astype(BF16), vh, preferred_element_type=F32)
        vt = head(v, hh).T.astype(BF16)
        st_ref[hh] = st * head(d_end, hh) + jnp.dot(vt, head(kd, hh), preferred_element_type=F32)
        outs.append(o_h)

    if not readout:
        for hh in range(HG_HEADS):
            o_ref[0, :, hh * HG_DK:(hh + 1) * HG_DK] = outs[hh]
        return

    gate = g_ref[0].astype(F32)
    gate = gate * _sigmoid(gate)
    onw = onw_ref[...]
    for hh in range(HG_HEADS):
        o_h = outs[hh] + of_ref[0, :, hh * HG_DK:(hh + 1) * HG_DK]
        ms = jnp.mean(o_h * o_h, axis=-1, keepdims=True)
        y = o_h * lax.rsqrt(ms + EPS) * onw
        o_ref[0, :, hh * HG_DK:(hh + 1) * HG_DK] = (y * head(gate, hh)).astype(o_ref.dtype)


def _hgrn_call(z, lb, ctx_len, reverse, o_fwd=None, onw=None):
    b, l, _ = z.shape
    t = HG_CHUNK
    n_chunks = l // t
    n_ctx = ctx_len // t
    readout = o_fwd is not None

    if reverse:
        def chunk(i):
            return jnp.where(i < n_ctx, n_ctx - 1 - i, n_chunks - 1 + n_ctx - i)
    else:
        def chunk(i):
            return i

    def zspec(col):
        return pl.BlockSpec((1, t, UNIT), lambda bi, i: (bi, chunk(i), col))

    row_spec = pl.BlockSpec((1, t, HG_W), lambda bi, i: (bi, chunk(i), 0))
    in_specs = [zspec(COL_HQ), zspec(COL_HFB if reverse else COL_HFF), zspec(COL_HI),
                pl.BlockSpec((1, HG_W), lambda bi, i: (0, 0))]
    args = [z, z, z, lb]
    if readout:
        in_specs += [row_spec, zspec(COL_HG), pl.BlockSpec((1, HG_DK), lambda bi, i: (0, 0))]
        args += [o_fwd, z, onw]
    return pl.pallas_call(
        functools.partial(_hgrn_kernel, reverse=reverse, readout=readout),
        grid=(b, n_chunks),
        in_specs=in_specs,
        out_specs=row_spec,
        out_shape=jax.ShapeDtypeStruct((b, l, HG_W), BF16 if readout else F32),
        scratch_shapes=[pltpu.VMEM((HG_HEADS, HG_DK, HG_DK), F32)],
        compiler_params=_params("arbitrary", "arbitrary"),
    )(*args)


def _group_mean_sq(x, bd):
    sq = x * x
    hi = sq.astype(BF16)
    lo = (sq - hi.astype(F32)).astype(BF16)
    s = jnp.dot(hi, bd, preferred_element_type=F32) + jnp.dot(lo, bd, preferred_element_type=F32)
    return s * (1.0 / NA_HD)


def _rotate(x, cos, sin_signed, first_half):
    w = x.shape[-1]
    quarter = NA_HD // 4
    partner = jnp.where(first_half, pltpu.roll(x, w - quarter, 1), pltpu.roll(x, quarter, 1))
    return x * cos + partner * sin_signed


def _qkprep_kernel(q_ref, k_ref, v_ref, cos_ref, sin_ref, qw_ref, kw_ref, bd_ref,
                   qo_ref, ko_ref, vo_ref):
    bd = bd_ref[...]
    cos = cos_ref[...]
    sin = sin_ref[...]
    lane = lax.broadcasted_iota(jnp.int32, (1, NA_W), 1)
    first_half = (lane % (NA_HD // 2)) < (NA_HD // 4)

    def prep(ref, w):
        x = ref[0].astype(F32)
        y = x * lax.rsqrt(_group_mean_sq(x, bd) + EPS) * w
        return _rotate(y, cos, sin, first_half)

    qo_ref[0] = (prep(q_ref, qw_ref[...]) * (NA_HD ** -0.5)).astype(BF16)
    ko_ref[0] = prep(k_ref, kw_ref[...]).astype(BF16)
    vo_ref[0] = v_ref[0].astype(BF16)


def _qkprep_call(z, cos, sin, qw, kw, bd):
    b, l, _ = z.shape
    tm = _pick_tile(l, 544, BF16_ROWS)

    def zspec(col):
        return pl.BlockSpec((1, tm, UNIT), lambda ti, bi: (bi, ti, col))

    tab = pl.BlockSpec((tm, NA_W), lambda ti, bi: (ti, 0))
    vec = pl.BlockSpec((1, NA_W), lambda ti, bi: (0, 0))
    out = pl.BlockSpec((1, tm, NA_W), lambda ti, bi: (bi, ti, 0))
    shp = jax.ShapeDtypeStruct((b, l, NA_W), BF16)
    return pl.pallas_call(
        _qkprep_kernel,
        grid=(l // tm, b),
        in_specs=[zspec(COL_NQ), zspec(COL_NK), zspec(COL_NV), tab, tab, vec, vec,
                  pl.BlockSpec((NA_W, NA_W), lambda ti, bi: (0, 0))],
        out_specs=[out, out, out],
        out_shape=[shp, shp, shp],
        compiler_params=_params("arbitrary", "arbitrary"),
    )(z, z, z, cos, sin, qw, kw, bd)


def _natten_kernel(q_ref, k_ref, v_ref, bm_ref, o_ref, *, ctx_len, rows):
    i = pl.program_id(1)
    n_ctx_blocks = ctx_len // GRID_W
    band = NA_WIN_R * GRID_W
    q = q_ref[0]

    def nt_dot(a, b):
        return lax.dot_general(a, b, (((1,), (1,)), ((), ())), preferred_element_type=F32)

    def head(a, hh):
        return a[:, hh * NA_HD:(hh + 1) * NA_HD]

    kc = k_ref[0, 0:ctx_len, :]
    vc = v_ref[0, 0:ctx_len, :]

    @pl.when(i < n_ctx_blocks)
    def _():
        outs = []
        for hh in range(NA_HEADS):
            s = nt_dot(head(q, hh), head(kc, hh))
            m = jnp.max(s, axis=-1, keepdims=True)
            p = jnp.exp(s - m)
            den = jnp.sum(p, axis=-1, keepdims=True)
            o = jnp.dot(p.astype(BF16), head(vc, hh), preferred_element_type=F32)
            outs.append(o / den)
        o_ref[0] = jnp.concatenate(outs, axis=-1).astype(o_ref.dtype)

    @pl.when(i >= n_ctx_blocks)
    def _():
        r = i - n_ctx_blocks
        r0 = jnp.clip(r - NA_WIN_R // 2, 0, rows - NA_WIN_R)
        start = pl.multiple_of(ctx_len + r0 * GRID_W, GRID_W)
        kb = k_ref[0, pl.ds(start, band), :]
        vb = v_ref[0, pl.ds(start, band), :]
        outs = []
        for hh in range(NA_HEADS):
            qh = head(q, hh)
            s_loc = nt_dot(qh, head(kb, hh)) + bm_ref[0, hh]
            s_ctx = nt_dot(qh, head(kc, hh))
            m = jnp.maximum(jnp.max(s_loc, axis=-1, keepdims=True),
                            jnp.max(s_ctx, axis=-1, keepdims=True))
            p_loc = jnp.exp(s_loc - m)
            p_ctx = jnp.exp(s_ctx - m)
            den = jnp.sum(p_loc, axis=-1, keepdims=True) + jnp.sum(p_ctx, axis=-1, keepdims=True)
            o = jnp.dot(p_loc.astype(BF16), head(vb, hh), preferred_element_type=F32)
            o = o + jnp.dot(p_ctx.astype(BF16), head(vc, hh), preferred_element_type=F32)
            outs.append(o / den)
        o_ref[0] = jnp.concatenate(outs, axis=-1).astype(o_ref.dtype)


def _natten_call(qn, kn, vb, bias, ctx_len):
    b, l, _ = qn.shape
    rows = (l - ctx_len) // GRID_W
    n_ctx_blocks = ctx_len // GRID_W

    def pattern(i):
        r = jnp.maximum(i - n_ctx_blocks, 0)
        return r - jnp.clip(r - NA_WIN_R // 2, 0, rows - NA_WIN_R)

    full = pl.BlockSpec((1, l, NA_W), lambda bi, i: (bi, 0, 0))
    return pl.pallas_call(
        functools.partial(_natten_kernel, ctx_len=ctx_len, rows=rows),
        grid=(b, l // GRID_W),
        in_specs=[
            pl.BlockSpec((1, GRID_W, NA_W), lambda bi, i: (bi, i, 0)),
            full, full,
            pl.BlockSpec((1, NA_HEADS, GRID_W, NA_WIN_R * GRID_W),
                         lambda bi, i: (pattern(i), 0, 0, 0)),
        ],
        out_specs=pl.BlockSpec((1, GRID_W, NA_W), lambda bi, i: (bi, i, 0)),
        out_shape=jax.ShapeDtypeStruct((b, l, NA_W), BF16),
        compiler_params=_params("arbitrary", "arbitrary"),
    )(qn, kn, vb, bias)


def _bias_table(rpb):
    p = np.arange(NA_WIN_R)[:, None, None, None]
    qc = np.arange(GRID_W)[None, :, None, None]
    j = np.arange(NA_WIN_R)[None, None, :, None]
    kc = np.arange(GRID_W)[None, None, None, :]
    c0 = np.clip(qc - NA_WIN_C // 2, 0, GRID_W - NA_WIN_C)
    valid = (kc >= c0) & (kc < c0 + NA_WIN_C)
    ri = np.clip(j - p + NA_WIN_R - 1, 0, 2 * NA_WIN_R - 2)
    ci = np.clip(kc - qc + NA_WIN_C - 1, 0, 2 * NA_WIN_C - 2)
    ri, ci, valid = np.broadcast_arrays(ri, ci, valid)
    tab = rpb[:, ri, ci]
    tab = jnp.where(valid[None], tab, NEG_BIG)
    tab = tab.transpose(1, 0, 2, 3, 4)
    return tab.reshape(NA_WIN_R, NA_HEADS, GRID_W, NA_WIN_R * GRID_W).astype(F32)


def _rope_tables(ctx_len, s):
    tpos = np.arange(s)
    pos = np.stack([tpos // GRID_W, tpos % GRID_W], axis=-1).astype(np.float32)
    half = NA_HD // 2
    inv = (ROPE_THETA ** (-jnp.arange(0, half, 2, dtype=F32) / half))
    ang = jnp.asarray(pos)[:, :, None] * inv
    cos = jnp.cos(ang)
    sin = jnp.sin(ang)
    cos_h = jnp.concatenate([cos, cos], axis=-1).reshape(s, NA_HD)
    sin_h = jnp.concatenate([-sin, sin], axis=-1).reshape(s, NA_HD)
    cos_t = jnp.tile(cos_h, (1, NA_HEADS))
    sin_t = jnp.tile(sin_h, (1, NA_HEADS))
    cos_t = jnp.concatenate([jnp.ones((ctx_len, NA_W), F32), cos_t], axis=0)
    sin_t = jnp.concatenate([jnp.zeros((ctx_len, NA_W), F32), sin_t], axis=0)
    return cos_t, sin_t


def _merge_kernel(x_ref, mod_ref, ya_ref, yb_ref, cb_ref, cc_ref, cx_ref,
                  ccp_ref, cxp_ref, ccn_ref, cxn_ref, ga_ref, gb_ref, gc_ref,
                  cw_ref, wa_ref, wb_ref, wc_ref, wo_ref, o_ref, *, ctx_len, seq_len, tm, d):
    ti = pl.program_id(1)
    row = ti * tm + lax.broadcasted_iota(jnp.int32, (tm, 1), 0)
    is_ctx = row < ctx_len
    local = lax.broadcasted_iota(jnp.int32, (tm, 1), 0)

    u = cc_ref[0].astype(F32) * cx_ref[0].astype(F32)
    u_before = ccp_ref[0, BF16_ROWS - 1:BF16_ROWS, :].astype(F32) * cxp_ref[0, BF16_ROWS - 1:BF16_ROWS, :].astype(F32)
    u_after = ccn_ref[0, 0:1, :].astype(F32) * cxn_ref[0, 0:1, :].astype(F32)
    up = jnp.where(local == 0, u_before, pltpu.roll(u, 1, 0))
    un = jnp.where(local == tm - 1, u_after, pltpu.roll(u, tm - 1, 0))
    has_prev = (row != 0) & (row != ctx_len)
    has_next = (row != ctx_len - 1) & (row != seq_len - 1)
    cw = cw_ref[...]
    conv = (jnp.where(has_prev, up, 0.0) * cw[0:1, :] + u * cw[1:2, :]
            + jnp.where(has_next, un, 0.0) * cw[2:3, :])
    yc = (cb_ref[0].astype(F32) * conv).astype(BF16)

    mix = _sigmoid(ga_ref[0].astype(F32)) * jnp.dot(ya_ref[0], wa_ref[...], preferred_element_type=F32)
    mix = mix + _sigmoid(gb_ref[0].astype(F32)) * jnp.dot(yb_ref[0], wb_ref[...], preferred_element_type=F32)
    mix = mix + _sigmoid(gc_ref[0].astype(F32)) * jnp.dot(yc, wc_ref[...], preferred_element_type=F32)
    out = jnp.dot(mix.astype(BF16), wo_ref[...], preferred_element_type=F32)
    o_ref[0] = x_ref[0] + _row_mod(mod_ref, 2, is_ctx, d) * out


def _merge_call(xs, mod, ya, yb, z, cw, wa, wb, wc, wo, ctx_len):
    b, l, d = xs.shape
    tm = _pick_tile(l, 544, BF16_ROWS)
    halo = tm // BF16_ROWS
    n_halo = l // BF16_ROWS

    def zspec(col, width=UNIT):
        return pl.BlockSpec((1, tm, width), lambda bi, ti: (bi, ti, col * UNIT // width))

    def prev(col):
        return pl.BlockSpec((1, BF16_ROWS, UNIT), lambda bi, ti: (bi, jnp.maximum(ti * halo - 1, 0), col))

    def nxt(col):
        return pl.BlockSpec((1, BF16_ROWS, UNIT),
                            lambda bi, ti: (bi, jnp.minimum((ti + 1) * halo, n_halo - 1), col))

    def const(shape):
        return pl.BlockSpec(shape, lambda bi, ti: tuple(0 for _ in shape))

    tok = pl.BlockSpec((1, tm, d), lambda bi, ti: (bi, ti, 0))
    half = pl.BlockSpec((1, tm, UNIT), lambda bi, ti: (bi, ti, 0))
    return pl.pallas_call(
        functools.partial(_merge_kernel, ctx_len=ctx_len, seq_len=l, tm=tm, d=d),
        grid=(b, l // tm),
        in_specs=[tok, pl.BlockSpec((1, 2, N_MOD * d), lambda bi, ti: (bi, 0, 0)), half, half,
                  zspec(COL_CB), zspec(COL_CC), zspec(COL_CX),
                  prev(COL_CC), prev(COL_CX), nxt(COL_CC), nxt(COL_CX),
                  zspec(COL_GA, d), zspec(COL_GB, d), zspec(COL_GC, d),
                  const(cw.shape), const(wa.shape), const(wb.shape), const(wc.shape), const(wo.shape)],
        out_specs=tok,
        out_shape=jax.ShapeDtypeStruct((b, l, d), F32),
        compiler_params=_params("arbitrary", "arbitrary"),
    )(xs, mod, ya, yb, z, z, z, z, z, z, z, z, z, z, cw, wa, wb, wc, wo)


def _mlp_kernel(x_ref, mod_ref, nw_ref, w1_ref, w2_ref, o_ref, *, ctx_len, tm, d, ff_chunk):
    ti = pl.program_id(1)
    row = ti * tm + lax.broadcasted_iota(jnp.int32, (tm, 1), 0)
    is_ctx = row < ctx_len
    x = x_ref[0]
    h = _modulated_norm(x, nw_ref[...], _row_mod(mod_ref, 3, is_ctx, d),
                        _row_mod(mod_ref, 4, is_ctx, d)).astype(BF16)
    acc = jnp.zeros((tm, d), F32)
    for c in range(w1_ref.shape[1] // ff_chunk):
        a = jnp.dot(h, w1_ref[:, c * ff_chunk:(c + 1) * ff_chunk], preferred_element_type=F32)
        a = jnp.maximum(a, 0.0)
        acc = acc + jnp.dot((a * a).astype(BF16), w2_ref[c * ff_chunk:(c + 1) * ff_chunk, :],
                            preferred_element_type=F32)
    o_ref[0] = x + _row_mod(mod_ref, 5, is_ctx, d) * acc


def _mlp_call(xs, mod, nw, w1, w2, ctx_len):
    b, l, d = xs.shape
    tm = _pick_tile(l, 544, BF16_ROWS)
    tok = pl.BlockSpec((1, tm, d), lambda bi, ti: (bi, ti, 0))
    return pl.pallas_call(
        functools.partial(_mlp_kernel, ctx_len=ctx_len, tm=tm, d=d, ff_chunk=1024),
        grid=(b, l // tm),
        in_specs=[tok, pl.BlockSpec((1, 2, N_MOD * d), lambda bi, ti: (bi, 0, 0)),
                  pl.BlockSpec((1, d), lambda bi, ti: (0, 0)),
                  pl.BlockSpec(w1.shape, lambda bi, ti: (0, 0), pipeline_mode=pl.Buffered(1)),
                  pl.BlockSpec(w2.shape, lambda bi, ti: (0, 0), pipeline_mode=pl.Buffered(1))],
        out_specs=tok,
        out_shape=jax.ShapeDtypeStruct((b, l, d), F32),
        compiler_params=_params("arbitrary", "arbitrary"),
    )(xs, mod, nw, w1, w2)


def _reorder_proj(w_in):
    hg = 5 * UNIT
    na = 3 * UNIT
    sc = 3 * UNIT
    gates = w_in[..., hg + na + sc:]
    return jnp.concatenate([gates, w_in[..., :hg + na + sc]], axis=-1)


def kernel(x, c, ctx, c_ctx, ada_w, ada_b, norm1_w, norm2_w, w_in, hgrn_lb_logits, hgrn_onorm_w,
           q_norm_w, k_norm_w, natten_rpb, conv_w, w_branch_a, w_branch_b, w_branch_c, w_out,
           mlp_w1, mlp_w2):
    b, s, d = x.shape
    ctx_len = ctx.shape[1]
    depth = ada_w.shape[0]
    assert s % GRID_W == 0 and s // GRID_W >= NA_WIN_R
    assert ctx_len % HG_CHUNK == 0 and s % HG_CHUNK == 0
    assert w_in.shape[-1] == PROJ_UNITS * UNIT

    lb_p = jax.nn.softmax(hgrn_lb_logits.astype(F32), axis=1)
    lower_bounds = jnp.cumsum(lb_p, axis=1) - lb_p[:, :1]

    pad_rows = -(b + 1) % SUBLANES
    cvec = jnp.concatenate([c_ctx[None, :], c, jnp.zeros((pad_rows, d), F32)], axis=0)
    mod_all = _ada_call(cvec, ada_w, ada_b)
    mod_sel = jnp.stack([jnp.broadcast_to(mod_all[:, 0:1], (depth, b, N_MOD * d)),
                         mod_all[:, 1:b + 1]], axis=2)

    cos_t, sin_t = _rope_tables(ctx_len, s)
    bd = jnp.asarray(np.kron(np.eye(NA_HEADS), np.ones((NA_HD, NA_HD))), BF16)
    w_in_b = _reorder_proj(w_in).astype(BF16)

    xs = jnp.concatenate([ctx, x], axis=1)
    for l in range(depth):
        mod = mod_sel[l]
        z = _inproj_call(xs, mod, norm1_w[l][None, :], w_in_b[l], ctx_len)
        o_f = _hgrn_call(z, lower_bounds[0, l][None, :], ctx_len, False)
        ya = _hgrn_call(z, lower_bounds[1, l][None, :], ctx_len, True, o_f, hgrn_onorm_w[l][None, :])
        qn, kn, vb = _qkprep_call(z, cos_t, sin_t, jnp.tile(q_norm_w[l], NA_HEADS)[None, :],
                                  jnp.tile(k_norm_w[l], NA_HEADS)[None, :], bd)
        yb = _natten_call(qn, kn, vb, _bias_table(natten_rpb[l]), ctx_len)
        xs = _merge_call(xs, mod, ya, yb, z, conv_w[l], w_branch_a[l].astype(BF16),
                         w_branch_b[l].astype(BF16), w_branch_c[l].astype(BF16),
                         w_out[l].astype(BF16), ctx_len)
        xs = _mlp_call(xs, mod, norm2_w[l][None, :], mlp_w1[l].astype(BF16),
                       mlp_w2[l].astype(BF16), ctx_len)
    return xs[:, ctx_len:, :]
```

```python
import functools

import numpy as np
import jax
import jax.numpy as jnp
from jax import lax
from jax.experimental import pallas as pl
from jax.experimental.pallas import tpu as pltpu

GRID_W = 64
HG_HEADS = 4
HG_DK = 128
HG_W = HG_HEADS * HG_DK
HG_CHUNK = 64
HG_LEVELS = 6
NA_HEADS = 8
NA_HD = 64
NA_W = NA_HEADS * NA_HD
NA_WIN_R = 8
NA_WIN_C = 16
ROPE_THETA = 10000.0
SC_W = 512
N_MOD = 6
EPS = 1e-6
NEG_BIG = -1e30

LANES = 128
SUBLANES = 8
BF16_ROWS = 16
VMEM_LIMIT = 56 * 1024 * 1024

UNIT = 512
COL_GA, COL_GB, COL_GC = 0, 2, 4
COL_HQ, COL_HFF, COL_HFB, COL_HI, COL_HG = 6, 7, 8, 9, 10
COL_NQ, COL_NK, COL_NV = 11, 12, 13
COL_CB, COL_CC, COL_CX = 14, 15, 16
PROJ_UNITS = 17

F32 = jnp.float32
BF16 = jnp.bfloat16


def _pick_tile(n, target, mult):
    best = None
    for t in range(mult, min(n, target) + 1, mult):
        if n % t == 0:
            best = t
    if best is None:
        raise ValueError(f"no tile for {n} (target {target}, multiple {mult})")
    return best


def _sigmoid(x):
    return 1.0 / (1.0 + jnp.exp(-x))


def _params(*sem):
    return pltpu.CompilerParams(dimension_semantics=sem, vmem_limit_bytes=VMEM_LIMIT)


def _ada_kernel(c_ref, w_ref, b_ref, o_ref):
    c = c_ref[...]
    s = c * _sigmoid(c)
    o_ref[0] = jnp.dot(s, w_ref[0], precision=lax.Precision.HIGHEST,
                       preferred_element_type=F32) + b_ref[0]


def _ada_call(cvec, ada_w, ada_b):
    depth, d, n = ada_w.shape
    rows = cvec.shape[0]
    tn = _pick_tile(n, 1024, LANES)
    return pl.pallas_call(
        _ada_kernel,
        grid=(depth, n // tn),
        in_specs=[
            pl.BlockSpec((rows, d), lambda l, j: (0, 0)),
            pl.BlockSpec((1, d, tn), lambda l, j: (l, 0, j)),
            pl.BlockSpec((1, 1, tn), lambda l, j: (l, 0, j)),
        ],
        out_specs=pl.BlockSpec((1, rows, tn), lambda l, j: (l, 0, j)),
        out_shape=jax.ShapeDtypeStruct((depth, rows, n), F32),
        compiler_params=_params("arbitrary", "arbitrary"),
    )(cvec, ada_w, ada_b.reshape(depth, 1, n))


def _row_mod(mod_ref, idx, is_ctx, d):
    mc = mod_ref[0, 0:1, idx * d:(idx + 1) * d]
    ml = mod_ref[0, 1:2, idx * d:(idx + 1) * d]
    return jnp.where(is_ctx, mc, ml)


def _modulated_norm(x, nw, shift, scale):
    ms = jnp.mean(x * x, axis=-1, keepdims=True)
    return (x * lax.rsqrt(ms + EPS) * nw) * (1.0 + scale) + shift


def _inproj_kernel(x_ref, mod_ref, nw_ref, w_ref, z_ref, xn_ref, *, ctx_len, tm, d):
    ti = pl.program_id(1)

    @pl.when(pl.program_id(2) == 0)
    def _():
        row = ti * tm + lax.broadcasted_iota(jnp.int32, (tm, 1), 0)
        is_ctx = row < ctx_len
        shift = _row_mod(mod_ref, 0, is_ctx, d)
        scale = _row_mod(mod_ref, 1, is_ctx, d)
        xn_ref[...] = _modulated_norm(x_ref[0], nw_ref[...], shift, scale).astype(BF16)

    z_ref[0] = jnp.dot(xn_ref[...], w_ref[...], preferred_element_type=F32).astype(z_ref.dtype)


def _inproj_call(xs, mod, nw, w, ctx_len):
    b, l, d = xs.shape
    n = w.shape[1]
    tm = _pick_tile(l, 544, BF16_ROWS)
    tn = _pick_tile(n, 2176, LANES)
    return pl.pallas_call(
        functools.partial(_inproj_kernel, ctx_len=ctx_len, tm=tm, d=d),
        grid=(b, l // tm, n // tn),
        in_specs=[
            pl.BlockSpec((1, tm, d), lambda bi, ti, ni: (bi, ti, 0)),
            pl.BlockSpec((1, 2, N_MOD * d), lambda bi, ti, ni: (bi, 0, 0)),
            pl.BlockSpec((1, d), lambda bi, ti, ni: (0, 0)),
            pl.BlockSpec((d, tn), lambda bi, ti, ni: (0, ni)),
        ],
        out_specs=pl.BlockSpec((1, tm, tn), lambda bi, ti, ni: (bi, ti, ni)),
        out_shape=jax.ShapeDtypeStruct((b, l, n), BF16),
        scratch_shapes=[pltpu.VMEM((tm, d), BF16)],
        compiler_params=_params("arbitrary", "arbitrary", "arbitrary"),
    )(xs, mod, nw, w)


def _boundary_rows(gc, h, reverse):
    t, w = gc.shape
    off = h if reverse else h - 1
    if 2 * h >= SUBLANES:
        g3 = gc.reshape(t // (2 * h), 2 * h, w)
        return jnp.broadcast_to(g3[:, off:off + 1, :], g3.shape).reshape(t, w)
    g3 = gc.reshape(t // SUBLANES, SUBLANES, w)
    sub = lax.broadcasted_iota(jnp.int32, (1, SUBLANES, 1), 1)
    out = None
    for start in range(0, SUBLANES, 2 * h):
        piece = jnp.broadcast_to(g3[:, start + off:start + off + 1, :], g3.shape)
        out = piece if out is None else jnp.where(sub >= start, piece, out)
    return out.reshape(t, w)


def _hgrn_kernel(*refs, reverse, readout):
    if readout:
        q_ref, f_ref, i_ref, lb_ref, hm_ref, of_ref, g_ref, onw_ref, o_ref, st_ref = refs
    else:
        q_ref, f_ref, i_ref, lb_ref, hm_ref, o_ref, st_ref = refs
    t = HG_CHUNK

    @pl.when(pl.program_id(1) == 0)
    def _():
        st_ref[...] = jnp.zeros_like(st_ref)

    q = q_ref[0].astype(F32)
    f = f_ref[0].astype(F32)
    v = i_ref[0].astype(F32)
    lb = lb_ref[...]
    g = lb + (1.0 - lb) * _sigmoid(f)
    kk = 1.0 - g
    lf = jnp.log(g)
    qs = q * _sigmoid(q)

    row = lax.broadcasted_iota(jnp.int32, (t, 1), 0)
    gc = lf
    for lvl in range(HG_LEVELS):
        dist = 1 << lvl
        if reverse:
            gc = gc + jnp.where(row < t - dist, pltpu.roll(gc, t - dist, 0), 0.0)
        else:
            gc = gc + jnp.where(row >= dist, pltpu.roll(gc, dist, 0), 0.0)

    def nt_dot(a, b):
        return lax.dot_general(a, b, (((1,), (1,)), ((), ())), preferred_element_type=F32)

    def head(a, hh):
        return a[:, hh * HG_DK:(hh + 1) * HG_DK]

    qs_b = qs.astype(BF16)
    kk_b = kk.astype(BF16)
    attn = [nt_dot(head(qs_b, hh), head(kk_b, hh)) * hm_ref[0] for hh in range(HG_HEADS)]
    for lvl in range(HG_LEVELS):
        e = jnp.exp(-jnp.abs(gc - _boundary_rows(gc, 1 << lvl, reverse)))
        qh = (qs * e).astype(BF16)
        kh = (kk * e).astype(BF16)
        mask = hm_ref[lvl + 1]
        for hh in range(HG_HEADS):
            attn[hh] = attn[hh] + nt_dot(head(qh, hh), head(kh, hh)) * mask

    g_end = gc[0:1, :] if reverse else gc[t - 1:t, :]
    qd = (qs * jnp.exp(gc)).astype(BF16)
    kd = (kk * jnp.exp(g_end - gc)).astype(BF16)
    d_end = jnp.exp(g_end)
    v_b = v.astype(BF16)

    outs = []
    for hh in range(HG_HEADS):
        st = st_ref[hh]
        vh = head(v_b, hh)
        o_h = nt_dot(head(qd, hh), st.astype(BF16))
        o_h = o_h + jnp.dot(attn[hh].astype(BF16), vh, preferred_element_type=F32)
        vt = head(v, hh).T.astype(BF16)
        st_ref[hh] = st * head(d_end, hh) + jnp.dot(vt, head(kd, hh), preferred_element_type=F32)
        outs.append(o_h)

    if not readout:
        for hh in range(HG_HEADS):
            o_ref[0, :, hh * HG_DK:(hh + 1) * HG_DK] = outs[hh]
        return

    gate = g_ref[0].astype(F32)
    gate = gate * _sigmoid(gate)
    onw = onw_ref[...]
    for hh in range(HG_HEADS):
        o_h = outs[hh] + of_ref[0, :, hh * HG_DK:(hh + 1) * HG_DK]
        ms = jnp.mean(o_h * o_h, axis=-1, keepdims=True)
        y = o_h * lax.rsqrt(ms + EPS) * onw
        o_ref[0, :, hh * HG_DK:(hh + 1) * HG_DK] = (y * head(gate, hh)).astype(o_ref.dtype)


def _hgrn_masks(reverse):
    t = HG_CHUNK
    rt = np.arange(t)[:, None]
    rs = np.arange(t)[None, :]
    qbit = 0 if reverse else 1
    masks = [rt == rs]
    for lvl in range(HG_LEVELS):
        masks.append(((rt >> (lvl + 1)) == (rs >> (lvl + 1))) & (((rt >> lvl) & 1) == qbit)
                     & (((rs >> lvl) & 1) != qbit))
    return jnp.asarray(np.stack(masks), F32)


def _hgrn_call(z, lb, ctx_len, reverse, o_fwd=None, onw=None):
    b, l, _ = z.shape
    t = HG_CHUNK
    n_chunks = l // t
    n_ctx = ctx_len // t
    readout = o_fwd is not None

    if reverse:
        def chunk(i):
            return jnp.where(i < n_ctx, n_ctx - 1 - i, n_chunks - 1 + n_ctx - i)
    else:
        def chunk(i):
            return i

    def zspec(col):
        return pl.BlockSpec((1, t, UNIT), lambda bi, i: (bi, chunk(i), col))

    row_spec = pl.BlockSpec((1, t, HG_W), lambda bi, i: (bi, chunk(i), 0))
    in_specs = [zspec(COL_HQ), zspec(COL_HFB if reverse else COL_HFF), zspec(COL_HI),
                pl.BlockSpec((1, HG_W), lambda bi, i: (0, 0)),
                pl.BlockSpec((HG_LEVELS + 1, t, t), lambda bi, i: (0, 0, 0))]
    args = [z, z, z, lb, _hgrn_masks(reverse)]
    if readout:
        in_specs += [row_spec, zspec(COL_HG), pl.BlockSpec((1, HG_DK), lambda bi, i: (0, 0))]
        args += [o_fwd, z, onw]
    return pl.pallas_call(
        functools.partial(_hgrn_kernel, reverse=reverse, readout=readout),
        grid=(b, n_chunks),
        in_specs=in_specs,
        out_specs=row_spec,
        out_shape=jax.ShapeDtypeStruct((b, l, HG_W), BF16 if readout else F32),
        scratch_shapes=[pltpu.VMEM((HG_HEADS, HG_DK, HG_DK), F32)],
        compiler_params=_params("arbitrary", "arbitrary"),
    )(*args)


def _group_mean_sq(x, bd):
    sq = x * x
    hi = sq.astype(BF16)
    lo = (sq - hi.astype(F32)).astype(BF16)
    s = jnp.dot(hi, bd, preferred_element_type=F32) + jnp.dot(lo, bd, preferred_element_type=F32)
    return s * (1.0 / NA_HD)


def _rotate(x, cos, sin_signed, first_half):
    w = x.shape[-1]
    quarter = NA_HD // 4
    partner = jnp.where(first_half, pltpu.roll(x, w - quarter, 1), pltpu.roll(x, quarter, 1))
    return x * cos + partner * sin_signed


def _qkprep_kernel(q_ref, k_ref, cos_ref, sin_ref, qw_ref, kw_ref, bd_ref, qo_ref, ko_ref):
    bd = bd_ref[...]
    cos = cos_ref[...]
    sin = sin_ref[...]
    lane = lax.broadcasted_iota(jnp.int32, (1, NA_W), 1)
    first_half = (lane % (NA_HD // 2)) < (NA_HD // 4)

    def prep(ref, w):
        x = ref[0].astype(F32)
        y = x * lax.rsqrt(_group_mean_sq(x, bd) + EPS) * w
        return _rotate(y, cos, sin, first_half)

    qo_ref[0] = (prep(q_ref, qw_ref[...]) * (NA_HD ** -0.5)).astype(BF16)
    ko_ref[0] = prep(k_ref, kw_ref[...]).astype(BF16)


def _qkprep_call(z, cos, sin, qw, kw, bd):
    b, l, _ = z.shape
    tm = _pick_tile(l, 544, BF16_ROWS)

    def zspec(col):
        return pl.BlockSpec((1, tm, UNIT), lambda ti, bi: (bi, ti, col))

    tab = pl.BlockSpec((tm, NA_W), lambda ti, bi: (ti, 0))
    vec = pl.BlockSpec((1, NA_W), lambda ti, bi: (0, 0))
    out = pl.BlockSpec((1, tm, NA_W), lambda ti, bi: (bi, ti, 0))
    shp = jax.ShapeDtypeStruct((b, l, NA_W), BF16)
    return pl.pallas_call(
        _qkprep_kernel,
        grid=(l // tm, b),
        in_specs=[zspec(COL_NQ), zspec(COL_NK), tab, tab, vec, vec,
                  pl.BlockSpec((NA_W, NA_W), lambda ti, bi: (0, 0))],
        out_specs=[out, out],
        out_shape=[shp, shp],
        compiler_params=_params("arbitrary", "arbitrary"),
    )(z, z, cos, sin, qw, kw, bd)


NA_ROWS_PER_STEP = 4
NA_QUERIES = NA_ROWS_PER_STEP * GRID_W
NA_BAND_ROWS = NA_WIN_R + NA_ROWS_PER_STEP
NA_BAND = NA_BAND_ROWS * GRID_W
NA_PATTERNS = 3


def _band_start(group_row, rows):
    return jnp.clip(group_row - NA_WIN_R // 2, 0, rows - NA_BAND_ROWS)


def _natten_kernel(q_ref, k_ref, v_ref, bm_ref, o_ref, *, ctx_len, rows):
    i = pl.program_id(1)
    n_ctx_blocks = ctx_len // NA_QUERIES
    pair_w = 2 * NA_HD
    lane = lax.broadcasted_iota(jnp.int32, (1, pair_w), 1)
    low = lane < NA_HD

    def nt_dot(a, b):
        return lax.dot_general(a, b, (((1,), (1,)), ((), ())), preferred_element_type=F32)

    def attend(local):
        if local:
            r = (i - n_ctx_blocks) * NA_ROWS_PER_STEP
            start = pl.multiple_of(ctx_len + _band_start(r, rows) * GRID_W, GRID_W)
        for p in range(NA_HEADS // 2):
            q2 = q_ref[0, :, p * pair_w:(p + 1) * pair_w]
            kc = k_ref[0, 0:ctx_len, p * pair_w:(p + 1) * pair_w]
            vc = v_ref[0, 0:ctx_len, p * pair_w:(p + 1) * pair_w]
            if local:
                kb = k_ref[0, pl.ds(start, NA_BAND), p * pair_w:(p + 1) * pair_w]
                vb = v_ref[0, pl.ds(start, NA_BAND), p * pair_w:(p + 1) * pair_w]
            halves = []
            for hh in range(2):
                qm = jnp.where(low if hh == 0 else ~low, q2, jnp.zeros_like(q2))
                s_ctx = nt_dot(qm, kc)
                m = jnp.max(s_ctx, axis=-1, keepdims=True)
                if local:
                    s_loc = nt_dot(qm, kb) + bm_ref[0, 2 * p + hh]
                    m = jnp.maximum(m, jnp.max(s_loc, axis=-1, keepdims=True))
                p_ctx = jnp.exp(s_ctx - m)
                den = jnp.sum(p_ctx, axis=-1, keepdims=True)
                o = jnp.dot(p_ctx.astype(BF16), vc, preferred_element_type=F32)
                if local:
                    p_loc = jnp.exp(s_loc - m)
                    den = den + jnp.sum(p_loc, axis=-1, keepdims=True)
                    o = o + jnp.dot(p_loc.astype(BF16), vb, preferred_element_type=F32)
                halves.append(o / den)
            o_ref[0, :, p * pair_w:(p + 1) * pair_w] = jnp.where(low, halves[0], halves[1]).astype(o_ref.dtype)

    @pl.when(i < n_ctx_blocks)
    def _():
        attend(False)

    @pl.when(i >= n_ctx_blocks)
    def _():
        attend(True)


def _natten_call(qn, kn, z, bias, ctx_len):
    b, l, _ = qn.shape
    rows = (l - ctx_len) // GRID_W
    n_ctx_blocks = ctx_len // NA_QUERIES

    def pattern(i):
        r = jnp.maximum(i - n_ctx_blocks, 0) * NA_ROWS_PER_STEP
        return (r - _band_start(r, rows)) // NA_ROWS_PER_STEP

    full = pl.BlockSpec((1, l, NA_W), lambda bi, i: (bi, 0, 0), pipeline_mode=pl.Buffered(1))
    vfull = pl.BlockSpec((1, l, NA_W), lambda bi, i: (bi, 0, COL_NV), pipeline_mode=pl.Buffered(1))
    blk = pl.BlockSpec((1, NA_QUERIES, NA_W), lambda bi, i: (bi, i, 0))
    return pl.pallas_call(
        functools.partial(_natten_kernel, ctx_len=ctx_len, rows=rows),
        grid=(b, l // NA_QUERIES),
        in_specs=[blk, full, vfull,
                  pl.BlockSpec((1, NA_HEADS, NA_QUERIES, NA_BAND), lambda bi, i: (pattern(i), 0, 0, 0))],
        out_specs=blk,
        out_shape=jax.ShapeDtypeStruct((b, l, NA_W), BF16),
        compiler_params=_params("arbitrary", "arbitrary"),
    )(qn, kn, z, bias)


def _bias_table(rpb):
    qc = np.arange(GRID_W)[:, None]
    kc = np.arange(GRID_W)[None, :]
    c0 = np.clip(qc - NA_WIN_C // 2, 0, GRID_W - NA_WIN_C)
    col_ok = (kc >= c0) & (kc < c0 + NA_WIN_C)
    n_ci = 2 * NA_WIN_C - 1
    oc = ((kc - qc + NA_WIN_C - 1)[:, :, None] == np.arange(n_ci)) & col_ok[:, :, None]

    g = np.arange(NA_PATTERNS)[:, None, None]
    a = np.arange(NA_ROWS_PER_STEP)[None, :, None]
    j = np.arange(NA_BAND_ROWS)[None, None, :]
    r_rel = NA_ROWS_PER_STEP * g + a
    r0_rel = np.clip(r_rel - NA_WIN_R // 2, 0, NA_BAND_ROWS - NA_WIN_R)
    row_ok = (j >= r0_rel) & (j < r0_rel + NA_WIN_R)
    n_ri = 2 * NA_WIN_R - 1
    orow = ((j - r_rel + NA_WIN_R - 1)[..., None] == np.arange(n_ri)) & row_ok[..., None]

    hp = lax.Precision.HIGHEST
    cols = jnp.einsum('hrc,qkc->hrqk', rpb.astype(F32), jnp.asarray(oc, F32), precision=hp)
    tab = jnp.einsum('gajr,hrqk->ghaqjk', jnp.asarray(orow, F32), cols, precision=hp)
    ok = row_ok[:, None, :, None, :, None] & col_ok[None, None, None, :, None, :]
    tab = jnp.where(ok, tab, NEG_BIG)
    return tab.reshape(NA_PATTERNS, NA_HEADS, NA_QUERIES, NA_BAND)


def _rope_tables(ctx_len, s):
    tpos = np.arange(s)
    pos = np.stack([tpos // GRID_W, tpos % GRID_W], axis=-1).astype(np.float32)
    half = NA_HD // 2
    inv = (ROPE_THETA ** (-jnp.arange(0, half, 2, dtype=F32) / half))
    ang = jnp.asarray(pos)[:, :, None] * inv
    cos = jnp.cos(ang)
    sin = jnp.sin(ang)
    cos_h = jnp.concatenate([cos, cos], axis=-1).reshape(s, NA_HD)
    sin_h = jnp.concatenate([-sin, sin], axis=-1).reshape(s, NA_HD)
    cos_t = jnp.tile(cos_h, (1, NA_HEADS))
    sin_t = jnp.tile(sin_h, (1, NA_HEADS))
    cos_t = jnp.concatenate([jnp.ones((ctx_len, NA_W), F32), cos_t], axis=0)
    sin_t = jnp.concatenate([jnp.zeros((ctx_len, NA_W), F32), sin_t], axis=0)
    return cos_t, sin_t


def _merge_kernel(x_ref, mod_ref, ya_ref, yb_ref, cb_ref, cc_ref, cx_ref,
                  ccp_ref, cxp_ref, ccn_ref, cxn_ref, ga_ref, gb_ref, gc_ref,
                  cw_ref, wa_ref, wb_ref, wc_ref, wo_ref, o_ref, *, ctx_len, seq_len, tm, d):
    ti = pl.program_id(1)
    row = ti * tm + lax.broadcasted_iota(jnp.int32, (tm, 1), 0)
    is_ctx = row < ctx_len
    local = lax.broadcasted_iota(jnp.int32, (tm, 1), 0)

    u = cc_ref[0].astype(F32) * cx_ref[0].astype(F32)
    u_before = ccp_ref[0, BF16_ROWS - 1:BF16_ROWS, :].astype(F32) * cxp_ref[0, BF16_ROWS - 1:BF16_ROWS, :].astype(F32)
    u_after = ccn_ref[0, 0:1, :].astype(F32) * cxn_ref[0, 0:1, :].astype(F32)
    up = jnp.where(local == 0, u_before, pltpu.roll(u, 1, 0))
    un = jnp.where(local == tm - 1, u_after, pltpu.roll(u, tm - 1, 0))
    has_prev = (row != 0) & (row != ctx_len)
    has_next = (row != ctx_len - 1) & (row != seq_len - 1)
    cw = cw_ref[...]
    conv = (jnp.where(has_prev, up, 0.0) * cw[0:1, :] + u * cw[1:2, :]
            + jnp.where(has_next, un, 0.0) * cw[2:3, :])
    yc = (cb_ref[0].astype(F32) * conv).astype(BF16)

    mix = _sigmoid(ga_ref[0].astype(F32)) * jnp.dot(ya_ref[0], wa_ref[...], preferred_element_type=F32)
    mix = mix + _sigmoid(gb_ref[0].astype(F32)) * jnp.dot(yb_ref[0], wb_ref[...], preferred_element_type=F32)
    mix = mix + _sigmoid(gc_ref[0].astype(F32)) * jnp.dot(yc, wc_ref[...], preferred_element_type=F32)
    out = jnp.dot(mix.astype(BF16), wo_ref[...], preferred_element_type=F32)
    o_ref[0] = x_ref[0] + _row_mod(mod_ref, 2, is_ctx, d) * out


def _merge_call(xs, mod, ya, yb, z, cw, wa, wb, wc, wo, ctx_len):
    b, l, d = xs.shape
    tm = _pick_tile(l, 544, BF16_ROWS)
    halo = tm // BF16_ROWS
    n_halo = l // BF16_ROWS

    def zspec(col, width=UNIT):
        return pl.BlockSpec((1, tm, width), lambda bi, ti: (bi, ti, col * UNIT // width))

    def prev(col):
        return pl.BlockSpec((1, BF16_ROWS, UNIT), lambda bi, ti: (bi, jnp.maximum(ti * halo - 1, 0), col))

    def nxt(col):
        return pl.BlockSpec((1, BF16_ROWS, UNIT),
                            lambda bi, ti: (bi, jnp.minimum((ti + 1) * halo, n_halo - 1), col))

    def const(shape):
        return pl.BlockSpec(shape, lambda bi, ti: tuple(0 for _ in shape))

    tok = pl.BlockSpec((1, tm, d), lambda bi, ti: (bi, ti, 0))
    half = pl.BlockSpec((1, tm, UNIT), lambda bi, ti: (bi, ti, 0))
    return pl.pallas_call(
        functools.partial(_merge_kernel, ctx_len=ctx_len, seq_len=l, tm=tm, d=d),
        grid=(b, l // tm),
        in_specs=[tok, pl.BlockSpec((1, 2, N_MOD * d), lambda bi, ti: (bi, 0, 0)), half, half,
                  zspec(COL_CB), zspec(COL_CC), zspec(COL_CX),
                  prev(COL_CC), prev(COL_CX), nxt(COL_CC), nxt(COL_CX),
                  zspec(COL_GA, d), zspec(COL_GB, d), zspec(COL_GC, d),
                  const(cw.shape), const(wa.shape), const(wb.shape), const(wc.shape), const(wo.shape)],
        out_specs=tok,
        out_shape=jax.ShapeDtypeStruct((b, l, d), F32),
        compiler_params=_params("arbitrary", "arbitrary"),
    )(xs, mod, ya, yb, z, z, z, z, z, z, z, z, z, z, cw, wa, wb, wc, wo)


def _mlp_kernel(x_ref, mod_ref, nw_ref, w1_ref, w2_ref, o_ref, *, ctx_len, tm, d, ff_chunk):
    ti = pl.program_id(1)
    row = ti * tm + lax.broadcasted_iota(jnp.int32, (tm, 1), 0)
    is_ctx = row < ctx_len
    x = x_ref[0]
    h = _modulated_norm(x, nw_ref[...], _row_mod(mod_ref, 3, is_ctx, d),
                        _row_mod(mod_ref, 4, is_ctx, d)).astype(BF16)
    acc = jnp.zeros((tm, d), F32)
    for c in range(w1_ref.shape[1] // ff_chunk):
        a = jnp.dot(h, w1_ref[:, c * ff_chunk:(c + 1) * ff_chunk], preferred_element_type=F32)
        a = jnp.maximum(a, 0.0)
        acc = acc + jnp.dot((a * a).astype(BF16), w2_ref[c * ff_chunk:(c + 1) * ff_chunk, :],
                            preferred_element_type=F32)
    o_ref[0] = x + _row_mod(mod_ref, 5, is_ctx, d) * acc


def _mlp_call(xs, mod, nw, w1, w2, ctx_len):
    b, l, d = xs.shape
    tm = _pick_tile(l, 544, BF16_ROWS)
    tok = pl.BlockSpec((1, tm, d), lambda bi, ti: (bi, ti, 0))
    return pl.pallas_call(
        functools.partial(_mlp_kernel, ctx_len=ctx_len, tm=tm, d=d, ff_chunk=1024),
        grid=(b, l // tm),
        in_specs=[tok, pl.BlockSpec((1, 2, N_MOD * d), lambda bi, ti: (bi, 0, 0)),
                  pl.BlockSpec((1, d), lambda bi, ti: (0, 0)),
                  pl.BlockSpec(w1.shape, lambda bi, ti: (0, 0), pipeline_mode=pl.Buffered(1)),
                  pl.BlockSpec(w2.shape, lambda bi, ti: (0, 0), pipeline_mode=pl.Buffered(1))],
        out_specs=tok,
        out_shape=jax.ShapeDtypeStruct((b, l, d), F32),
        compiler_params=_params("arbitrary", "arbitrary"),
    )(xs, mod, nw, w1, w2)


def _reorder_proj(w_in):
    hg = 5 * UNIT
    na = 3 * UNIT
    sc = 3 * UNIT
    gates = w_in[..., hg + na + sc:]
    return jnp.concatenate([gates, w_in[..., :hg + na + sc]], axis=-1)


def kernel(x, c, ctx, c_ctx, ada_w, ada_b, norm1_w, norm2_w, w_in, hgrn_lb_logits, hgrn_onorm_w,
           q_norm_w, k_norm_w, natten_rpb, conv_w, w_branch_a, w_branch_b, w_branch_c, w_out,
           mlp_w1, mlp_w2):
    b, s, d = x.shape
    ctx_len = ctx.shape[1]
    depth = ada_w.shape[0]
    assert s % NA_QUERIES == 0 and s // GRID_W >= NA_BAND_ROWS and ctx_len % NA_QUERIES == 0
    assert ctx_len % HG_CHUNK == 0 and s % HG_CHUNK == 0
    assert w_in.shape[-1] == PROJ_UNITS * UNIT

    lb_p = jax.nn.softmax(hgrn_lb_logits.astype(F32), axis=1)
    lower_bounds = jnp.cumsum(lb_p, axis=1) - lb_p[:, :1]

    pad_rows = -(b + 1) % SUBLANES
    cvec = jnp.concatenate([c_ctx[None, :], c, jnp.zeros((pad_rows, d), F32)], axis=0)
    mod_all = _ada_call(cvec, ada_w, ada_b)
    mod_sel = jnp.stack([jnp.broadcast_to(mod_all[:, 0:1], (depth, b, N_MOD * d)),
                         mod_all[:, 1:b + 1]], axis=2)

    cos_t, sin_t = _rope_tables(ctx_len, s)
    bd = jnp.asarray(np.kron(np.eye(NA_HEADS), np.ones((NA_HD, NA_HD))), BF16)
    w_in_b = _reorder_proj(w_in).astype(BF16)

    xs = jnp.concatenate([ctx, x], axis=1)
    for l in range(depth):
        mod = mod_sel[l]
        z = _inproj_call(xs, mod, norm1_w[l][None, :], w_in_b[l], ctx_len)
        o_f = _hgrn_call(z, lower_bounds[0, l][None, :], ctx_len, False)
        ya = _hgrn_call(z, lower_bounds[1, l][None, :], ctx_len, True, o_f, hgrn_onorm_w[l][None, :])
        qn, kn = _qkprep_call(z, cos_t, sin_t, jnp.tile(q_norm_w[l], NA_HEADS)[None, :],
                              jnp.tile(k_norm_w[l], NA_HEADS)[None, :], bd)
        yb = _natten_call(qn, kn, z, _bias_table(natten_rpb[l]), ctx_len)
        xs = _merge_call(xs, mod, ya, yb, z, conv_w[l], w_branch_a[l].astype(BF16),
                         w_branch_b[l].astype(BF16), w_branch_c[l].astype(BF16),
                         w_out[l].astype(BF16), ctx_len)
        xs = _mlp_call(xs, mod, norm2_w[l][None, :], mlp_w1[l].astype(BF16),
                       mlp_w2[l].astype(BF16), ctx_len)
    return xs[:, ctx_len:, :]
```

```python
import functools

import numpy as np
import jax
import jax.numpy as jnp
from jax import lax
from jax.experimental import pallas as pl
from jax.experimental.pallas import tpu as pltpu

GRID_W = 64
HG_HEADS = 4
HG_DK = 128
HG_W = HG_HEADS * HG_DK
HG_CHUNK = 64
HG_LEVELS = 6
HG_CHUNKS_PER_STEP = 4
NA_HEADS = 8
NA_HD = 64
NA_W = NA_HEADS * NA_HD
NA_WIN_R = 8
NA_WIN_C = 16
ROPE_THETA = 10000.0
SC_W = 512
N_MOD = 6
EPS = 1e-6
NEG_BIG = -1e30

LANES = 128
SUBLANES = 8
BF16_ROWS = 16
VMEM_LIMIT = 56 * 1024 * 1024

UNIT = 512
COL_GA, COL_GB, COL_GC = 0, 2, 4
COL_HQ, COL_HFF, COL_HFB, COL_HI, COL_HG = 6, 7, 8, 9, 10
COL_NQ, COL_NK, COL_NV = 11, 12, 13
COL_CB, COL_CC, COL_CX = 14, 15, 16
PROJ_UNITS = 17

F32 = jnp.float32
BF16 = jnp.bfloat16


def _pick_tile(n, target, mult):
    best = None
    for t in range(mult, min(n, target) + 1, mult):
        if n % t == 0:
            best = t
    if best is None:
        raise ValueError(f"no tile for {n} (target {target}, multiple {mult})")
    return best


def _sigmoid(x):
    return 1.0 / (1.0 + jnp.exp(-x))


def _params(*sem):
    return pltpu.CompilerParams(dimension_semantics=sem, vmem_limit_bytes=VMEM_LIMIT)


def _ada_kernel(c_ref, w_ref, b_ref, o_ref):
    c = c_ref[...]
    s = c * _sigmoid(c)
    o_ref[0] = jnp.dot(s, w_ref[0], precision=lax.Precision.HIGHEST,
                       preferred_element_type=F32) + b_ref[0]


def _ada_call(cvec, ada_w, ada_b):
    depth, d, n = ada_w.shape
    rows = cvec.shape[0]
    tn = _pick_tile(n, 1024, LANES)
    return pl.pallas_call(
        _ada_kernel,
        grid=(depth, n // tn),
        in_specs=[
            pl.BlockSpec((rows, d), lambda l, j: (0, 0)),
            pl.BlockSpec((1, d, tn), lambda l, j: (l, 0, j)),
            pl.BlockSpec((1, 1, tn), lambda l, j: (l, 0, j)),
        ],
        out_specs=pl.BlockSpec((1, rows, tn), lambda l, j: (l, 0, j)),
        out_shape=jax.ShapeDtypeStruct((depth, rows, n), F32),
        compiler_params=_params("arbitrary", "arbitrary"),
    )(cvec, ada_w, ada_b.reshape(depth, 1, n))


def _row_mod(mod_ref, idx, is_ctx, d):
    mc = mod_ref[0, 0:1, idx * d:(idx + 1) * d]
    ml = mod_ref[0, 1:2, idx * d:(idx + 1) * d]
    return jnp.where(is_ctx, mc, ml)


def _modulated_norm(x, nw, shift, scale):
    ms = jnp.mean(x * x, axis=-1, keepdims=True)
    return (x * lax.rsqrt(ms + EPS) * nw) * (1.0 + scale) + shift


def _inproj_kernel(x_ref, mod_ref, nw_ref, w_ref, z_ref, *, ctx_len, tm, d, n_chunk):
    row = pl.program_id(1) * tm + lax.broadcasted_iota(jnp.int32, (tm, 1), 0)
    is_ctx = row < ctx_len
    xn = _modulated_norm(x_ref[0], nw_ref[...], _row_mod(mod_ref, 0, is_ctx, d),
                         _row_mod(mod_ref, 1, is_ctx, d)).astype(BF16)
    for c in range(w_ref.shape[1] // n_chunk):
        cols = slice(c * n_chunk, (c + 1) * n_chunk)
        z_ref[0, :, cols] = jnp.dot(xn, w_ref[:, cols], preferred_element_type=F32).astype(z_ref.dtype)


def _inproj_call(xs, mod, nw, w, ctx_len):
    b, l, d = xs.shape
    n = w.shape[1]
    tm = _pick_tile(l, 544, BF16_ROWS)
    return pl.pallas_call(
        functools.partial(_inproj_kernel, ctx_len=ctx_len, tm=tm, d=d, n_chunk=UNIT),
        grid=(b, l // tm),
        in_specs=[
            pl.BlockSpec((1, tm, d), lambda bi, ti: (bi, ti, 0)),
            pl.BlockSpec((1, 2, N_MOD * d), lambda bi, ti: (bi, 0, 0)),
            pl.BlockSpec((1, d), lambda bi, ti: (0, 0)),
            pl.BlockSpec((d, n), lambda bi, ti: (0, 0), pipeline_mode=pl.Buffered(1)),
        ],
        out_specs=pl.BlockSpec((1, tm, n), lambda bi, ti: (bi, ti, 0)),
        out_shape=jax.ShapeDtypeStruct((b, l, n), BF16),
        compiler_params=_params("arbitrary", "arbitrary"),
    )(xs, mod, nw, w)


def _boundary_rows(gc, h, reverse):
    t, w = gc.shape
    off = h if reverse else h - 1
    if 2 * h >= SUBLANES:
        g3 = gc.reshape(t // (2 * h), 2 * h, w)
        return jnp.broadcast_to(g3[:, off:off + 1, :], g3.shape).reshape(t, w)
    g3 = gc.reshape(t // SUBLANES, SUBLANES, w)
    sub = lax.broadcasted_iota(jnp.int32, (1, SUBLANES, 1), 1)
    out = None
    for start in range(0, SUBLANES, 2 * h):
        piece = jnp.broadcast_to(g3[:, start + off:start + off + 1, :], g3.shape)
        out = piece if out is None else jnp.where(sub >= start, piece, out)
    return out.reshape(t, w)


def _hgrn_chunk(q, f, v, lb, hm_ref, states, reverse):
    t = HG_CHUNK
    g = lb + (1.0 - lb) * _sigmoid(f)
    kk = 1.0 - g
    lf = jnp.log(g)
    qs = q * _sigmoid(q)

    row = lax.broadcasted_iota(jnp.int32, (t, 1), 0)
    gc = lf
    for lvl in range(HG_LEVELS):
        dist = 1 << lvl
        if reverse:
            gc = gc + jnp.where(row < t - dist, pltpu.roll(gc, t - dist, 0), 0.0)
        else:
            gc = gc + jnp.where(row >= dist, pltpu.roll(gc, dist, 0), 0.0)

    def nt_dot(a, b):
        return lax.dot_general(a, b, (((1,), (1,)), ((), ())), preferred_element_type=F32)

    def head(a, hh):
        return a[:, hh * HG_DK:(hh + 1) * HG_DK]

    qs_b = qs.astype(BF16)
    kk_b = kk.astype(BF16)
    attn = [nt_dot(head(qs_b, hh), head(kk_b, hh)) * hm_ref[0] for hh in range(HG_HEADS)]
    qbit = 0 if reverse else 1
    for lvl in range(HG_LEVELS):
        e = jnp.exp(-jnp.abs(gc - _boundary_rows(gc, 1 << lvl, reverse)))
        is_q = ((row >> lvl) & 1) == qbit
        both = (jnp.where(is_q, qs, kk) * e).astype(BF16)
        mask = hm_ref[lvl + 1]
        for hh in range(HG_HEADS):
            attn[hh] = attn[hh] + nt_dot(head(both, hh), head(both, hh)) * mask

    g_end = gc[0:1, :] if reverse else gc[t - 1:t, :]
    qd = (qs * jnp.exp(gc)).astype(BF16)
    kd = (kk * jnp.exp(g_end - gc)).astype(BF16)
    d_end = jnp.exp(g_end)
    v_b = v.astype(BF16)

    outs = []
    new_states = []
    for hh in range(HG_HEADS):
        st = states[hh]
        o_h = nt_dot(head(qd, hh), st.astype(BF16))
        o_h = o_h + jnp.dot(attn[hh].astype(BF16), head(v_b, hh), preferred_element_type=F32)
        vt = head(v, hh).T.astype(BF16)
        new_states.append(st * head(d_end, hh) + jnp.dot(vt, head(kd, hh), preferred_element_type=F32))
        outs.append(o_h)
    return outs, new_states


def _hgrn_kernel(*refs, reverse, readout, n_sub):
    if readout:
        q_ref, f_ref, i_ref, lb_ref, hm_ref, of_ref, g_ref, onw_ref, o_ref, st_ref = refs
    else:
        q_ref, f_ref, i_ref, lb_ref, hm_ref, o_ref, st_ref = refs
    t = HG_CHUNK

    @pl.when(pl.program_id(1) == 0)
    def _():
        st_ref[...] = jnp.zeros_like(st_ref)

    lb = lb_ref[...]
    states = [st_ref[hh] for hh in range(HG_HEADS)]
    for sub in (range(n_sub - 1, -1, -1) if reverse else range(n_sub)):
        rows = slice(sub * t, (sub + 1) * t)
        outs, states = _hgrn_chunk(q_ref[0, rows, :].astype(F32), f_ref[0, rows, :].astype(F32),
                                   i_ref[0, rows, :].astype(F32), lb, hm_ref, states, reverse)
        if readout:
            gate = g_ref[0, rows, :].astype(F32)
            gate = gate * _sigmoid(gate)
        for hh in range(HG_HEADS):
            cols = slice(hh * HG_DK, (hh + 1) * HG_DK)
            if readout:
                o_h = outs[hh] + of_ref[0, rows, cols]
                ms = jnp.mean(o_h * o_h, axis=-1, keepdims=True)
                y = o_h * lax.rsqrt(ms + EPS) * onw_ref[...]
                o_ref[0, rows, cols] = (y * gate[:, cols]).astype(o_ref.dtype)
            else:
                o_ref[0, rows, cols] = outs[hh]
    for hh in range(HG_HEADS):
        st_ref[hh] = states[hh]


def _hgrn_masks(reverse):
    t = HG_CHUNK
    rt = np.arange(t)[:, None]
    rs = np.arange(t)[None, :]
    qbit = 0 if reverse else 1
    masks = [rt == rs]
    for lvl in range(HG_LEVELS):
        masks.append(((rt >> (lvl + 1)) == (rs >> (lvl + 1))) & (((rt >> lvl) & 1) == qbit)
                     & (((rs >> lvl) & 1) != qbit))
    return jnp.asarray(np.stack(masks), F32)


def _hgrn_call(z, lb, ctx_len, reverse, o_fwd=None, onw=None):
    b, l, _ = z.shape
    t = HG_CHUNK
    rows = HG_CHUNKS_PER_STEP * t
    n_blocks = l // rows
    n_ctx = ctx_len // rows
    readout = o_fwd is not None

    if reverse:
        def block(i):
            return jnp.where(i < n_ctx, n_ctx - 1 - i, n_blocks - 1 + n_ctx - i)
    else:
        def block(i):
            return i

    def zspec(col):
        return pl.BlockSpec((1, rows, UNIT), lambda bi, i: (bi, block(i), col))

    row_spec = pl.BlockSpec((1, rows, HG_W), lambda bi, i: (bi, block(i), 0))
    in_specs = [zspec(COL_HQ), zspec(COL_HFB if reverse else COL_HFF), zspec(COL_HI),
                pl.BlockSpec((1, HG_W), lambda bi, i: (0, 0)),
                pl.BlockSpec((HG_LEVELS + 1, t, t), lambda bi, i: (0, 0, 0))]
    args = [z, z, z, lb, _hgrn_masks(reverse)]
    if readout:
        in_specs += [row_spec, zspec(COL_HG), pl.BlockSpec((1, HG_DK), lambda bi, i: (0, 0))]
        args += [o_fwd, z, onw]
    return pl.pallas_call(
        functools.partial(_hgrn_kernel, reverse=reverse, readout=readout, n_sub=HG_CHUNKS_PER_STEP),
        grid=(b, n_blocks),
        in_specs=in_specs,
        out_specs=row_spec,
        out_shape=jax.ShapeDtypeStruct((b, l, HG_W), BF16 if readout else F32),
        scratch_shapes=[pltpu.VMEM((HG_HEADS, HG_DK, HG_DK), F32)],
        compiler_params=_params("arbitrary", "arbitrary"),
    )(*args)


def _group_mean_sq(x, bd):
    sq = x * x
    hi = sq.astype(BF16)
    lo = (sq - hi.astype(F32)).astype(BF16)
    s = jnp.dot(hi, bd, preferred_element_type=F32) + jnp.dot(lo, bd, preferred_element_type=F32)
    return s * (1.0 / NA_HD)


def _rotate(x, cos, sin_signed, first_half):
    w = x.shape[-1]
    quarter = NA_HD // 4
    partner = jnp.where(first_half, pltpu.roll(x, w - quarter, 1), pltpu.roll(x, quarter, 1))
    return x * cos + partner * sin_signed


def _qkprep_kernel(q_ref, k_ref, cos_ref, sin_ref, qw_ref, kw_ref, bd_ref, qo_ref, ko_ref):
    bd = bd_ref[...]
    cos = cos_ref[...]
    sin = sin_ref[...]
    lane = lax.broadcasted_iota(jnp.int32, (1, NA_W), 1)
    first_half = (lane % (NA_HD // 2)) < (NA_HD // 4)

    def prep(ref, w):
        x = ref[0].astype(F32)
        y = x * lax.rsqrt(_group_mean_sq(x, bd) + EPS) * w
        return _rotate(y, cos, sin, first_half)

    qo_ref[0] = (prep(q_ref, qw_ref[...]) * (NA_HD ** -0.5)).astype(BF16)
    ko_ref[0] = prep(k_ref, kw_ref[...]).astype(BF16)


def _qkprep_call(z, cos, sin, qw, kw, bd):
    b, l, _ = z.shape
    tm = _pick_tile(l, 544, BF16_ROWS)

    def zspec(col):
        return pl.BlockSpec((1, tm, UNIT), lambda ti, bi: (bi, ti, col))

    tab = pl.BlockSpec((tm, NA_W), lambda ti, bi: (ti, 0))
    vec = pl.BlockSpec((1, NA_W), lambda ti, bi: (0, 0))
    out = pl.BlockSpec((1, tm, NA_W), lambda ti, bi: (bi, ti, 0))
    shp = jax.ShapeDtypeStruct((b, l, NA_W), BF16)
    return pl.pallas_call(
        _qkprep_kernel,
        grid=(l // tm, b),
        in_specs=[zspec(COL_NQ), zspec(COL_NK), tab, tab, vec, vec,
                  pl.BlockSpec((NA_W, NA_W), lambda ti, bi: (0, 0))],
        out_specs=[out, out],
        out_shape=[shp, shp],
        compiler_params=_params("arbitrary", "arbitrary"),
    )(z, z, cos, sin, qw, kw, bd)


NA_ROWS_PER_STEP = 4
NA_QUERIES = NA_ROWS_PER_STEP * GRID_W
NA_BAND_ROWS = NA_WIN_R + NA_ROWS_PER_STEP
NA_BAND = NA_BAND_ROWS * GRID_W
NA_PATTERNS = 3


def _band_start(group_row, rows):
    return jnp.clip(group_row - NA_WIN_R // 2, 0, rows - NA_BAND_ROWS)


def _natten_kernel(q_ref, k_ref, v_ref, bm_ref, o_ref, *, ctx_len, rows):
    i = pl.program_id(1)
    n_ctx_blocks = ctx_len // NA_QUERIES
    pair_w = 2 * NA_HD
    lane = lax.broadcasted_iota(jnp.int32, (1, pair_w), 1)
    low = lane < NA_HD

    def nt_dot(a, b):
        return lax.dot_general(a, b, (((1,), (1,)), ((), ())), preferred_element_type=F32)

    def attend(local):
        if local:
            r = (i - n_ctx_blocks) * NA_ROWS_PER_STEP
            start = pl.multiple_of(ctx_len + _band_start(r, rows) * GRID_W, GRID_W)
        for p in range(NA_HEADS // 2):
            q2 = q_ref[0, :, p * pair_w:(p + 1) * pair_w]
            kc = k_ref[0, 0:ctx_len, p * pair_w:(p + 1) * pair_w]
            vc = v_ref[0, 0:ctx_len, p * pair_w:(p + 1) * pair_w]
            if local:
                kb = k_ref[0, pl.ds(start, NA_BAND), p * pair_w:(p + 1) * pair_w]
                vb = v_ref[0, pl.ds(start, NA_BAND), p * pair_w:(p + 1) * pair_w]
            halves = []
            for hh in range(2):
                qm = jnp.where(low if hh == 0 else ~low, q2, jnp.zeros_like(q2))
                s_ctx = nt_dot(qm, kc)
                m = jnp.max(s_ctx, axis=-1, keepdims=True)
                if local:
                    s_loc = nt_dot(qm, kb) + bm_ref[0, 2 * p + hh]
                    m = jnp.maximum(m, jnp.max(s_loc, axis=-1, keepdims=True))
                p_ctx = jnp.exp(s_ctx - m)
                den = jnp.sum(p_ctx, axis=-1, keepdims=True)
                o = jnp.dot(p_ctx.astype(BF16), vc, preferred_element_type=F32)
                if local:
                    p_loc = jnp.exp(s_loc - m)
                    den = den + jnp.sum(p_loc, axis=-1, keepdims=True)
                    o = o + jnp.dot(p_loc.astype(BF16), vb, preferred_element_type=F32)
                halves.append(o / den)
            o_ref[0, :, p * pair_w:(p + 1) * pair_w] = jnp.where(low, halves[0], halves[1]).astype(o_ref.dtype)

    @pl.when(i < n_ctx_blocks)
    def _():
        attend(False)

    @pl.when(i >= n_ctx_blocks)
    def _():
        attend(True)


def _natten_call(qn, kn, z, bias, ctx_len):
    b, l, _ = qn.shape
    rows = (l - ctx_len) // GRID_W
    n_ctx_blocks = ctx_len // NA_QUERIES

    def pattern(i):
        r = jnp.maximum(i - n_ctx_blocks, 0) * NA_ROWS_PER_STEP
        return (r - _band_start(r, rows)) // NA_ROWS_PER_STEP

    full = pl.BlockSpec((1, l, NA_W), lambda bi, i: (bi, 0, 0), pipeline_mode=pl.Buffered(1))
    vfull = pl.BlockSpec((1, l, NA_W), lambda bi, i: (bi, 0, COL_NV), pipeline_mode=pl.Buffered(1))
    blk = pl.BlockSpec((1, NA_QUERIES, NA_W), lambda bi, i: (bi, i, 0))
    return pl.pallas_call(
        functools.partial(_natten_kernel, ctx_len=ctx_len, rows=rows),
        grid=(b, l // NA_QUERIES),
        in_specs=[blk, full, vfull,
                  pl.BlockSpec((1, NA_HEADS, NA_QUERIES, NA_BAND), lambda bi, i: (pattern(i), 0, 0, 0))],
        out_specs=blk,
        out_shape=jax.ShapeDtypeStruct((b, l, NA_W), BF16),
        compiler_params=_params("arbitrary", "arbitrary"),
    )(qn, kn, z, bias)


def _bias_table(rpb):
    qc = np.arange(GRID_W)[:, None]
    kc = np.arange(GRID_W)[None, :]
    c0 = np.clip(qc - NA_WIN_C // 2, 0, GRID_W - NA_WIN_C)
    col_ok = (kc >= c0) & (kc < c0 + NA_WIN_C)
    n_ci = 2 * NA_WIN_C - 1
    oc = ((kc - qc + NA_WIN_C - 1)[:, :, None] == np.arange(n_ci)) & col_ok[:, :, None]

    g = np.arange(NA_PATTERNS)[:, None, None]
    a = np.arange(NA_ROWS_PER_STEP)[None, :, None]
    j = np.arange(NA_BAND_ROWS)[None, None, :]
    r_rel = NA_ROWS_PER_STEP * g + a
    r0_rel = np.clip(r_rel - NA_WIN_R // 2, 0, NA_BAND_ROWS - NA_WIN_R)
    row_ok = (j >= r0_rel) & (j < r0_rel + NA_WIN_R)
    n_ri = 2 * NA_WIN_R - 1
    orow = ((j - r_rel + NA_WIN_R - 1)[..., None] == np.arange(n_ri)) & row_ok[..., None]

    hp = lax.Precision.HIGHEST
    cols = jnp.einsum('hrc,qkc->hrqk', rpb.astype(F32), jnp.asarray(oc, F32), precision=hp)
    tab = jnp.einsum('gajr,hrqk->ghaqjk', jnp.asarray(orow, F32), cols, precision=hp)
    ok = row_ok[:, None, :, None, :, None] & col_ok[None, None, None, :, None, :]
    tab = jnp.where(ok, tab, NEG_BIG)
    return tab.reshape(NA_PATTERNS, NA_HEADS, NA_QUERIES, NA_BAND)


def _rope_tables(ctx_len, s):
    tpos = np.arange(s)
    pos = np.stack([tpos // GRID_W, tpos % GRID_W], axis=-1).astype(np.float32)
    half = NA_HD // 2
    inv = (ROPE_THETA ** (-jnp.arange(0, half, 2, dtype=F32) / half))
    ang = jnp.asarray(pos)[:, :, None] * inv
    cos = jnp.cos(ang)
    sin = jnp.sin(ang)
    cos_h = jnp.concatenate([cos, cos], axis=-1).reshape(s, NA_HD)
    sin_h = jnp.concatenate([-sin, sin], axis=-1).reshape(s, NA_HD)
    cos_t = jnp.tile(cos_h, (1, NA_HEADS))
    sin_t = jnp.tile(sin_h, (1, NA_HEADS))
    cos_t = jnp.concatenate([jnp.ones((ctx_len, NA_W), F32), cos_t], axis=0)
    sin_t = jnp.concatenate([jnp.zeros((ctx_len, NA_W), F32), sin_t], axis=0)
    return cos_t, sin_t


def _merge_kernel(x_ref, mod_ref, ya_ref, yb_ref, cb_ref, cc_ref, cx_ref,
                  ccp_ref, cxp_ref, ccn_ref, cxn_ref, ga_ref, gb_ref, gc_ref,
                  cw_ref, wa_ref, wb_ref, wc_ref, wo_ref, o_ref, *, ctx_len, seq_len, tm, d):
    ti = pl.program_id(1)
    row = ti * tm + lax.broadcasted_iota(jnp.int32, (tm, 1), 0)
    is_ctx = row < ctx_len
    local = lax.broadcasted_iota(jnp.int32, (tm, 1), 0)

    u = cc_ref[0].astype(F32) * cx_ref[0].astype(F32)
    u_before = ccp_ref[0, BF16_ROWS - 1:BF16_ROWS, :].astype(F32) * cxp_ref[0, BF16_ROWS - 1:BF16_ROWS, :].astype(F32)
    u_after = ccn_ref[0, 0:1, :].astype(F32) * cxn_ref[0, 0:1, :].astype(F32)
    up = jnp.where(local == 0, u_before, pltpu.roll(u, 1, 0))
    un = jnp.where(local == tm - 1, u_after, pltpu.roll(u, tm - 1, 0))
    has_prev = (row != 0) & (row != ctx_len)
    has_next = (row != ctx_len - 1) & (row != seq_len - 1)
    cw = cw_ref[...]
    conv = (jnp.where(has_prev, up, 0.0) * cw[0:1, :] + u * cw[1:2, :]
            + jnp.where(has_next, un, 0.0) * cw[2:3, :])
    yc = (cb_ref[0].astype(F32) * conv).astype(BF16)

    mix = _sigmoid(ga_ref[0].astype(F32)) * jnp.dot(ya_ref[0], wa_ref[...], preferred_element_type=F32)
    mix = mix + _sigmoid(gb_ref[0].astype(F32)) * jnp.dot(yb_ref[0], wb_ref[...], preferred_element_type=F32)
    mix = mix + _sigmoid(gc_ref[0].astype(F32)) * jnp.dot(yc, wc_ref[...], preferred_element_type=F32)
    out = jnp.dot(mix.astype(BF16), wo_ref[...], preferred_element_type=F32)
    o_ref[0] = x_ref[0] + _row_mod(mod_ref, 2, is_ctx, d) * out


def _merge_call(xs, mod, ya, yb, z, cw, wa, wb, wc, wo, ctx_len):
    b, l, d = xs.shape
    tm = _pick_tile(l, 544, BF16_ROWS)
    halo = tm // BF16_ROWS
    n_halo = l // BF16_ROWS

    def zspec(col, width=UNIT):
        return pl.BlockSpec((1, tm, width), lambda bi, ti: (bi, ti, col * UNIT // width))

    def prev(col):
        return pl.BlockSpec((1, BF16_ROWS, UNIT), lambda bi, ti: (bi, jnp.maximum(ti * halo - 1, 0), col))

    def nxt(col):
        return pl.BlockSpec((1, BF16_ROWS, UNIT),
                            lambda bi, ti: (bi, jnp.minimum((ti + 1) * halo, n_halo - 1), col))

    def const(shape):
        return pl.BlockSpec(shape, lambda bi, ti: tuple(0 for _ in shape))

    tok = pl.BlockSpec((1, tm, d), lambda bi, ti: (bi, ti, 0))
    half = pl.BlockSpec((1, tm, UNIT), lambda bi, ti: (bi, ti, 0))
    return pl.pallas_call(
        functools.partial(_merge_kernel, ctx_len=ctx_len, seq_len=l, tm=tm, d=d),
        grid=(b, l // tm),
        in_specs=[tok, pl.BlockSpec((1, 2, N_MOD * d), lambda bi, ti: (bi, 0, 0)), half, half,
                  zspec(COL_CB), zspec(COL_CC), zspec(COL_CX),
                  prev(COL_CC), prev(COL_CX), nxt(COL_CC), nxt(COL_CX),
                  zspec(COL_GA, d), zspec(COL_GB, d), zspec(COL_GC, d),
                  const(cw.shape), const(wa.shape), const(wb.shape), const(wc.shape), const(wo.shape)],
        out_specs=tok,
        out_shape=jax.ShapeDtypeStruct((b, l, d), F32),
        compiler_params=_params("arbitrary", "arbitrary"),
    )(xs, mod, ya, yb, z, z, z, z, z, z, z, z, z, z, cw, wa, wb, wc, wo)


def _mlp_kernel(x_ref, mod_ref, nw_ref, w1_ref, w2_ref, o_ref, *, ctx_len, tm, d, ff_chunk):
    ti = pl.program_id(1)
    row = ti * tm + lax.broadcasted_iota(jnp.int32, (tm, 1), 0)
    is_ctx = row < ctx_len
    x = x_ref[0]
    h = _modulated_norm(x, nw_ref[...], _row_mod(mod_ref, 3, is_ctx, d),
                        _row_mod(mod_ref, 4, is_ctx, d)).astype(BF16)
    acc = jnp.zeros((tm, d), F32)
    for c in range(w1_ref.shape[1] // ff_chunk):
        a = jnp.dot(h, w1_ref[:, c * ff_chunk:(c + 1) * ff_chunk], preferred_element_type=F32)
        a = jnp.maximum(a, 0.0)
        acc = acc + jnp.dot((a * a).astype(BF16), w2_ref[c * ff_chunk:(c + 1) * ff_chunk, :],
                            preferred_element_type=F32)
    o_ref[0] = x + _row_mod(mod_ref, 5, is_ctx, d) * acc


def _mlp_call(xs, mod, nw, w1, w2, ctx_len):
    b, l, d = xs.shape
    tm = _pick_tile(l, 544, BF16_ROWS)
    tok = pl.BlockSpec((1, tm, d), lambda bi, ti: (bi, ti, 0))
    return pl.pallas_call(
        functools.partial(_mlp_kernel, ctx_len=ctx_len, tm=tm, d=d, ff_chunk=1024),
        grid=(b, l // tm),
        in_specs=[tok, pl.BlockSpec((1, 2, N_MOD * d), lambda bi, ti: (bi, 0, 0)),
                  pl.BlockSpec((1, d), lambda bi, ti: (0, 0)),
                  pl.BlockSpec(w1.shape, lambda bi, ti: (0, 0), pipeline_mode=pl.Buffered(1)),
                  pl.BlockSpec(w2.shape, lambda bi, ti: (0, 0), pipeline_mode=pl.Buffered(1))],
        out_specs=tok,
        out_shape=jax.ShapeDtypeStruct((b, l, d), F32),
        compiler_params=_params("arbitrary", "arbitrary"),
    )(xs, mod, nw, w1, w2)


def _reorder_proj(w_in):
    hg = 5 * UNIT
    na = 3 * UNIT
    sc = 3 * UNIT
    gates = w_in[..., hg + na + sc:]
    return jnp.concatenate([gates, w_in[..., :hg + na + sc]], axis=-1)


def kernel(x, c, ctx, c_ctx, ada_w, ada_b, norm1_w, norm2_w, w_in, hgrn_lb_logits, hgrn_onorm_w,
           q_norm_w, k_norm_w, natten_rpb, conv_w, w_branch_a, w_branch_b, w_branch_c, w_out,
           mlp_w1, mlp_w2):
    b, s, d = x.shape
    ctx_len = ctx.shape[1]
    depth = ada_w.shape[0]
    assert s % NA_QUERIES == 0 and s // GRID_W >= NA_BAND_ROWS and ctx_len % NA_QUERIES == 0
    assert ctx_len % (HG_CHUNKS_PER_STEP * HG_CHUNK) == 0 and s % (HG_CHUNKS_PER_STEP * HG_CHUNK) == 0
    assert w_in.shape[-1] == PROJ_UNITS * UNIT

    lb_p = jax.nn.softmax(hgrn_lb_logits.astype(F32), axis=1)
    lower_bounds = jnp.cumsum(lb_p, axis=1) - lb_p[:, :1]

    pad_rows = -(b + 1) % SUBLANES
    cvec = jnp.concatenate([c_ctx[None, :], c, jnp.zeros((pad_rows, d), F32)], axis=0)
    mod_all = _ada_call(cvec, ada_w, ada_b)
    mod_sel = jnp.stack([jnp.broadcast_to(mod_all[:, 0:1], (depth, b, N_MOD * d)),
                         mod_all[:, 1:b + 1]], axis=2)

    cos_t, sin_t = _rope_tables(ctx_len, s)
    bd = jnp.asarray(np.kron(np.eye(NA_HEADS), np.ones((NA_HD, NA_HD))), BF16)
    w_in_b = _reorder_proj(w_in).astype(BF16)

    xs = jnp.concatenate([ctx, x], axis=1)
    for l in range(depth):
        mod = mod_sel[l]
        z = _inproj_call(xs, mod, norm1_w[l][None, :], w_in_b[l], ctx_len)
        o_f = _hgrn_call(z, lower_bounds[0, l][None, :], ctx_len, False)
        ya = _hgrn_call(z, lower_bounds[1, l][None, :], ctx_len, True, o_f, hgrn_onorm_w[l][None, :])
        qn, kn = _qkprep_call(z, cos_t, sin_t, jnp.tile(q_norm_w[l], NA_HEADS)[None, :],
                              jnp.tile(k_norm_w[l], NA_HEADS)[None, :], bd)
        yb = _natten_call(qn, kn, z, _bias_table(natten_rpb[l]), ctx_len)
        xs = _merge_call(xs, mod, ya, yb, z, conv_w[l], w_branch_a[l].astype(BF16),
                         w_branch_b[l].astype(BF16), w_branch_c[l].astype(BF16),
                         w_out[l].astype(BF16), ctx_len)
        xs = _mlp_call(xs, mod, norm2_w[l][None, :], mlp_w1[l].astype(BF16),
                       mlp_w2[l].astype(BF16), ctx_len)
    return xs[:, ctx_len:, :]
```

```python
import functools

import numpy as np
import jax
import jax.numpy as jnp
from jax import lax
from jax.experimental import pallas as pl
from jax.experimental.pallas import tpu as pltpu

GRID_W = 64
HG_HEADS = 4
HG_DK = 128
HG_W = HG_HEADS * HG_DK
HG_CHUNK = 64
HG_LEVELS = 6
HG_CHUNKS_PER_STEP = 4
NA_HEADS = 8
NA_HD = 64
NA_W = NA_HEADS * NA_HD
NA_WIN_R = 8
NA_WIN_C = 16
ROPE_THETA = 10000.0
SC_W = 512
N_MOD = 6
EPS = 1e-6
NEG_BIG = -1e30

LANES = 128
SUBLANES = 8
BF16_ROWS = 16
VMEM_LIMIT = 56 * 1024 * 1024

UNIT = 512
WU_GATES = 0
WU_HQ, WU_HF, WU_HI, WU_HG = 6, 7, 9, 10
WU_NQ, WU_NK, WU_NV = 11, 12, 13
WU_CB, WU_CC, WU_CX = 14, 15, 16
PROJ_UNITS = 17
COL_GA, COL_GB, COL_GC = 0, 2, 4
COL_HQ, COL_HI, COL_HG = 6, 7, 8
COL_NQ, COL_NK, COL_NV = 9, 10, 11
COL_CB, COL_CU = 12, 13
Z_UNITS = 14

F32 = jnp.float32
BF16 = jnp.bfloat16


def _pick_tile(n, target, mult):
    best = None
    for t in range(mult, min(n, target) + 1, mult):
        if n % t == 0:
            best = t
    if best is None:
        raise ValueError(f"no tile for {n} (target {target}, multiple {mult})")
    return best


def _sigmoid(x):
    return 1.0 / (1.0 + jnp.exp(-x))


def _params(*sem):
    return pltpu.CompilerParams(dimension_semantics=sem, vmem_limit_bytes=VMEM_LIMIT)


def _ada_kernel(c_ref, w_ref, b_ref, o_ref):
    c = c_ref[...]
    s = c * _sigmoid(c)
    o_ref[0] = jnp.dot(s, w_ref[0], precision=lax.Precision.HIGHEST,
                       preferred_element_type=F32) + b_ref[0]


def _ada_call(cvec, ada_w, ada_b):
    depth, d, n = ada_w.shape
    rows = cvec.shape[0]
    tn = _pick_tile(n, 1024, LANES)
    return pl.pallas_call(
        _ada_kernel,
        grid=(depth, n // tn),
        in_specs=[
            pl.BlockSpec((rows, d), lambda l, j: (0, 0)),
            pl.BlockSpec((1, d, tn), lambda l, j: (l, 0, j)),
            pl.BlockSpec((1, 1, tn), lambda l, j: (l, 0, j)),
        ],
        out_specs=pl.BlockSpec((1, rows, tn), lambda l, j: (l, 0, j)),
        out_shape=jax.ShapeDtypeStruct((depth, rows, n), F32),
        compiler_params=_params("arbitrary", "arbitrary"),
    )(cvec, ada_w, ada_b.reshape(depth, 1, n))


def _row_mod(mod_ref, idx, is_ctx, d):
    mc = mod_ref[0, 0:1, idx * d:(idx + 1) * d]
    ml = mod_ref[0, 1:2, idx * d:(idx + 1) * d]
    return jnp.where(is_ctx, mc, ml)


def _modulated_norm(x, nw, shift, scale):
    ms = jnp.mean(x * x, axis=-1, keepdims=True)
    return (x * lax.rsqrt(ms + EPS) * nw) * (1.0 + scale) + shift


def _rotate(x, cos, sin_signed, first_half):
    w = x.shape[-1]
    quarter = NA_HD // 4
    partner = jnp.where(first_half, pltpu.roll(x, w - quarter, 1), pltpu.roll(x, quarter, 1))
    return x * cos + partner * sin_signed


def _inproj_kernel(x_ref, mod_ref, nw_ref, w_ref, cos_ref, sin_ref, qw_ref, kw_ref, bd_ref, lb_ref,
                   z_ref, lf_ref, *, ctx_len, tm, d):
    row = pl.program_id(1) * tm + lax.broadcasted_iota(jnp.int32, (tm, 1), 0)
    is_ctx = row < ctx_len
    xn = _modulated_norm(x_ref[0], nw_ref[...], _row_mod(mod_ref, 0, is_ctx, d),
                         _row_mod(mod_ref, 1, is_ctx, d)).astype(BF16)
    lane = lax.broadcasted_iota(jnp.int32, (1, NA_W), 1)
    first_half = (lane % (NA_HD // 2)) < (NA_HD // 4)

    def unit(c):
        return jnp.dot(xn, w_ref[:, c * UNIT:(c + 1) * UNIT], preferred_element_type=F32)

    def put(col, val):
        z_ref[0, :, col * UNIT:(col + 1) * UNIT] = val.astype(z_ref.dtype)

    def head_prep(y, w):
        ms = jnp.dot((y * y).astype(BF16), bd_ref[...], preferred_element_type=F32) * (1.0 / NA_HD)
        return _rotate(y * lax.rsqrt(ms + EPS) * w, cos_ref[...], sin_ref[...], first_half)

    for c in range(6):
        put(COL_GA + c, _sigmoid(unit(WU_GATES + c)))
    q = unit(WU_HQ)
    put(COL_HQ, q * _sigmoid(q))
    put(COL_HI, unit(WU_HI))
    g = unit(WU_HG)
    put(COL_HG, g * _sigmoid(g))
    put(COL_NQ, head_prep(unit(WU_NQ), qw_ref[...]) * (NA_HD ** -0.5))
    put(COL_NK, head_prep(unit(WU_NK), kw_ref[...]))
    put(COL_NV, unit(WU_NV))
    put(COL_CB, unit(WU_CB))
    put(COL_CU, unit(WU_CC) * unit(WU_CX))
    for direction in range(2):
        lb = lb_ref[direction:direction + 1, :]
        forget = lb + (1.0 - lb) * _sigmoid(unit(WU_HF + direction))
        lf_ref[0, :, direction * UNIT:(direction + 1) * UNIT] = jnp.log(forget)


def _inproj_call(xs, mod, nw, w, cos, sin, qw, kw, bd, lb, ctx_len):
    b, l, d = xs.shape
    tm = _pick_tile(l, 544, BF16_ROWS)

    def const(shape):
        return pl.BlockSpec(shape, lambda bi, ti: (0, 0))

    tab = pl.BlockSpec((tm, NA_W), lambda bi, ti: (ti, 0))
    return pl.pallas_call(
        functools.partial(_inproj_kernel, ctx_len=ctx_len, tm=tm, d=d),
        grid=(b, l // tm),
        in_specs=[
            pl.BlockSpec((1, tm, d), lambda bi, ti: (bi, ti, 0)),
            pl.BlockSpec((1, 2, N_MOD * d), lambda bi, ti: (bi, 0, 0)),
            const((1, d)),
            pl.BlockSpec(w.shape, lambda bi, ti: (0, 0), pipeline_mode=pl.Buffered(1)),
            tab, tab, const((1, NA_W)), const((1, NA_W)), const((NA_W, NA_W)), const((2, HG_W)),
        ],
        out_specs=[pl.BlockSpec((1, tm, Z_UNITS * UNIT), lambda bi, ti: (bi, ti, 0)),
                   pl.BlockSpec((1, tm, 2 * HG_W), lambda bi, ti: (bi, ti, 0))],
        out_shape=[jax.ShapeDtypeStruct((b, l, Z_UNITS * UNIT), BF16),
                   jax.ShapeDtypeStruct((b, l, 2 * HG_W), F32)],
        compiler_params=_params("arbitrary", "arbitrary"),
    )(xs, mod, nw, w, cos, sin, qw, kw, bd, lb)


def _boundary_rows(gc, h, reverse):
    t, w = gc.shape
    off = h if reverse else h - 1
    if 2 * h >= SUBLANES:
        g3 = gc.reshape(t // (2 * h), 2 * h, w)
        return jnp.broadcast_to(g3[:, off:off + 1, :], g3.shape).reshape(t, w)
    g3 = gc.reshape(t // SUBLANES, SUBLANES, w)
    sub = lax.broadcasted_iota(jnp.int32, (1, SUBLANES, 1), 1)
    out = None
    for start in range(0, SUBLANES, 2 * h):
        piece = jnp.broadcast_to(g3[:, start + off:start + off + 1, :], g3.shape)
        out = piece if out is None else jnp.where(sub >= start, piece, out)
    return out.reshape(t, w)


def _hgrn_chunk(qs, lf, v, hm_ref, states, reverse):
    t = HG_CHUNK
    kk = 1.0 - jnp.exp(lf)
    row = lax.broadcasted_iota(jnp.int32, (t, 1), 0)
    gc = lf
    for lvl in range(HG_LEVELS):
        dist = 1 << lvl
        if reverse:
            gc = gc + jnp.where(row < t - dist, pltpu.roll(gc, t - dist, 0), 0.0)
        else:
            gc = gc + jnp.where(row >= dist, pltpu.roll(gc, dist, 0), 0.0)

    def nt_dot(a, b):
        return lax.dot_general(a, b, (((1,), (1,)), ((), ())), preferred_element_type=F32)

    def head(a, hh):
        return a[:, hh * HG_DK:(hh + 1) * HG_DK]

    qs_b = qs.astype(BF16)
    kk_b = kk.astype(BF16)
    diag = hm_ref[0] != 0.0
    attn = [jnp.where(diag, nt_dot(head(qs_b, hh), head(kk_b, hh)), 0.0) for hh in range(HG_HEADS)]
    qbit = 0 if reverse else 1
    for lvl in range(HG_LEVELS):
        e = jnp.exp(-jnp.abs(gc - _boundary_rows(gc, 1 << lvl, reverse)))
        is_q = ((row >> lvl) & 1) == qbit
        both = (jnp.where(is_q, qs, kk) * e).astype(BF16)
        mask = hm_ref[lvl + 1] != 0.0
        for hh in range(HG_HEADS):
            attn[hh] = jnp.where(mask, nt_dot(head(both, hh), head(both, hh)), attn[hh])

    g_end = gc[0:1, :] if reverse else gc[t - 1:t, :]
    qd = (qs * jnp.exp(gc)).astype(BF16)
    kd = (kk * jnp.exp(g_end - gc)).astype(BF16)
    d_end = jnp.exp(g_end)
    v_b = v.astype(BF16)

    outs = []
    new_states = []
    for hh in range(HG_HEADS):
        st = states[hh]
        o_h = nt_dot(head(qd, hh), st.astype(BF16))
        o_h = o_h + jnp.dot(attn[hh].astype(BF16), head(v_b, hh), preferred_element_type=F32)
        vt = head(v, hh).T.astype(BF16)
        new_states.append(st * head(d_end, hh) + jnp.dot(vt, head(kd, hh), preferred_element_type=F32))
        outs.append(o_h)
    return outs, new_states


def _hgrn_kernel(*refs, reverse, readout, n_sub):
    if readout:
        q_ref, lf_ref, i_ref, hm_ref, of_ref, g_ref, onw_ref, o_ref, st_ref = refs
    else:
        q_ref, lf_ref, i_ref, hm_ref, o_ref, st_ref = refs
    t = HG_CHUNK

    @pl.when(pl.program_id(1) == 0)
    def _():
        st_ref[...] = jnp.zeros_like(st_ref)

    states = [st_ref[hh] for hh in range(HG_HEADS)]
    for sub in (range(n_sub - 1, -1, -1) if reverse else range(n_sub)):
        rows = slice(sub * t, (sub + 1) * t)
        outs, states = _hgrn_chunk(q_ref[0, rows, :].astype(F32), lf_ref[0, rows, :],
                                   i_ref[0, rows, :].astype(F32), hm_ref, states, reverse)
        if readout:
            gate = g_ref[0, rows, :].astype(F32)
        for hh in range(HG_HEADS):
            cols = slice(hh * HG_DK, (hh + 1) * HG_DK)
            if readout:
                o_h = outs[hh] + of_ref[0, rows, cols]
                ms = jnp.mean(o_h * o_h, axis=-1, keepdims=True)
                y = o_h * lax.rsqrt(ms + EPS) * onw_ref[...]
                o_ref[0, rows, cols] = (y * gate[:, cols]).astype(o_ref.dtype)
            else:
                o_ref[0, rows, cols] = outs[hh]
    for hh in range(HG_HEADS):
        st_ref[hh] = states[hh]


def _hgrn_masks(reverse):
    t = HG_CHUNK
    rt = np.arange(t)[:, None]
    rs = np.arange(t)[None, :]
    qbit = 0 if reverse else 1
    masks = [rt == rs]
    for lvl in range(HG_LEVELS):
        masks.append(((rt >> (lvl + 1)) == (rs >> (lvl + 1))) & (((rt >> lvl) & 1) == qbit)
                     & (((rs >> lvl) & 1) != qbit))
    return jnp.asarray(np.stack(masks), F32)


def _hgrn_call(z, lf, ctx_len, reverse, o_fwd=None, onw=None):
    b, l, _ = z.shape
    t = HG_CHUNK
    rows = HG_CHUNKS_PER_STEP * t
    n_blocks = l // rows
    n_ctx = ctx_len // rows
    readout = o_fwd is not None

    if reverse:
        def block(i):
            return jnp.where(i < n_ctx, n_ctx - 1 - i, n_blocks - 1 + n_ctx - i)
    else:
        def block(i):
            return i

    def zspec(col):
        return pl.BlockSpec((1, rows, UNIT), lambda bi, i: (bi, block(i), col))

    row_spec = pl.BlockSpec((1, rows, HG_W), lambda bi, i: (bi, block(i), 0))
    in_specs = [zspec(COL_HQ), zspec(1 if reverse else 0), zspec(COL_HI),
                pl.BlockSpec((HG_LEVELS + 1, t, t), lambda bi, i: (0, 0, 0))]
    args = [z, lf, z, _hgrn_masks(reverse)]
    if readout:
        in_specs += [row_spec, zspec(COL_HG), pl.BlockSpec((1, HG_DK), lambda bi, i: (0, 0))]
        args += [o_fwd, z, onw]
    return pl.pallas_call(
        functools.partial(_hgrn_kernel, reverse=reverse, readout=readout, n_sub=HG_CHUNKS_PER_STEP),
        grid=(b, n_blocks),
        in_specs=in_specs,
        out_specs=row_spec,
        out_shape=jax.ShapeDtypeStruct((b, l, HG_W), BF16 if readout else F32),
        scratch_shapes=[pltpu.VMEM((HG_HEADS, HG_DK, HG_DK), F32)],
        compiler_params=_params("arbitrary", "arbitrary"),
    )(*args)


NA_ROWS_PER_STEP = 4
NA_QUERIES = NA_ROWS_PER_STEP * GRID_W
NA_BAND_ROWS = NA_WIN_R + NA_ROWS_PER_STEP
NA_BAND = NA_BAND_ROWS * GRID_W
NA_PATTERNS = 3


def _band_start(group_row, rows):
    return jnp.clip(group_row - NA_WIN_R // 2, 0, rows - NA_BAND_ROWS)


def _natten_kernel(q_ref, k_ref, v_ref, bm_ref, o_ref, *, ctx_len, rows):
    i = pl.program_id(1)
    n_ctx_blocks = ctx_len // NA_QUERIES
    pair_w = 2 * NA_HD
    lane = lax.broadcasted_iota(jnp.int32, (1, pair_w), 1)
    low = lane < NA_HD

    def nt_dot(a, b):
        return lax.dot_general(a, b, (((1,), (1,)), ((), ())), preferred_element_type=F32)

    def attend(local):
        if local:
            r = (i - n_ctx_blocks) * NA_ROWS_PER_STEP
            start = pl.multiple_of(ctx_len + _band_start(r, rows) * GRID_W, GRID_W)
        for p in range(NA_HEADS // 2):
            q2 = q_ref[0, :, p * pair_w:(p + 1) * pair_w]
            kc = k_ref[0, 0:ctx_len, p * pair_w:(p + 1) * pair_w]
            vc = v_ref[0, 0:ctx_len, p * pair_w:(p + 1) * pair_w]
            if local:
                kb = k_ref[0, pl.ds(start, NA_BAND), p * pair_w:(p + 1) * pair_w]
                vb = v_ref[0, pl.ds(start, NA_BAND), p * pair_w:(p + 1) * pair_w]
            halves = []
            for hh in range(2):
                qm = jnp.where(low if hh == 0 else ~low, q2, jnp.zeros_like(q2))
                s_ctx = nt_dot(qm, kc)
                m = jnp.max(s_ctx, axis=-1, keepdims=True)
                if local:
                    s_loc = nt_dot(qm, kb) + bm_ref[0, 2 * p + hh]
                    m = jnp.maximum(m, jnp.max(s_loc, axis=-1, keepdims=True))
                p_ctx = jnp.exp(s_ctx - m)
                den = jnp.sum(p_ctx, axis=-1, keepdims=True)
                o = jnp.dot(p_ctx.astype(BF16), vc, preferred_element_type=F32)
                if local:
                    p_loc = jnp.exp(s_loc - m)
                    den = den + jnp.sum(p_loc, axis=-1, keepdims=True)
                    o = o + jnp.dot(p_loc.astype(BF16), vb, preferred_element_type=F32)
                halves.append(o / den)
            o_ref[0, :, p * pair_w:(p + 1) * pair_w] = jnp.where(low, halves[0], halves[1]).astype(o_ref.dtype)

    @pl.when(i < n_ctx_blocks)
    def _():
        attend(False)

    @pl.when(i >= n_ctx_blocks)
    def _():
        attend(True)


def _natten_call(z, bias, ctx_len):
    b, l, _ = z.shape
    rows = (l - ctx_len) // GRID_W
    n_ctx_blocks = ctx_len // NA_QUERIES

    def pattern(i):
        r = jnp.maximum(i - n_ctx_blocks, 0) * NA_ROWS_PER_STEP
        return (r - _band_start(r, rows)) // NA_ROWS_PER_STEP

    def full(col):
        return pl.BlockSpec((1, l, NA_W), lambda bi, i: (bi, 0, col), pipeline_mode=pl.Buffered(1))

    return pl.pallas_call(
        functools.partial(_natten_kernel, ctx_len=ctx_len, rows=rows),
        grid=(b, l // NA_QUERIES),
        in_specs=[pl.BlockSpec((1, NA_QUERIES, NA_W), lambda bi, i: (bi, i, COL_NQ)),
                  full(COL_NK), full(COL_NV),
                  pl.BlockSpec((1, NA_HEADS, NA_QUERIES, NA_BAND), lambda bi, i: (pattern(i), 0, 0, 0))],
        out_specs=pl.BlockSpec((1, NA_QUERIES, NA_W), lambda bi, i: (bi, i, 0)),
        out_shape=jax.ShapeDtypeStruct((b, l, NA_W), BF16),
        compiler_params=_params("arbitrary", "arbitrary"),
    )(z, z, z, bias)


def _bias_kernel(cols_ref, o_ref):
    g = pl.program_id(0)
    for a in range(NA_ROWS_PER_STEP):
        r_rel = NA_ROWS_PER_STEP * g + a
        r0_rel = jnp.clip(r_rel - NA_WIN_R // 2, 0, NA_BAND_ROWS - NA_WIN_R)
        for j in range(NA_BAND_ROWS):
            in_window = (j >= r0_rel) & (j < r0_rel + NA_WIN_R)
            plane = cols_ref[0, jnp.clip(j - r_rel + NA_WIN_R - 1, 0, 2 * NA_WIN_R - 2)]
            o_ref[0, 0, a * GRID_W:(a + 1) * GRID_W, j * GRID_W:(j + 1) * GRID_W] = jnp.where(
                in_window, plane, NEG_BIG)


def _bias_table(rpb):
    qc = np.arange(GRID_W)[:, None]
    kc = np.arange(GRID_W)[None, :]
    c0 = np.clip(qc - NA_WIN_C // 2, 0, GRID_W - NA_WIN_C)
    col_ok = (kc >= c0) & (kc < c0 + NA_WIN_C)
    n_ci = 2 * NA_WIN_C - 1
    n_ri = 2 * NA_WIN_R - 1
    onehot = (kc - qc + NA_WIN_C - 1)[:, :, None] == np.arange(n_ci)
    cols = jnp.einsum('hrc,qkc->hrqk', rpb.astype(F32), jnp.asarray(onehot, F32),
                      precision=lax.Precision.HIGHEST)
    cols = jnp.where(col_ok[None, None], cols, NEG_BIG)
    return pl.pallas_call(
        _bias_kernel,
        grid=(NA_PATTERNS, NA_HEADS),
        in_specs=[pl.BlockSpec((1, n_ri, GRID_W, GRID_W), lambda g, h: (h, 0, 0, 0))],
        out_specs=pl.BlockSpec((1, 1, NA_QUERIES, NA_BAND), lambda g, h: (g, h, 0, 0)),
        out_shape=jax.ShapeDtypeStruct((NA_PATTERNS, NA_HEADS, NA_QUERIES, NA_BAND), F32),
        compiler_params=_params("arbitrary", "arbitrary"),
    )(cols)


def _rope_tables(ctx_len, s):
    tpos = np.arange(s)
    pos = np.stack([tpos // GRID_W, tpos % GRID_W], axis=-1).astype(np.float32)
    half = NA_HD // 2
    inv = (ROPE_THETA ** (-jnp.arange(0, half, 2, dtype=F32) / half))
    ang = jnp.asarray(pos)[:, :, None] * inv
    cos = jnp.cos(ang)
    sin = jnp.sin(ang)
    cos_h = jnp.concatenate([cos, cos], axis=-1).reshape(s, NA_HD)
    sin_h = jnp.concatenate([-sin, sin], axis=-1).reshape(s, NA_HD)
    cos_t = jnp.tile(cos_h, (1, NA_HEADS))
    sin_t = jnp.tile(sin_h, (1, NA_HEADS))
    cos_t = jnp.concatenate([jnp.ones((ctx_len, NA_W), F32), cos_t], axis=0)
    sin_t = jnp.concatenate([jnp.zeros((ctx_len, NA_W), F32), sin_t], axis=0)
    return cos_t, sin_t


def _merge_kernel(x_ref, mod_ref, ya_ref, yb_ref, cb_ref, u_ref, up_ref, un_ref, ga_ref, gb_ref, gc_ref,
                  cw_ref, wa_ref, wb_ref, wc_ref, wo_ref, o_ref, *, ctx_len, seq_len, tm, d):
    ti = pl.program_id(1)
    row = ti * tm + lax.broadcasted_iota(jnp.int32, (tm, 1), 0)
    is_ctx = row < ctx_len
    local = lax.broadcasted_iota(jnp.int32, (tm, 1), 0)

    u = u_ref[0].astype(F32)
    u_before = up_ref[0, BF16_ROWS - 1:BF16_ROWS, :].astype(F32)
    u_after = un_ref[0, 0:1, :].astype(F32)
    up = jnp.where(local == 0, u_before, pltpu.roll(u, 1, 0))
    un = jnp.where(local == tm - 1, u_after, pltpu.roll(u, tm - 1, 0))
    has_prev = (row != 0) & (row != ctx_len)
    has_next = (row != ctx_len - 1) & (row != seq_len - 1)
    cw = cw_ref[...]
    conv = (jnp.where(has_prev, up, 0.0) * cw[0:1, :] + u * cw[1:2, :]
            + jnp.where(has_next, un, 0.0) * cw[2:3, :])
    yc = (cb_ref[0].astype(F32) * conv).astype(BF16)

    mix = ga_ref[0].astype(F32) * jnp.dot(ya_ref[0], wa_ref[...], preferred_element_type=F32)
    mix = mix + gb_ref[0].astype(F32) * jnp.dot(yb_ref[0], wb_ref[...], preferred_element_type=F32)
    mix = mix + gc_ref[0].astype(F32) * jnp.dot(yc, wc_ref[...], preferred_element_type=F32)
    out = jnp.dot(mix.astype(BF16), wo_ref[...], preferred_element_type=F32)
    o_ref[0] = x_ref[0] + _row_mod(mod_ref, 2, is_ctx, d) * out


def _merge_call(xs, mod, ya, yb, z, cw, wa, wb, wc, wo, ctx_len):
    b, l, d = xs.shape
    tm = _pick_tile(l, 544, BF16_ROWS)
    halo = tm // BF16_ROWS
    n_halo = l // BF16_ROWS

    def zspec(col, width=UNIT):
        return pl.BlockSpec((1, tm, width), lambda bi, ti: (bi, ti, col * UNIT // width))

    def prev(col):
        return pl.BlockSpec((1, BF16_ROWS, UNIT), lambda bi, ti: (bi, jnp.maximum(ti * halo - 1, 0), col))

    def nxt(col):
        return pl.BlockSpec((1, BF16_ROWS, UNIT),
                            lambda bi, ti: (bi, jnp.minimum((ti + 1) * halo, n_halo - 1), col))

    def const(shape):
        return pl.BlockSpec(shape, lambda bi, ti: tuple(0 for _ in shape))

    tok = pl.BlockSpec((1, tm, d), lambda bi, ti: (bi, ti, 0))
    half = pl.BlockSpec((1, tm, UNIT), lambda bi, ti: (bi, ti, 0))
    return pl.pallas_call(
        functools.partial(_merge_kernel, ctx_len=ctx_len, seq_len=l, tm=tm, d=d),
        grid=(b, l // tm),
        in_specs=[tok, pl.BlockSpec((1, 2, N_MOD * d), lambda bi, ti: (bi, 0, 0)), half, half,
                  zspec(COL_CB), zspec(COL_CU), prev(COL_CU), nxt(COL_CU),
                  zspec(COL_GA, d), zspec(COL_GB, d), zspec(COL_GC, d),
                  const(cw.shape), const(wa.shape), const(wb.shape), const(wc.shape), const(wo.shape)],
        out_specs=tok,
        out_shape=jax.ShapeDtypeStruct((b, l, d), F32),
        compiler_params=_params("arbitrary", "arbitrary"),
    )(xs, mod, ya, yb, z, z, z, z, z, z, z, cw, wa, wb, wc, wo)


def _mlp_kernel(x_ref, mod_ref, nw_ref, w1_ref, w2_ref, o_ref, *, ctx_len, tm, d, ff_chunk):
    ti = pl.program_id(1)
    row = ti * tm + lax.broadcasted_iota(jnp.int32, (tm, 1), 0)
    is_ctx = row < ctx_len
    x = x_ref[0]
    h = _modulated_norm(x, nw_ref[...], _row_mod(mod_ref, 3, is_ctx, d),
                        _row_mod(mod_ref, 4, is_ctx, d)).astype(BF16)
    acc = jnp.zeros((tm, d), F32)
    for c in range(w1_ref.shape[1] // ff_chunk):
        a = jnp.dot(h, w1_ref[:, c * ff_chunk:(c + 1) * ff_chunk], preferred_element_type=F32)
        a = jnp.maximum(a, 0.0)
        acc = acc + jnp.dot((a * a).astype(BF16), w2_ref[c * ff_chunk:(c + 1) * ff_chunk, :],
                            preferred_element_type=F32)
    o_ref[0] = x + _row_mod(mod_ref, 5, is_ctx, d) * acc


def _mlp_call(xs, mod, nw, w1, w2, ctx_len):
    b, l, d = xs.shape
    tm = _pick_tile(l, 544, BF16_ROWS)
    tok = pl.BlockSpec((1, tm, d), lambda bi, ti: (bi, ti, 0))
    return pl.pallas_call(
        functools.partial(_mlp_kernel, ctx_len=ctx_len, tm=tm, d=d, ff_chunk=1024),
        grid=(b, l // tm),
        in_specs=[tok, pl.BlockSpec((1, 2, N_MOD * d), lambda bi, ti: (bi, 0, 0)),
                  pl.BlockSpec((1, d), lambda bi, ti: (0, 0)),
                  pl.BlockSpec(w1.shape, lambda bi, ti: (0, 0), pipeline_mode=pl.Buffered(1)),
                  pl.BlockSpec(w2.shape, lambda bi, ti: (0, 0), pipeline_mode=pl.Buffered(1))],
        out_specs=tok,
        out_shape=jax.ShapeDtypeStruct((b, l, d), F32),
        compiler_params=_params("arbitrary", "arbitrary"),
    )(xs, mod, nw, w1, w2)


def _reorder_proj(w_in):
    hg = 5 * UNIT
    na = 3 * UNIT
    sc = 3 * UNIT
    gates = w_in[..., hg + na + sc:]
    return jnp.concatenate([gates, w_in[..., :hg + na + sc]], axis=-1)


def kernel(x, c, ctx, c_ctx, ada_w, ada_b, norm1_w, norm2_w, w_in, hgrn_lb_logits, hgrn_onorm_w,
           q_norm_w, k_norm_w, natten_rpb, conv_w, w_branch_a, w_branch_b, w_branch_c, w_out,
           mlp_w1, mlp_w2):
    b, s, d = x.shape
    ctx_len = ctx.shape[1]
    depth = ada_w.shape[0]
    assert s % NA_QUERIES == 0 and s // GRID_W >= NA_BAND_ROWS and ctx_len % NA_QUERIES == 0
    assert ctx_len % (HG_CHUNKS_PER_STEP * HG_CHUNK) == 0 and s % (HG_CHUNKS_PER_STEP * HG_CHUNK) == 0
    assert w_in.shape[-1] == PROJ_UNITS * UNIT

    lb_p = jax.nn.softmax(hgrn_lb_logits.astype(F32), axis=1)
    lower_bounds = jnp.cumsum(lb_p, axis=1) - lb_p[:, :1]

    pad_rows = -(b + 1) % SUBLANES
    cvec = jnp.concatenate([c_ctx[None, :], c, jnp.zeros((pad_rows, d), F32)], axis=0)
    mod_all = _ada_call(cvec, ada_w, ada_b)
    mod_sel = jnp.stack([jnp.broadcast_to(mod_all[:, 0:1], (depth, b, N_MOD * d)),
                         mod_all[:, 1:b + 1]], axis=2)

    cos_t, sin_t = _rope_tables(ctx_len, s)
    bd = jnp.asarray(np.kron(np.eye(NA_HEADS), np.ones((NA_HD, NA_HD))), BF16)
    w_in_b = _reorder_proj(w_in).astype(BF16)

    xs = jnp.concatenate([ctx, x], axis=1)
    for l in range(depth):
        mod = mod_sel[l]
        z, lf = _inproj_call(xs, mod, norm1_w[l][None, :], w_in_b[l], cos_t, sin_t,
                             jnp.tile(q_norm_w[l], NA_HEADS)[None, :],
                             jnp.tile(k_norm_w[l], NA_HEADS)[None, :], bd, lower_bounds[:, l], ctx_len)
        o_f = _hgrn_call(z, lf, ctx_len, False)
        ya = _hgrn_call(z, lf, ctx_len, True, o_f, hgrn_onorm_w[l][None, :])
        yb = _natten_call(z, _bias_table(natten_rpb[l]), ctx_len)
        xs = _merge_call(xs, mod, ya, yb, z, conv_w[l], w_branch_a[l].astype(BF16),
                         w_branch_b[l].astype(BF16), w_branch_c[l].astype(BF16),
                         w_out[l].astype(BF16), ctx_len)
        xs = _mlp_call(xs, mod, norm2_w[l][None, :], mlp_w1[l].astype(BF16),
                       mlp_w2[l].astype(BF16), ctx_len)
    return xs[:, ctx_len:, :]
```

```python
import functools

import numpy as np
import jax
import jax.numpy as jnp
from jax import lax
from jax.experimental import pallas as pl
from jax.experimental.pallas import tpu as pltpu

GRID_W = 64
HG_HEADS = 4
HG_DK = 128
HG_W = HG_HEADS * HG_DK
HG_CHUNK = 64
HG_LEVELS = 6
HG_CHUNKS_PER_STEP = 4
NA_HEADS = 8
NA_HD = 64
NA_W = NA_HEADS * NA_HD
NA_WIN_R = 8
NA_WIN_C = 16
ROPE_THETA = 10000.0
SC_W = 512
N_MOD = 6
EPS = 1e-6
NEG_BIG = -1e30
LOG2E = 1.4426950408889634

LANES = 128
SUBLANES = 8
BF16_ROWS = 16
VMEM_LIMIT = 56 * 1024 * 1024

UNIT = 512
WU_GATES = 0
WU_HQ, WU_HF, WU_HI, WU_HG = 6, 7, 9, 10
WU_NQ, WU_NK, WU_NV = 11, 12, 13
WU_CB, WU_CC, WU_CX = 14, 15, 16
PROJ_UNITS = 17
COL_GA, COL_GB, COL_GC = 0, 2, 4
COL_HQ, COL_HI, COL_HG = 6, 7, 8
COL_NQ, COL_NK, COL_NV = 9, 10, 11
COL_CB, COL_CU = 12, 13
Z_UNITS = 14

F32 = jnp.float32
BF16 = jnp.bfloat16


def _pick_tile(n, target, mult):
    best = None
    for t in range(mult, min(n, target) + 1, mult):
        if n % t == 0:
            best = t
    if best is None:
        raise ValueError(f"no tile for {n} (target {target}, multiple {mult})")
    return best


def _sigmoid(x):
    return 1.0 / (1.0 + jnp.exp(-x))


def _params(*sem):
    return pltpu.CompilerParams(dimension_semantics=sem, vmem_limit_bytes=VMEM_LIMIT)


def _ada_kernel(c_ref, w_ref, b_ref, o_ref):
    c = c_ref[...]
    s = c * _sigmoid(c)
    o_ref[0] = jnp.dot(s, w_ref[0], precision=lax.Precision.HIGHEST,
                       preferred_element_type=F32) + b_ref[0]


def _ada_call(cvec, ada_w, ada_b):
    depth, d, n = ada_w.shape
    rows = cvec.shape[0]
    tn = _pick_tile(n, 1024, LANES)
    return pl.pallas_call(
        _ada_kernel,
        grid=(depth, n // tn),
        in_specs=[
            pl.BlockSpec((rows, d), lambda l, j: (0, 0)),
            pl.BlockSpec((1, d, tn), lambda l, j: (l, 0, j)),
            pl.BlockSpec((1, 1, tn), lambda l, j: (l, 0, j)),
        ],
        out_specs=pl.BlockSpec((1, rows, tn), lambda l, j: (l, 0, j)),
        out_shape=jax.ShapeDtypeStruct((depth, rows, n), F32),
        compiler_params=_params("arbitrary", "arbitrary"),
    )(cvec, ada_w, ada_b.reshape(depth, 1, n))


def _row_mod(mod_ref, idx, is_ctx, d):
    mc = mod_ref[0, 0:1, idx * d:(idx + 1) * d]
    ml = mod_ref[0, 1:2, idx * d:(idx + 1) * d]
    return jnp.where(is_ctx, mc, ml)


def _modulated_norm(x, nw, shift, scale):
    ms = jnp.mean(x * x, axis=-1, keepdims=True)
    return (x * lax.rsqrt(ms + EPS) * nw) * (1.0 + scale) + shift


def _rotate(x, cos, sin_signed, first_half):
    w = x.shape[-1]
    quarter = NA_HD // 4
    partner = jnp.where(first_half, pltpu.roll(x, w - quarter, 1), pltpu.roll(x, quarter, 1))
    return x * cos + partner * sin_signed


def _inproj_kernel(x_ref, mod_ref, nw_ref, w_ref, cos_ref, sin_ref, qw_ref, kw_ref, bd_ref, lb_ref,
                   z_ref, lf_ref, *, ctx_len, tm, d):
    row = pl.program_id(1) * tm + lax.broadcasted_iota(jnp.int32, (tm, 1), 0)
    is_ctx = row < ctx_len
    xn = _modulated_norm(x_ref[0], nw_ref[...], _row_mod(mod_ref, 0, is_ctx, d),
                         _row_mod(mod_ref, 1, is_ctx, d)).astype(BF16)
    lane = lax.broadcasted_iota(jnp.int32, (1, NA_W), 1)
    first_half = (lane % (NA_HD // 2)) < (NA_HD // 4)

    def unit(c):
        return jnp.dot(xn, w_ref[:, c * UNIT:(c + 1) * UNIT], preferred_element_type=F32)

    def put(col, val):
        z_ref[0, :, col * UNIT:(col + 1) * UNIT] = val.astype(z_ref.dtype)

    def head_prep(y, w):
        ms = jnp.dot((y * y).astype(BF16), bd_ref[...], preferred_element_type=F32) * (1.0 / NA_HD)
        return _rotate(y * lax.rsqrt(ms + EPS) * w, cos_ref[...], sin_ref[...], first_half)

    held = {}

    def finish(c, r):
        if c < WU_HQ:
            put(COL_GA + c - WU_GATES, _sigmoid(r))
        elif c == WU_HQ:
            put(COL_HQ, r * _sigmoid(r))
        elif c == WU_HG:
            put(COL_HG, r * _sigmoid(r))
        elif c == WU_NQ:
            put(COL_NQ, head_prep(r, qw_ref[...]) * (NA_HD ** -0.5 * LOG2E))
        elif c == WU_NK:
            put(COL_NK, head_prep(r, kw_ref[...]))
        elif c in (WU_HF, WU_HF + 1):
            direction = c - WU_HF
            lb = lb_ref[direction:direction + 1, :]
            lf_ref[0, :, direction * UNIT:(direction + 1) * UNIT] = jnp.log(lb + (1.0 - lb) * _sigmoid(r))
        elif c == WU_CC:
            held[c] = r
        elif c == WU_CX:
            put(COL_CU, held[WU_CC] * r)
        else:
            put({WU_HI: COL_HI, WU_NV: COL_NV, WU_CB: COL_CB}[c], r)

    order = [WU_NQ, WU_NK, WU_HF, WU_HF + 1, WU_HQ, WU_HG] + list(range(WU_GATES, WU_HQ)) \
        + [WU_CC, WU_CX, WU_HI, WU_NV, WU_CB]
    r = unit(order[0])
    for n, c in enumerate(order):
        r_next = unit(order[n + 1]) if n + 1 < len(order) else None
        finish(c, r)
        r = r_next


def _inproj_call(xs, mod, nw, w, cos, sin, qw, kw, bd, lb, ctx_len):
    b, l, d = xs.shape
    tm = _pick_tile(l, 544, BF16_ROWS)

    def const(shape):
        return pl.BlockSpec(shape, lambda bi, ti: (0, 0))

    tab = pl.BlockSpec((tm, NA_W), lambda bi, ti: (ti, 0))
    return pl.pallas_call(
        functools.partial(_inproj_kernel, ctx_len=ctx_len, tm=tm, d=d),
        grid=(b, l // tm),
        in_specs=[
            pl.BlockSpec((1, tm, d), lambda bi, ti: (bi, ti, 0)),
            pl.BlockSpec((1, 2, N_MOD * d), lambda bi, ti: (bi, 0, 0)),
            const((1, d)),
            pl.BlockSpec(w.shape, lambda bi, ti: (0, 0), pipeline_mode=pl.Buffered(1)),
            tab, tab, const((1, NA_W)), const((1, NA_W)), const((NA_W, NA_W)), const((2, HG_W)),
        ],
        out_specs=[pl.BlockSpec((1, tm, Z_UNITS * UNIT), lambda bi, ti: (bi, ti, 0)),
                   pl.BlockSpec((1, tm, 2 * HG_W), lambda bi, ti: (bi, ti, 0))],
        out_shape=[jax.ShapeDtypeStruct((b, l, Z_UNITS * UNIT), BF16),
                   jax.ShapeDtypeStruct((b, l, 2 * HG_W), F32)],
        compiler_params=_params("arbitrary", "arbitrary"),
    )(xs, mod, nw, w, cos, sin, qw, kw, bd, lb)


def _boundary_rows(gc, h, reverse):
    t, w = gc.shape
    off = h if reverse else h - 1
    if 2 * h >= SUBLANES:
        g3 = gc.reshape(t // (2 * h), 2 * h, w)
        return jnp.broadcast_to(g3[:, off:off + 1, :], g3.shape).reshape(t, w)
    g3 = gc.reshape(t // SUBLANES, SUBLANES, w)
    sub = lax.broadcasted_iota(jnp.int32, (1, SUBLANES, 1), 1)
    out = None
    for start in range(0, SUBLANES, 2 * h):
        piece = jnp.broadcast_to(g3[:, start + off:start + off + 1, :], g3.shape)
        out = piece if out is None else jnp.where(sub >= start, piece, out)
    return out.reshape(t, w)


def _hgrn_chunk(qs, lf, v, hm_ref, states, reverse):
    t = HG_CHUNK
    kk = 1.0 - jnp.exp(lf)
    row = lax.broadcasted_iota(jnp.int32, (t, 1), 0)
    gc = lf
    for lvl in range(HG_LEVELS):
        dist = 1 << lvl
        if reverse:
            gc = gc + jnp.where(row < t - dist, pltpu.roll(gc, t - dist, 0), 0.0)
        else:
            gc = gc + jnp.where(row >= dist, pltpu.roll(gc, dist, 0), 0.0)

    def nt_dot(a, b):
        return lax.dot_general(a, b, (((1,), (1,)), ((), ())), preferred_element_type=F32)

    def head(a, hh):
        return a[:, hh * HG_DK:(hh + 1) * HG_DK]

    qs_b = qs.astype(BF16)
    kk_b = kk.astype(BF16)
    diag = hm_ref[0] != 0.0
    attn = [jnp.where(diag, nt_dot(head(qs_b, hh), head(kk_b, hh)), 0.0) for hh in range(HG_HEADS)]
    qbit = 0 if reverse else 1
    for lvl in range(HG_LEVELS):
        e = jnp.exp(-jnp.abs(gc - _boundary_rows(gc, 1 << lvl, reverse)))
        is_q = ((row >> lvl) & 1) == qbit
        both = (jnp.where(is_q, qs, kk) * e).astype(BF16)
        mask = hm_ref[lvl + 1] != 0.0
        for hh in range(HG_HEADS):
            attn[hh] = jnp.where(mask, nt_dot(head(both, hh), head(both, hh)), attn[hh])

    g_end = gc[0:1, :] if reverse else gc[t - 1:t, :]
    qd = (qs * jnp.exp(gc)).astype(BF16)
    kd = (kk * jnp.exp(g_end - gc)).astype(BF16)
    d_end = jnp.exp(g_end)
    v_b = v.astype(BF16)

    outs = []
    new_states = []
    for hh in range(HG_HEADS):
        st = states[hh]
        o_h = nt_dot(head(qd, hh), st.astype(BF16))
        o_h = o_h + jnp.dot(attn[hh].astype(BF16), head(v_b, hh), preferred_element_type=F32)
        vt = head(v, hh).T.astype(BF16)
        new_states.append(st * head(d_end, hh) + jnp.dot(vt, head(kd, hh), preferred_element_type=F32))
        outs.append(o_h)
    return outs, new_states


def _hgrn_kernel(*refs, reverse, readout, n_sub):
    if readout:
        q_ref, lf_ref, i_ref, hm_ref, of_ref, g_ref, onw_ref, o_ref, st_ref = refs
    else:
        q_ref, lf_ref, i_ref, hm_ref, o_ref, st_ref = refs
    t = HG_CHUNK

    @pl.when(pl.program_id(1) == 0)
    def _():
        st_ref[...] = jnp.zeros_like(st_ref)

    states = [st_ref[hh] for hh in range(HG_HEADS)]
    for sub in (range(n_sub - 1, -1, -1) if reverse else range(n_sub)):
        rows = slice(sub * t, (sub + 1) * t)
        outs, states = _hgrn_chunk(q_ref[0, rows, :].astype(F32), lf_ref[0, rows, :],
                                   i_ref[0, rows, :].astype(F32), hm_ref, states, reverse)
        if readout:
            gate = g_ref[0, rows, :].astype(F32)
        for hh in range(HG_HEADS):
            cols = slice(hh * HG_DK, (hh + 1) * HG_DK)
            if readout:
                o_h = outs[hh] + of_ref[0, rows, cols]
                ms = jnp.mean(o_h * o_h, axis=-1, keepdims=True)
                y = o_h * lax.rsqrt(ms + EPS) * onw_ref[...]
                o_ref[0, rows, cols] = (y * gate[:, cols]).astype(o_ref.dtype)
            else:
                o_ref[0, rows, cols] = outs[hh]
    for hh in range(HG_HEADS):
        st_ref[hh] = states[hh]


def _hgrn_masks(reverse):
    t = HG_CHUNK
    rt = np.arange(t)[:, None]
    rs = np.arange(t)[None, :]
    qbit = 0 if reverse else 1
    masks = [rt == rs]
    for lvl in range(HG_LEVELS):
        masks.append(((rt >> (lvl + 1)) == (rs >> (lvl + 1))) & (((rt >> lvl) & 1) == qbit)
                     & (((rs >> lvl) & 1) != qbit))
    return jnp.asarray(np.stack(masks), F32)


def _hgrn_call(z, lf, ctx_len, reverse, o_fwd=None, onw=None):
    b, l, _ = z.shape
    t = HG_CHUNK
    rows = HG_CHUNKS_PER_STEP * t
    n_blocks = l // rows
    n_ctx = ctx_len // rows
    readout = o_fwd is not None

    if reverse:
        def block(i):
            return jnp.where(i < n_ctx, n_ctx - 1 - i, n_blocks - 1 + n_ctx - i)
    else:
        def block(i):
            return i

    def zspec(col):
        return pl.BlockSpec((1, rows, UNIT), lambda bi, i: (bi, block(i), col))

    row_spec = pl.BlockSpec((1, rows, HG_W), lambda bi, i: (bi, block(i), 0))
    in_specs = [zspec(COL_HQ), zspec(1 if reverse else 0), zspec(COL_HI),
                pl.BlockSpec((HG_LEVELS + 1, t, t), lambda bi, i: (0, 0, 0))]
    args = [z, lf, z, _hgrn_masks(reverse)]
    if readout:
        in_specs += [row_spec, zspec(COL_HG), pl.BlockSpec((1, HG_DK), lambda bi, i: (0, 0))]
        args += [o_fwd, z, onw]
    return pl.pallas_call(
        functools.partial(_hgrn_kernel, reverse=reverse, readout=readout, n_sub=HG_CHUNKS_PER_STEP),
        grid=(b, n_blocks),
        in_specs=in_specs,
        out_specs=row_spec,
        out_shape=jax.ShapeDtypeStruct((b, l, HG_W), BF16 if readout else F32),
        scratch_shapes=[pltpu.VMEM((HG_HEADS, HG_DK, HG_DK), F32)],
        compiler_params=_params("arbitrary", "arbitrary"),
    )(*args)


NA_ROWS_PER_STEP = 4
NA_QUERIES = NA_ROWS_PER_STEP * GRID_W
NA_BAND_ROWS = NA_WIN_R + NA_ROWS_PER_STEP
NA_BAND = NA_BAND_ROWS * GRID_W
NA_PATTERNS = 3


def _band_start(group_row, rows):
    return jnp.clip(group_row - NA_WIN_R // 2, 0, rows - NA_BAND_ROWS)


def _natten_kernel(q_ref, k_ref, v_ref, bm_ref, o_ref, *, ctx_len, rows):
    i = pl.program_id(1)
    n_ctx_blocks = ctx_len // NA_QUERIES
    pair_w = 2 * NA_HD
    lane = lax.broadcasted_iota(jnp.int32, (1, pair_w), 1)
    low = lane < NA_HD

    def nt_dot(a, b):
        return lax.dot_general(a, b, (((1,), (1,)), ((), ())), preferred_element_type=F32)

    def attend(local):
        if local:
            r = (i - n_ctx_blocks) * NA_ROWS_PER_STEP
            start = pl.multiple_of(ctx_len + _band_start(r, rows) * GRID_W, GRID_W)

        def pair_cols(ref, row_slice, p):
            return ref[0, row_slice, p * pair_w:(p + 1) * pair_w]

        def gather(ref, p):
            ctx_part = pair_cols(ref, slice(0, ctx_len), p)
            if not local:
                return ctx_part
            return jnp.concatenate([pair_cols(ref, pl.ds(start, NA_BAND), p), ctx_part], axis=0)

        def scores(h):
            p, hh = divmod(h, 2)
            q2 = pair_cols(q_ref, slice(None), p)
            qm = jnp.where(low if hh == 0 else ~low, q2, jnp.zeros_like(q2))
            s = nt_dot(qm, gather(k_ref, p))
            return s + bm_ref[0, h] if local else s

        s_next = scores(0)
        halves = []
        for h in range(NA_HEADS):
            s = s_next
            if h + 1 < NA_HEADS:
                s_next = scores(h + 1)
            e = jnp.exp2(s - jnp.max(s, axis=-1, keepdims=True))
            o = jnp.dot(e.astype(BF16), gather(v_ref, h // 2), preferred_element_type=F32)
            halves.append(o / jnp.sum(e, axis=-1, keepdims=True))
            if h % 2 == 1:
                p = h // 2
                o_ref[0, :, p * pair_w:(p + 1) * pair_w] = jnp.where(low, *halves).astype(o_ref.dtype)
                halves = []

    @pl.when(i < n_ctx_blocks)
    def _():
        attend(False)

    @pl.when(i >= n_ctx_blocks)
    def _():
        attend(True)


def _natten_call(z, bias, ctx_len):
    b, l, _ = z.shape
    rows = (l - ctx_len) // GRID_W
    n_ctx_blocks = ctx_len // NA_QUERIES

    def pattern(i):
        r = jnp.maximum(i - n_ctx_blocks, 0) * NA_ROWS_PER_STEP
        return (r - _band_start(r, rows)) // NA_ROWS_PER_STEP

    def full(col):
        return pl.BlockSpec((1, l, NA_W), lambda bi, i: (bi, 0, col), pipeline_mode=pl.Buffered(1))

    return pl.pallas_call(
        functools.partial(_natten_kernel, ctx_len=ctx_len, rows=rows),
        grid=(b, l // NA_QUERIES),
        in_specs=[pl.BlockSpec((1, NA_QUERIES, NA_W), lambda bi, i: (bi, i, COL_NQ)),
                  full(COL_NK), full(COL_NV),
                  pl.BlockSpec((1, NA_HEADS, NA_QUERIES, NA_BAND + ctx_len),
                               lambda bi, i: (pattern(i), 0, 0, 0))],
        out_specs=pl.BlockSpec((1, NA_QUERIES, NA_W), lambda bi, i: (bi, i, 0)),
        out_shape=jax.ShapeDtypeStruct((b, l, NA_W), BF16),
        compiler_params=_params("arbitrary", "arbitrary"),
    )(z, z, z, bias)


def _bias_kernel(cols_ref, o_ref):
    g = pl.program_id(0)
    for a in range(NA_ROWS_PER_STEP):
        r_rel = NA_ROWS_PER_STEP * g + a
        r0_rel = jnp.clip(r_rel - NA_WIN_R // 2, 0, NA_BAND_ROWS - NA_WIN_R)
        for j in range(NA_BAND_ROWS):
            in_window = (j >= r0_rel) & (j < r0_rel + NA_WIN_R)
            plane = cols_ref[0, jnp.clip(j - r_rel + NA_WIN_R - 1, 0, 2 * NA_WIN_R - 2)]
            o_ref[0, 0, a * GRID_W:(a + 1) * GRID_W, j * GRID_W:(j + 1) * GRID_W] = jnp.where(
                in_window, plane, NEG_BIG)
    o_ref[0, 0, :, NA_BAND:] = jnp.zeros((NA_QUERIES, o_ref.shape[3] - NA_BAND), F32)


def _bias_table(rpb, ctx_len):
    qc = np.arange(GRID_W)[:, None]
    kc = np.arange(GRID_W)[None, :]
    c0 = np.clip(qc - NA_WIN_C // 2, 0, GRID_W - NA_WIN_C)
    col_ok = (kc >= c0) & (kc < c0 + NA_WIN_C)
    n_ci = 2 * NA_WIN_C - 1
    n_ri = 2 * NA_WIN_R - 1
    onehot = (kc - qc + NA_WIN_C - 1)[:, :, None] == np.arange(n_ci)
    cols = jnp.einsum('hrc,qkc->hrqk', rpb.astype(F32), jnp.asarray(onehot, F32),
                      precision=lax.Precision.HIGHEST)
    cols = jnp.where(col_ok[None, None], cols * LOG2E, NEG_BIG)
    n_keys = NA_BAND + ctx_len
    return pl.pallas_call(
        _bias_kernel,
        grid=(NA_PATTERNS, NA_HEADS),
        in_specs=[pl.BlockSpec((1, n_ri, GRID_W, GRID_W), lambda g, h: (h, 0, 0, 0))],
        out_specs=pl.BlockSpec((1, 1, NA_QUERIES, n_keys), lambda g, h: (g, h, 0, 0)),
        out_shape=jax.ShapeDtypeStruct((NA_PATTERNS, NA_HEADS, NA_QUERIES, n_keys), F32),
        compiler_params=_params("arbitrary", "arbitrary"),
    )(cols)


def _rope_tables(ctx_len, s):
    tpos = np.arange(s)
    pos = np.stack([tpos // GRID_W, tpos % GRID_W], axis=-1).astype(np.float32)
    half = NA_HD // 2
    inv = (ROPE_THETA ** (-jnp.arange(0, half, 2, dtype=F32) / half))
    ang = jnp.asarray(pos)[:, :, None] * inv
    cos = jnp.cos(ang)
    sin = jnp.sin(ang)
    cos_h = jnp.concatenate([cos, cos], axis=-1).reshape(s, NA_HD)
    sin_h = jnp.concatenate([-sin, sin], axis=-1).reshape(s, NA_HD)
    cos_t = jnp.tile(cos_h, (1, NA_HEADS))
    sin_t = jnp.tile(sin_h, (1, NA_HEADS))
    cos_t = jnp.concatenate([jnp.ones((ctx_len, NA_W), F32), cos_t], axis=0)
    sin_t = jnp.concatenate([jnp.zeros((ctx_len, NA_W), F32), sin_t], axis=0)
    return cos_t, sin_t


def _merge_kernel(x_ref, mod_ref, ya_ref, yb_ref, cb_ref, u_ref, up_ref, un_ref, ga_ref, gb_ref, gc_ref,
                  cw_ref, wa_ref, wb_ref, wc_ref, wo_ref, o_ref, *, ctx_len, seq_len, tm, d):
    ti = pl.program_id(1)
    row = ti * tm + lax.broadcasted_iota(jnp.int32, (tm, 1), 0)
    is_ctx = row < ctx_len
    local = lax.broadcasted_iota(jnp.int32, (tm, 1), 0)

    u = u_ref[0].astype(F32)
    u_before = up_ref[0, BF16_ROWS - 1:BF16_ROWS, :].astype(F32)
    u_after = un_ref[0, 0:1, :].astype(F32)
    up = jnp.where(local == 0, u_before, pltpu.roll(u, 1, 0))
    un = jnp.where(local == tm - 1, u_after, pltpu.roll(u, tm - 1, 0))
    has_prev = (row != 0) & (row != ctx_len)
    has_next = (row != ctx_len - 1) & (row != seq_len - 1)
    cw = cw_ref[...]
    conv = (jnp.where(has_prev, up, 0.0) * cw[0:1, :] + u * cw[1:2, :]
            + jnp.where(has_next, un, 0.0) * cw[2:3, :])
    yc = (cb_ref[0].astype(F32) * conv).astype(BF16)

    mix = ga_ref[0].astype(F32) * jnp.dot(ya_ref[0], wa_ref[...], preferred_element_type=F32)
    mix = mix + gb_ref[0].astype(F32) * jnp.dot(yb_ref[0], wb_ref[...], preferred_element_type=F32)
    mix = mix + gc_ref[0].astype(F32) * jnp.dot(yc, wc_ref[...], preferred_element_type=F32)
    out = jnp.dot(mix.astype(BF16), wo_ref[...], preferred_element_type=F32)
    o_ref[0] = x_ref[0] + _row_mod(mod_ref, 2, is_ctx, d) * out


def _merge_call(xs, mod, ya, yb, z, cw, wa, wb, wc, wo, ctx_len):
    b, l, d = xs.shape
    tm = _pick_tile(l, 544, BF16_ROWS)
    halo = tm // BF16_ROWS
    n_halo = l // BF16_ROWS

    def zspec(col, width=UNIT):
        return pl.BlockSpec((1, tm, width), lambda bi, ti: (bi, ti, col * UNIT // width))

    def prev(col):
        return pl.BlockSpec((1, BF16_ROWS, UNIT), lambda bi, ti: (bi, jnp.maximum(ti * halo - 1, 0), col))

    def nxt(col):
        return pl.BlockSpec((1, BF16_ROWS, UNIT),
                            lambda bi, ti: (bi, jnp.minimum((ti + 1) * halo, n_halo - 1), col))

    def const(shape):
        return pl.BlockSpec(shape, lambda bi, ti: tuple(0 for _ in shape))

    tok = pl.BlockSpec((1, tm, d), lambda bi, ti: (bi, ti, 0))
    half = pl.BlockSpec((1, tm, UNIT), lambda bi, ti: (bi, ti, 0))
    return pl.pallas_call(
        functools.partial(_merge_kernel, ctx_len=ctx_len, seq_len=l, tm=tm, d=d),
        grid=(b, l // tm),
        in_specs=[tok, pl.BlockSpec((1, 2, N_MOD * d), lambda bi, ti: (bi, 0, 0)), half, half,
                  zspec(COL_CB), zspec(COL_CU), prev(COL_CU), nxt(COL_CU),
                  zspec(COL_GA, d), zspec(COL_GB, d), zspec(COL_GC, d),
                  const(cw.shape), const(wa.shape), const(wb.shape), const(wc.shape), const(wo.shape)],
        out_specs=tok,
        out_shape=jax.ShapeDtypeStruct((b, l, d), F32),
        compiler_params=_params("arbitrary", "arbitrary"),
    )(xs, mod, ya, yb, z, z, z, z, z, z, z, cw, wa, wb, wc, wo)


def _mlp_kernel(x_ref, mod_ref, nw_ref, w1_ref, w2_ref, o_ref, *, ctx_len, tm, d, ff_chunk):
    ti = pl.program_id(1)
    row = ti * tm + lax.broadcasted_iota(jnp.int32, (tm, 1), 0)
    is_ctx = row < ctx_len
    x = x_ref[0]
    h = _modulated_norm(x, nw_ref[...], _row_mod(mod_ref, 3, is_ctx, d),
                        _row_mod(mod_ref, 4, is_ctx, d)).astype(BF16)
    acc = jnp.zeros((tm, d), F32)
    for c in range(w1_ref.shape[1] // ff_chunk):
        a = jnp.dot(h, w1_ref[:, c * ff_chunk:(c + 1) * ff_chunk], preferred_element_type=F32)
        a = jnp.maximum(a, 0.0)
        acc = acc + jnp.dot((a * a).astype(BF16), w2_ref[c * ff_chunk:(c + 1) * ff_chunk, :],
                            preferred_element_type=F32)
    o_ref[0] = x + _row_mod(mod_ref, 5, is_ctx, d) * acc


def _mlp_call(xs, mod, nw, w1, w2, ctx_len):
    b, l, d = xs.shape
    tm = _pick_tile(l, 544, BF16_ROWS)
    tok = pl.BlockSpec((1, tm, d), lambda bi, ti: (bi, ti, 0))
    return pl.pallas_call(
        functools.partial(_mlp_kernel, ctx_len=ctx_len, tm=tm, d=d, ff_chunk=1024),
        grid=(b, l // tm),
        in_specs=[tok, pl.BlockSpec((1, 2, N_MOD * d), lambda bi, ti: (bi, 0, 0)),
                  pl.BlockSpec((1, d), lambda bi, ti: (0, 0)),
                  pl.BlockSpec(w1.shape, lambda bi, ti: (0, 0), pipeline_mode=pl.Buffered(1)),
                  pl.BlockSpec(w2.shape, lambda bi, ti: (0, 0), pipeline_mode=pl.Buffered(1))],
        out_specs=tok,
        out_shape=jax.ShapeDtypeStruct((b, l, d), F32),
        compiler_params=_params("arbitrary", "arbitrary"),
    )(xs, mod, nw, w1, w2)


def _reorder_proj(w_in):
    hg = 5 * UNIT
    na = 3 * UNIT
    sc = 3 * UNIT
    gates = w_in[..., hg + na + sc:]
    return jnp.concatenate([gates, w_in[..., :hg + na + sc]], axis=-1)


def kernel(x, c, ctx, c_ctx, ada_w, ada_b, norm1_w, norm2_w, w_in, hgrn_lb_logits, hgrn_onorm_w,
           q_norm_w, k_norm_w, natten_rpb, conv_w, w_branch_a, w_branch_b, w_branch_c, w_out,
           mlp_w1, mlp_w2):
    b, s, d = x.shape
    ctx_len = ctx.shape[1]
    depth = ada_w.shape[0]
    assert s % NA_QUERIES == 0 and s // GRID_W >= NA_BAND_ROWS and ctx_len % NA_QUERIES == 0
    assert ctx_len % (HG_CHUNKS_PER_STEP * HG_CHUNK) == 0 and s % (HG_CHUNKS_PER_STEP * HG_CHUNK) == 0
    assert w_in.shape[-1] == PROJ_UNITS * UNIT

    lb_p = jax.nn.softmax(hgrn_lb_logits.astype(F32), axis=1)
    lower_bounds = jnp.cumsum(lb_p, axis=1) - lb_p[:, :1]

    pad_rows = -(b + 1) % SUBLANES
    cvec = jnp.concatenate([c_ctx[None, :], c, jnp.zeros((pad_rows, d), F32)], axis=0)
    mod_all = _ada_call(cvec, ada_w, ada_b)
    mod_sel = jnp.stack([jnp.broadcast_to(mod_all[:, 0:1], (depth, b, N_MOD * d)),
                         mod_all[:, 1:b + 1]], axis=2)

    cos_t, sin_t = _rope_tables(ctx_len, s)
    bd = jnp.asarray(np.kron(np.eye(NA_HEADS), np.ones((NA_HD, NA_HD))), BF16)
    w_in_b = _reorder_proj(w_in).astype(BF16)

    xs = jnp.concatenate([ctx, x], axis=1)
    for l in range(depth):
        mod = mod_sel[l]
        z, lf = _inproj_call(xs, mod, norm1_w[l][None, :], w_in_b[l], cos_t, sin_t,
                             jnp.tile(q_norm_w[l], NA_HEADS)[None, :],
                             jnp.tile(k_norm_w[l], NA_HEADS)[None, :], bd, lower_bounds[:, l], ctx_len)
        o_f = _hgrn_call(z, lf, ctx_len, False)
        ya = _hgrn_call(z, lf, ctx_len, True, o_f, hgrn_onorm_w[l][None, :])
        yb = _natten_call(z, _bias_table(natten_rpb[l], ctx_len), ctx_len)
        xs = _merge_call(xs, mod, ya, yb, z, conv_w[l], w_branch_a[l].astype(BF16),
                         w_branch_b[l].astype(BF16), w_branch_c[l].astype(BF16),
                         w_out[l].astype(BF16), ctx_len)
        xs = _mlp_call(xs, mod, norm2_w[l][None, :], mlp_w1[l].astype(BF16),
                       mlp_w2[l].astype(BF16), ctx_len)
    return xs[:, ctx_len:, :]
```

```python
import functools

import numpy as np
import jax
import jax.numpy as jnp
from jax import lax
from jax.experimental import pallas as pl
from jax.experimental.pallas import tpu as pltpu

GRID_W = 64
HG_HEADS = 4
HG_DK = 128
HG_W = HG_HEADS * HG_DK
HG_CHUNK = 64
HG_LEVELS = 6
HG_CHUNKS_PER_STEP = 4
NA_HEADS = 8
NA_HD = 64
NA_W = NA_HEADS * NA_HD
NA_WIN_R = 8
NA_WIN_C = 16
ROPE_THETA = 10000.0
SC_W = 512
N_MOD = 6
EPS = 1e-6
NEG_BIG = -1e30
LOG2E = 1.4426950408889634

LANES = 128
SUBLANES = 8
BF16_ROWS = 16
VMEM_LIMIT = 56 * 1024 * 1024

UNIT = 512
PROJ_UNITS = 17
W_UNIT_ORDER = (11, 12, 13, 14, 15, 16, 0, 4, 5, 6, 1, 2, 9, 10, 3, 7, 8)
WU_GATES, WU_SILU, WU_QK, WU_FORGET, WU_CONV, WU_RAW, WU_CB = 0, 6, 8, 10, 12, 14, 16
COL_GA, COL_GB, COL_GC = 0, 2, 4
COL_HQ, COL_HG = 6, 7
COL_NQ, COL_NK = 8, 9
COL_HI, COL_NV = 10, 11
COL_CB, COL_CU = 12, 13
Z_UNITS = 14

F32 = jnp.float32
BF16 = jnp.bfloat16


def _pick_tile(n, target, mult):
    best = None
    for t in range(mult, min(n, target) + 1, mult):
        if n % t == 0:
            best = t
    if best is None:
        raise ValueError(f"no tile for {n} (target {target}, multiple {mult})")
    return best


def _token_tile(l):
    return _pick_tile(l, 544, BF16_ROWS)


def _sigmoid(x):
    return 1.0 / (1.0 + jnp.exp(-x))


def _gate_sigmoid(x):
    return 0.5 * jnp.tanh(0.5 * x) + 0.5


def _params(*sem):
    return pltpu.CompilerParams(dimension_semantics=sem, vmem_limit_bytes=VMEM_LIMIT)


def _ada_kernel(c_ref, w_ref, b_ref, o_ref):
    c = c_ref[...]
    s = c * _sigmoid(c)
    o_ref[0] = jnp.dot(s, w_ref[0], precision=lax.Precision.HIGHEST,
                       preferred_element_type=F32) + b_ref[0]


def _ada_call(cvec, ada_w, ada_b):
    depth, d, n = ada_w.shape
    rows = cvec.shape[0]
    tn = _pick_tile(n, 1024, LANES)
    return pl.pallas_call(
        _ada_kernel,
        grid=(depth, n // tn),
        in_specs=[
            pl.BlockSpec((rows, d), lambda l, j: (0, 0)),
            pl.BlockSpec((1, d, tn), lambda l, j: (l, 0, j)),
            pl.BlockSpec((1, 1, tn), lambda l, j: (l, 0, j)),
        ],
        out_specs=pl.BlockSpec((1, rows, tn), lambda l, j: (l, 0, j)),
        out_shape=jax.ShapeDtypeStruct((depth, rows, n), F32),
        compiler_params=_params("arbitrary", "arbitrary"),
    )(cvec, ada_w, ada_b.reshape(depth, 1, n))


def _segment_mod(mod_ref, idx, d):
    mc = mod_ref[0, 0:1, idx * d:(idx + 1) * d]
    ml = mod_ref[0, 1:2, idx * d:(idx + 1) * d]
    return jnp.where(pl.program_id(1) == 0, mc, ml), ml


def _by_segment(x, ctx_len, head_fn, rest_fn):
    return jnp.concatenate([head_fn(x[:ctx_len]), rest_fn(x[ctx_len:])], axis=0)


def _modulated_norm(x, nw, mod_ref, i_shift, i_scale, ctx_len):
    d = x.shape[-1]
    ms = jnp.mean(x * x, axis=-1, keepdims=True)
    xn = x * lax.rsqrt(ms + EPS)
    shifts = _segment_mod(mod_ref, i_shift, d)
    gains = [nw * (1.0 + s) for s in _segment_mod(mod_ref, i_scale, d)]
    return _by_segment(xn, ctx_len, lambda a: a * gains[0] + shifts[0], lambda a: a * gains[1] + shifts[1])


def _gated_residual(x, y, mod_ref, i_gate, ctx_len):
    gates = _segment_mod(mod_ref, i_gate, x.shape[-1])
    return x + _by_segment(y, ctx_len, lambda a: a * gates[0], lambda a: a * gates[1])


def _rotate(x, cos, sin_signed, first_half):
    w = x.shape[-1]
    quarter = NA_HD // 4
    partner = jnp.where(first_half, pltpu.roll(x, w - quarter, 1), pltpu.roll(x, quarter, 1))
    return x * cos + partner * sin_signed


def _inproj_kernel(x_ref, mod_ref, nw_ref, w_ref, cos_ref, sin_ref, qw_ref, kw_ref, bd_ref, lb_ref,
                   z_ref, lf_ref, *, ctx_len, tm, d):
    xn = _modulated_norm(x_ref[0], nw_ref[...], mod_ref, 0, 1, ctx_len).astype(BF16)
    lane = lax.broadcasted_iota(jnp.int32, (1, NA_W), 1)
    first_half = (lane % (NA_HD // 2)) < (NA_HD // 4)

    def project(u0, n_units):
        return jnp.dot(xn, w_ref[:, u0 * UNIT:(u0 + n_units) * UNIT], preferred_element_type=F32)

    def put(col, val):
        z_ref[0, :, col * UNIT:col * UNIT + val.shape[1]] = val.astype(z_ref.dtype)

    def head_prep(y, sq, w):
        ms = jnp.dot(sq, bd_ref[...], preferred_element_type=F32) * (1.0 / NA_HD)
        return _rotate(y * lax.rsqrt(ms + EPS) * w, cos_ref[...], sin_ref[...], first_half)

    def finish_forget(r):
        lb = lb_ref[...]
        lf_ref[0] = jnp.log(lb + (1.0 - lb) * _sigmoid(r))

    r_qk = project(WU_QK, 2)
    r_forget = project(WU_FORGET, 2)
    sq = (r_qk * r_qk).astype(BF16)
    r_silu = project(WU_SILU, 2)
    finish_forget(r_forget)
    q_ready = head_prep(r_qk[:, :UNIT], sq[:, :UNIT], qw_ref[...]) * (NA_HD ** -0.5 * LOG2E)
    k_ready = head_prep(r_qk[:, UNIT:], sq[:, UNIT:], kw_ref[...])
    r = project(WU_GATES, 2)
    put(COL_NQ, q_ready)
    put(COL_NK, k_ready)
    put(COL_HQ, r_silu * _gate_sigmoid(r_silu))
    plan = [(WU_GATES + 2, 2, lambda g: put(COL_GA, _gate_sigmoid(g))),
            (WU_GATES + 4, 2, lambda g: put(COL_GB, _gate_sigmoid(g))),
            (WU_CONV, 2, lambda g: put(COL_GC, _gate_sigmoid(g))),
            (WU_RAW, 2, lambda cx: put(COL_CU, cx[:, :UNIT] * cx[:, UNIT:])),
            (WU_CB, 1, lambda raw: put(COL_HI, raw))]
    for u0, n_units, finish_previous in plan:
        r_next = project(u0, n_units)
        finish_previous(r)
        r = r_next
    put(COL_CB, r)


def _inproj_call(xs, mod, nw, w, cos, sin, qw, kw, bd, lb, ctx_len):
    b, l, d = xs.shape
    tm = _token_tile(l)

    def const(shape):
        return pl.BlockSpec(shape, lambda bi, ti: (0, 0))

    tab = pl.BlockSpec((tm, NA_W), lambda bi, ti: (ti, 0))
    return pl.pallas_call(
        functools.partial(_inproj_kernel, ctx_len=ctx_len, tm=tm, d=d),
        grid=(b, l // tm),
        in_specs=[
            pl.BlockSpec((1, tm, d), lambda bi, ti: (bi, ti, 0)),
            pl.BlockSpec((1, 2, N_MOD * d), lambda bi, ti: (bi, 0, 0)),
            const((1, d)),
            pl.BlockSpec(w.shape, lambda bi, ti: (0, 0), pipeline_mode=pl.Buffered(1)),
            tab, tab, const((1, NA_W)), const((1, NA_W)), const((NA_W, NA_W)), const((1, 2 * HG_W)),
        ],
        out_specs=[pl.BlockSpec((1, tm, Z_UNITS * UNIT), lambda bi, ti: (bi, ti, 0)),
                   pl.BlockSpec((1, tm, 2 * HG_W), lambda bi, ti: (bi, ti, 0))],
        out_shape=[jax.ShapeDtypeStruct((b, l, Z_UNITS * UNIT), BF16),
                   jax.ShapeDtypeStruct((b, l, 2 * HG_W), F32)],
        compiler_params=_params("arbitrary", "arbitrary"),
    )(xs, mod, nw, w, cos, sin, qw, kw, bd, lb)


def _boundary_rows(gc, h, reverse):
    t, w = gc.shape
    off = h if reverse else h - 1
    if 2 * h >= SUBLANES:
        g3 = gc.reshape(t // (2 * h), 2 * h, w)
        return jnp.broadcast_to(g3[:, off:off + 1, :], g3.shape).reshape(t, w)
    g3 = gc.reshape(t // SUBLANES, SUBLANES, w)
    sub = lax.broadcasted_iota(jnp.int32, (1, SUBLANES, 1), 1)
    out = None
    for start in range(0, SUBLANES, 2 * h):
        piece = jnp.broadcast_to(g3[:, start + off:start + off + 1, :], g3.shape)
        out = piece if out is None else jnp.where(sub >= start, piece, out)
    return out.reshape(t, w)


def _scan_cumsum(lf, row, reverse):
    t = lf.shape[0]
    within = row & (SUBLANES - 1)
    g = lf
    for dist in (1, 2, 4):
        if reverse:
            g = g + jnp.where(within < SUBLANES - dist, pltpu.roll(g, t - dist, 0), 0.0)
        else:
            g = g + jnp.where(within >= dist, pltpu.roll(g, dist, 0), 0.0)
    groups = list(range(t // SUBLANES))
    pieces = {}
    carry = None
    for gi in (reversed(groups) if reverse else groups):
        piece = g[gi * SUBLANES:(gi + 1) * SUBLANES]
        if carry is not None:
            piece = piece + carry
        pieces[gi] = piece
        carry = piece[0:1] if reverse else piece[SUBLANES - 1:SUBLANES]
    return jnp.concatenate([pieces[gi] for gi in groups], axis=0)


def _level_operand(qs, kk, gc, h, reverse):
    pieces = []
    for s0 in range(0, gc.shape[0], 2 * h):
        first, second = slice(s0, s0 + h), slice(s0 + h, s0 + 2 * h)
        if reverse:
            gb = gc[s0 + h:s0 + h + 1]
            pieces += [qs[first] * jnp.exp(gc[first] - gb), kk[second] * jnp.exp(gb - gc[second])]
        else:
            gb = gc[s0 + h - 1:s0 + h]
            pieces += [kk[first] * jnp.exp(gb - gc[first]), qs[second] * jnp.exp(gc[second] - gb)]
    return jnp.concatenate(pieces, axis=0).astype(BF16)


def _hgrn_chunk(qs, lf, v, masks, reverse):
    t = HG_CHUNK
    forget = jnp.exp(lf)
    kk = 1.0 - forget
    row = lax.broadcasted_iota(jnp.int32, (t, 1), 0)
    gc = _scan_cumsum(lf, row, reverse)

    def nt_dot(a, b):
        return lax.dot_general(a, b, (((1,), (1,)), ((), ())), preferred_element_type=F32)

    def head(a, hh):
        return a[:, hh * HG_DK:(hh + 1) * HG_DK]

    qs_b = qs.astype(BF16)
    kk_b = kk.astype(BF16)
    attn = [jnp.where(masks[0], nt_dot(head(qs_b, hh), head(kk_b, hh)), 0.0) for hh in range(HG_HEADS)]
    qbit = 0 if reverse else 1
    for lvl in range(HG_LEVELS):
        is_q = ((row >> lvl) & 1) == qbit
        if lvl == 0:
            both = jnp.where(is_q, qs * forget, kk).astype(BF16)
        elif (1 << lvl) >= SUBLANES:
            both = _level_operand(qs, kk, gc, 1 << lvl, reverse)
        else:
            e = jnp.exp(-jnp.abs(gc - _boundary_rows(gc, 1 << lvl, reverse)))
            both = (jnp.where(is_q, qs, kk) * e).astype(BF16)
        for hh in range(HG_HEADS):
            attn[hh] = jnp.where(masks[lvl + 1], nt_dot(head(both, hh), head(both, hh)), attn[hh])

    g_end = gc[0:1, :] if reverse else gc[t - 1:t, :]
    qd = (qs * jnp.exp(gc)).astype(BF16)
    kd = (kk * jnp.exp(g_end - gc)).astype(BF16)
    d_end = jnp.exp(g_end)
    v_b = v.astype(BF16)

    intra = []
    update = []
    for hh in range(HG_HEADS):
        intra.append(jnp.dot(attn[hh].astype(BF16), head(v_b, hh), preferred_element_type=F32))
        vt = head(v, hh).T.astype(BF16)
        update.append(jnp.dot(vt, head(kd, hh), preferred_element_type=F32))
    return intra, qd, d_end, update


def _hgrn_kernel(*refs, reverse, readout, n_sub):
    if readout:
        q_ref, lf_ref, i_ref, hm_ref, of_ref, g_ref, onw_ref, o_ref, st_ref = refs
    else:
        q_ref, lf_ref, i_ref, hm_ref, o_ref, st_ref = refs
    t = HG_CHUNK

    def nt_dot(a, b):
        return lax.dot_general(a, b, (((1,), (1,)), ((), ())), preferred_element_type=F32)

    @pl.when(pl.program_id(1) == 0)
    def _():
        st_ref[...] = jnp.zeros_like(st_ref)

    order = list(range(n_sub - 1, -1, -1) if reverse else range(n_sub))
    masks = [hm_ref[m] != 0.0 for m in range(HG_LEVELS + 1)]
    parts = []
    for sub in order:
        rows = slice(sub * t, (sub + 1) * t)
        parts.append(_hgrn_chunk(q_ref[0, rows, :].astype(F32), lf_ref[0, rows, :],
                                 i_ref[0, rows, :].astype(F32), masks, reverse))

    states = [st_ref[hh] for hh in range(HG_HEADS)]
    for sub, (intra, qd, d_end, update) in zip(order, parts):
        rows = slice(sub * t, (sub + 1) * t)
        if readout:
            gate = g_ref[0, rows, :].astype(F32)
        for hh in range(HG_HEADS):
            cols = slice(hh * HG_DK, (hh + 1) * HG_DK)
            o_h = intra[hh] + nt_dot(qd[:, cols], states[hh].astype(BF16))
            states[hh] = states[hh] * d_end[:, cols] + update[hh]
            if readout:
                o_h = o_h + of_ref[0, rows, cols]
                ms = jnp.mean(o_h * o_h, axis=-1, keepdims=True)
                y = o_h * lax.rsqrt(ms + EPS) * onw_ref[...]
                o_ref[0, rows, cols] = (y * gate[:, cols]).astype(o_ref.dtype)
            else:
                o_ref[0, rows, cols] = o_h
    for hh in range(HG_HEADS):
        st_ref[hh] = states[hh]


def _hgrn_masks(reverse):
    t = HG_CHUNK
    rt = np.arange(t)[:, None]
    rs = np.arange(t)[None, :]
    qbit = 0 if reverse else 1
    masks = [rt == rs]
    for lvl in range(HG_LEVELS):
        masks.append(((rt >> (lvl + 1)) == (rs >> (lvl + 1))) & (((rt >> lvl) & 1) == qbit)
                     & (((rs >> lvl) & 1) != qbit))
    return jnp.asarray(np.stack(masks), F32)


def _hgrn_call(z, lf, ctx_len, reverse, o_fwd=None, onw=None):
    b, l, _ = z.shape
    t = HG_CHUNK
    rows = HG_CHUNKS_PER_STEP * t
    n_blocks = l // rows
    n_ctx = ctx_len // rows
    readout = o_fwd is not None

    if reverse:
        def block(i):
            return jnp.where(i < n_ctx, n_ctx - 1 - i, n_blocks - 1 + n_ctx - i)
    else:
        def block(i):
            return i

    def zspec(col):
        return pl.BlockSpec((1, rows, UNIT), lambda bi, i: (bi, block(i), col))

    row_spec = pl.BlockSpec((1, rows, HG_W), lambda bi, i: (bi, block(i), 0))
    in_specs = [zspec(COL_HQ), zspec(1 if reverse else 0), zspec(COL_HI),
                pl.BlockSpec((HG_LEVELS + 1, t, t), lambda bi, i: (0, 0, 0))]
    args = [z, lf, z, _hgrn_masks(reverse)]
    if readout:
        in_specs += [row_spec, zspec(COL_HG), pl.BlockSpec((1, HG_DK), lambda bi, i: (0, 0))]
        args += [o_fwd, z, onw]
    return pl.pallas_call(
        functools.partial(_hgrn_kernel, reverse=reverse, readout=readout, n_sub=HG_CHUNKS_PER_STEP),
        grid=(b, n_blocks),
        in_specs=in_specs,
        out_specs=row_spec,
        out_shape=jax.ShapeDtypeStruct((b, l, HG_W), BF16 if readout else F32),
        scratch_shapes=[pltpu.VMEM((HG_HEADS, HG_DK, HG_DK), F32)],
        compiler_params=_params("arbitrary", "arbitrary"),
    )(*args)


NA_ROWS_PER_STEP = 4
NA_QUERIES = NA_ROWS_PER_STEP * GRID_W
NA_BAND_ROWS = NA_WIN_R + NA_ROWS_PER_STEP
NA_BAND = NA_BAND_ROWS * GRID_W
NA_PATTERNS = 3


def _band_start(group_row, rows):
    return jnp.clip(group_row - NA_WIN_R // 2, 0, rows - NA_BAND_ROWS)


def _natten_kernel(q_ref, k_ref, v_ref, bm_ref, o_ref, *, ctx_len, rows):
    i = pl.program_id(1)
    n_ctx_blocks = ctx_len // NA_QUERIES
    pair_w = 2 * NA_HD
    lane = lax.broadcasted_iota(jnp.int32, (1, pair_w), 1)
    low = lane < NA_HD

    def nt_dot(a, b):
        return lax.dot_general(a, b, (((1,), (1,)), ((), ())), preferred_element_type=F32)

    def attend(local):
        if local:
            r = (i - n_ctx_blocks) * NA_ROWS_PER_STEP
            start = pl.multiple_of(ctx_len + _band_start(r, rows) * GRID_W, GRID_W)

        def pair_cols(ref, row_slice, p):
            return ref[0, row_slice, p * pair_w:(p + 1) * pair_w]

        def gather(ref, p):
            ctx_part = pair_cols(ref, slice(0, ctx_len), p)
            if not local:
                return ctx_part
            return jnp.concatenate([pair_cols(ref, pl.ds(start, NA_BAND), p), ctx_part], axis=0)

        def scores(h):
            p, hh = divmod(h, 2)
            q2 = pair_cols(q_ref, slice(None), p)
            qm = jnp.where(low if hh == 0 else ~low, q2, jnp.zeros_like(q2))
            s = nt_dot(qm, gather(k_ref, p))
            return s + bm_ref[0, h] if local else s

        s_next = scores(0)
        halves = []
        for h in range(NA_HEADS):
            s = s_next
            if h + 1 < NA_HEADS:
                s_next = scores(h + 1)
            e = jnp.exp2(s - jnp.max(s, axis=-1, keepdims=True))
            o = jnp.dot(e.astype(BF16), gather(v_ref, h // 2), preferred_element_type=F32)
            halves.append(o / jnp.sum(e, axis=-1, keepdims=True))
            if h % 2 == 1:
                p = h // 2
                o_ref[0, :, p * pair_w:(p + 1) * pair_w] = jnp.where(low, *halves).astype(o_ref.dtype)
                halves = []

    @pl.when(i < n_ctx_blocks)
    def _():
        attend(False)

    @pl.when(i >= n_ctx_blocks)
    def _():
        attend(True)


def _natten_call(z, bias, ctx_len):
    b, l, _ = z.shape
    rows = (l - ctx_len) // GRID_W
    n_ctx_blocks = ctx_len // NA_QUERIES

    def pattern(i):
        r = jnp.maximum(i - n_ctx_blocks, 0) * NA_ROWS_PER_STEP
        return (r - _band_start(r, rows)) // NA_ROWS_PER_STEP

    def full(col):
        return pl.BlockSpec((1, l, NA_W), lambda bi, i: (bi, 0, col), pipeline_mode=pl.Buffered(1))

    return pl.pallas_call(
        functools.partial(_natten_kernel, ctx_len=ctx_len, rows=rows),
        grid=(b, l // NA_QUERIES),
        in_specs=[pl.BlockSpec((1, NA_QUERIES, NA_W), lambda bi, i: (bi, i, COL_NQ)),
                  full(COL_NK), full(COL_NV),
                  pl.BlockSpec((1, NA_HEADS, NA_QUERIES, NA_BAND + ctx_len),
                               lambda bi, i: (pattern(i), 0, 0, 0))],
        out_specs=pl.BlockSpec((1, NA_QUERIES, NA_W), lambda bi, i: (bi, i, 0)),
        out_shape=jax.ShapeDtypeStruct((b, l, NA_W), BF16),
        compiler_params=_params("arbitrary", "arbitrary"),
    )(z, z, z, bias)


def _bias_kernel(cols_ref, o_ref):
    g = pl.program_id(0)
    for a in range(NA_ROWS_PER_STEP):
        r_rel = NA_ROWS_PER_STEP * g + a
        r0_rel = jnp.clip(r_rel - NA_WIN_R // 2, 0, NA_BAND_ROWS - NA_WIN_R)
        for j in range(NA_BAND_ROWS):
            in_window = (j >= r0_rel) & (j < r0_rel + NA_WIN_R)
            plane = cols_ref[0, jnp.clip(j - r_rel + NA_WIN_R - 1, 0, 2 * NA_WIN_R - 2)]
            o_ref[0, 0, a * GRID_W:(a + 1) * GRID_W, j * GRID_W:(j + 1) * GRID_W] = jnp.where(
                in_window, plane, NEG_BIG)
    o_ref[0, 0, :, NA_BAND:] = jnp.zeros((NA_QUERIES, o_ref.shape[3] - NA_BAND), F32)


def _bias_table(rpb, ctx_len):
    qc = np.arange(GRID_W)[:, None]
    kc = np.arange(GRID_W)[None, :]
    c0 = np.clip(qc - NA_WIN_C // 2, 0, GRID_W - NA_WIN_C)
    col_ok = (kc >= c0) & (kc < c0 + NA_WIN_C)
    n_ci = 2 * NA_WIN_C - 1
    n_ri = 2 * NA_WIN_R - 1
    onehot = (kc - qc + NA_WIN_C - 1)[:, :, None] == np.arange(n_ci)
    cols = jnp.einsum('hrc,qkc->hrqk', rpb.astype(F32), jnp.asarray(onehot, F32),
                      precision=lax.Precision.HIGHEST)
    cols = jnp.where(col_ok[None, None], cols * LOG2E, NEG_BIG)
    n_keys = NA_BAND + ctx_len
    return pl.pallas_call(
        _bias_kernel,
        grid=(NA_PATTERNS, NA_HEADS),
        in_specs=[pl.BlockSpec((1, n_ri, GRID_W, GRID_W), lambda g, h: (h, 0, 0, 0))],
        out_specs=pl.BlockSpec((1, 1, NA_QUERIES, n_keys), lambda g, h: (g, h, 0, 0)),
        out_shape=jax.ShapeDtypeStruct((NA_PATTERNS, NA_HEADS, NA_QUERIES, n_keys), F32),
        compiler_params=_params("arbitrary", "arbitrary"),
    )(cols)


def _rope_tables(ctx_len, s):
    tpos = np.arange(s)
    pos = np.stack([tpos // GRID_W, tpos % GRID_W], axis=-1).astype(np.float32)
    half = NA_HD // 2
    inv = (ROPE_THETA ** (-jnp.arange(0, half, 2, dtype=F32) / half))
    ang = jnp.asarray(pos)[:, :, None] * inv
    cos = jnp.cos(ang)
    sin = jnp.sin(ang)
    cos_h = jnp.concatenate([cos, cos], axis=-1).reshape(s, NA_HD)
    sin_h = jnp.concatenate([-sin, sin], axis=-1).reshape(s, NA_HD)
    cos_t = jnp.tile(cos_h, (1, NA_HEADS))
    sin_t = jnp.tile(sin_h, (1, NA_HEADS))
    cos_t = jnp.concatenate([jnp.ones((ctx_len, NA_W), F32), cos_t], axis=0)
    sin_t = jnp.concatenate([jnp.zeros((ctx_len, NA_W), F32), sin_t], axis=0)
    return cos_t, sin_t


def _merge_kernel(x_ref, mod_ref, ya_ref, yb_ref, cb_ref, u_ref, up_ref, un_ref, ga_ref, gb_ref, gc_ref,
                  cw_ref, wa_ref, wb_ref, wc_ref, wo_ref, o_ref, *, ctx_len, seq_len, tm, d):
    ti = pl.program_id(1)
    row = ti * tm + lax.broadcasted_iota(jnp.int32, (tm, 1), 0)
    local = lax.broadcasted_iota(jnp.int32, (tm, 1), 0)

    u = u_ref[0].astype(F32)
    u_before = up_ref[0, BF16_ROWS - 1:BF16_ROWS, :].astype(F32)
    u_after = un_ref[0, 0:1, :].astype(F32)
    up = jnp.where(local == 0, u_before, pltpu.roll(u, 1, 0))
    un = jnp.where(local == tm - 1, u_after, pltpu.roll(u, tm - 1, 0))
    has_prev = (row != 0) & (row != ctx_len)
    has_next = (row != ctx_len - 1) & (row != seq_len - 1)
    cw = cw_ref[...]
    conv = (jnp.where(has_prev, up, 0.0) * cw[0:1, :] + u * cw[1:2, :]
            + jnp.where(has_next, un, 0.0) * cw[2:3, :])
    yc = (cb_ref[0].astype(F32) * conv).astype(BF16)

    mix = ga_ref[0].astype(F32) * jnp.dot(ya_ref[0], wa_ref[...], preferred_element_type=F32)
    mix = mix + gb_ref[0].astype(F32) * jnp.dot(yb_ref[0], wb_ref[...], preferred_element_type=F32)
    mix = mix + gc_ref[0].astype(F32) * jnp.dot(yc, wc_ref[...], preferred_element_type=F32)
    out = jnp.dot(mix.astype(BF16), wo_ref[...], preferred_element_type=F32)
    o_ref[0] = _gated_residual(x_ref[0], out, mod_ref, 2, ctx_len)


def _merge_call(xs, mod, ya, yb, z, cw, wa, wb, wc, wo, ctx_len):
    b, l, d = xs.shape
    tm = _token_tile(l)
    halo = tm // BF16_ROWS
    n_halo = l // BF16_ROWS

    def zspec(col, width=UNIT):
        return pl.BlockSpec((1, tm, width), lambda bi, ti: (bi, ti, col * UNIT // width))

    def prev(col):
        return pl.BlockSpec((1, BF16_ROWS, UNIT), lambda bi, ti: (bi, jnp.maximum(ti * halo - 1, 0), col))

    def nxt(col):
        return pl.BlockSpec((1, BF16_ROWS, UNIT),
                            lambda bi, ti: (bi, jnp.minimum((ti + 1) * halo, n_halo - 1), col))

    def const(shape):
        return pl.BlockSpec(shape, lambda bi, ti: tuple(0 for _ in shape))

    tok = pl.BlockSpec((1, tm, d), lambda bi, ti: (bi, ti, 0))
    half = pl.BlockSpec((1, tm, UNIT), lambda bi, ti: (bi, ti, 0))
    return pl.pallas_call(
        functools.partial(_merge_kernel, ctx_len=ctx_len, seq_len=l, tm=tm, d=d),
        grid=(b, l // tm),
        in_specs=[tok, pl.BlockSpec((1, 2, N_MOD * d), lambda bi, ti: (bi, 0, 0)), half, half,
                  zspec(COL_CB), zspec(COL_CU), prev(COL_CU), nxt(COL_CU),
                  zspec(COL_GA, d), zspec(COL_GB, d), zspec(COL_GC, d),
                  const(cw.shape), const(wa.shape), const(wb.shape), const(wc.shape), const(wo.shape)],
        out_specs=tok,
        out_shape=jax.ShapeDtypeStruct((b, l, d), F32),
        compiler_params=_params("arbitrary", "arbitrary"),
    )(xs, mod, ya, yb, z, z, z, z, z, z, z, cw, wa, wb, wc, wo)


def _mlp_kernel(x_ref, mod_ref, nw_ref, w1_ref, w2_ref, o_ref, *, ctx_len, tm, d, ff_chunk):
    x = x_ref[0]
    h = _modulated_norm(x, nw_ref[...], mod_ref, 3, 4, ctx_len).astype(BF16)
    acc = jnp.zeros((tm, d), F32)
    for c in range(w1_ref.shape[1] // ff_chunk):
        a = jnp.dot(h, w1_ref[:, c * ff_chunk:(c + 1) * ff_chunk], preferred_element_type=F32)
        a = jnp.maximum(a, 0.0)
        acc = acc + jnp.dot((a * a).astype(BF16), w2_ref[c * ff_chunk:(c + 1) * ff_chunk, :],
                            preferred_element_type=F32)
    o_ref[0] = _gated_residual(x, acc, mod_ref, 5, ctx_len)


def _mlp_call(xs, mod, nw, w1, w2, ctx_len):
    b, l, d = xs.shape
    tm = _token_tile(l)
    tok = pl.BlockSpec((1, tm, d), lambda bi, ti: (bi, ti, 0))
    return pl.pallas_call(
        functools.partial(_mlp_kernel, ctx_len=ctx_len, tm=tm, d=d, ff_chunk=1024),
        grid=(b, l // tm),
        in_specs=[tok, pl.BlockSpec((1, 2, N_MOD * d), lambda bi, ti: (bi, 0, 0)),
                  pl.BlockSpec((1, d), lambda bi, ti: (0, 0)),
                  pl.BlockSpec(w1.shape, lambda bi, ti: (0, 0), pipeline_mode=pl.Buffered(1)),
                  pl.BlockSpec(w2.shape, lambda bi, ti: (0, 0), pipeline_mode=pl.Buffered(1))],
        out_specs=tok,
        out_shape=jax.ShapeDtypeStruct((b, l, d), F32),
        compiler_params=_params("arbitrary", "arbitrary"),
    )(xs, mod, nw, w1, w2)


def _reorder_proj(w_in):
    return jnp.concatenate([w_in[..., u * UNIT:(u + 1) * UNIT] for u in W_UNIT_ORDER], axis=-1)


def kernel(x, c, ctx, c_ctx, ada_w, ada_b, norm1_w, norm2_w, w_in, hgrn_lb_logits, hgrn_onorm_w,
           q_norm_w, k_norm_w, natten_rpb, conv_w, w_branch_a, w_branch_b, w_branch_c, w_out,
           mlp_w1, mlp_w2):
    b, s, d = x.shape
    ctx_len = ctx.shape[1]
    depth = ada_w.shape[0]
    assert s % NA_QUERIES == 0 and s // GRID_W >= NA_BAND_ROWS and ctx_len % NA_QUERIES == 0
    assert ctx_len % (HG_CHUNKS_PER_STEP * HG_CHUNK) == 0 and s % (HG_CHUNKS_PER_STEP * HG_CHUNK) == 0
    assert w_in.shape[-1] == PROJ_UNITS * UNIT
    assert ctx_len <= _token_tile(ctx_len + s)

    lb_p = jax.nn.softmax(hgrn_lb_logits.astype(F32), axis=1)
    lower_bounds = jnp.cumsum(lb_p, axis=1) - lb_p[:, :1]

    pad_rows = -(b + 1) % SUBLANES
    cvec = jnp.concatenate([c_ctx[None, :], c, jnp.zeros((pad_rows, d), F32)], axis=0)
    mod_all = _ada_call(cvec, ada_w, ada_b)
    mod_sel = jnp.stack([jnp.broadcast_to(mod_all[:, 0:1], (depth, b, N_MOD * d)),
                         mod_all[:, 1:b + 1]], axis=2)

    cos_t, sin_t = _rope_tables(ctx_len, s)
    bd = jnp.asarray(np.kron(np.eye(NA_HEADS), np.ones((NA_HD, NA_HD))), BF16)
    w_in_b = _reorder_proj(w_in).astype(BF16)

    xs = jnp.concatenate([ctx, x], axis=1)
    for l in range(depth):
        mod = mod_sel[l]
        z, lf = _inproj_call(xs, mod, norm1_w[l][None, :], w_in_b[l], cos_t, sin_t,
                             jnp.tile(q_norm_w[l], NA_HEADS)[None, :],
                             jnp.tile(k_norm_w[l], NA_HEADS)[None, :], bd, lower_bounds[:, l].reshape(1, 2 * HG_W), ctx_len)
        o_f = _hgrn_call(z, lf, ctx_len, False)
        ya = _hgrn_call(z, lf, ctx_len, True, o_f, hgrn_onorm_w[l][None, :])
        yb = _natten_call(z, _bias_table(natten_rpb[l], ctx_len), ctx_len)
        xs = _merge_call(xs, mod, ya, yb, z, conv_w[l], w_branch_a[l].astype(BF16),
                         w_branch_b[l].astype(BF16), w_branch_c[l].astype(BF16),
                         w_out[l].astype(BF16), ctx_len)
        xs = _mlp_call(xs, mod, norm2_w[l][None, :], mlp_w1[l].astype(BF16),
                       mlp_w2[l].astype(BF16), ctx_len)
    return xs[:, ctx_len:, :]
```

```python
import functools

import numpy as np
import jax
import jax.numpy as jnp
from jax import lax
from jax.experimental import pallas as pl
from jax.experimental.pallas import tpu as pltpu

GRID_W = 64
HG_HEADS = 4
HG_DK = 128
HG_W = HG_HEADS * HG_DK
HG_CHUNK = 64
HG_LEVELS = 6
HG_CHUNKS_PER_STEP = 4
NA_HEADS = 8
NA_HD = 64
NA_W = NA_HEADS * NA_HD
NA_WIN_R = 8
NA_WIN_C = 16
ROPE_THETA = 10000.0
SC_W = 512
N_MOD = 6
EPS = 1e-6
NEG_BIG = -1e30
LOG2E = 1.4426950408889634

LANES = 128
SUBLANES = 8
BF16_ROWS = 16
VMEM_LIMIT = 56 * 1024 * 1024

UNIT = 512
PROJ_UNITS = 17
W_UNIT_ORDER = (11, 12, 13, 14, 15, 16, 0, 4, 5, 6, 1, 2, 9, 10, 3, 7, 8)
WU_GATES, WU_SILU, WU_QK, WU_FORGET, WU_CONV, WU_RAW, WU_CB = 0, 6, 8, 10, 12, 14, 16
COL_GA, COL_GB, COL_GC = 0, 2, 4
COL_HQ, COL_HG = 6, 7
COL_NQ, COL_NK = 8, 9
COL_HI, COL_NV = 10, 11
COL_CB, COL_CU = 12, 13
Z_UNITS = 14

F32 = jnp.float32
BF16 = jnp.bfloat16


def _pick_tile(n, target, mult):
    best = None
    for t in range(mult, min(n, target) + 1, mult):
        if n % t == 0:
            best = t
    if best is None:
        raise ValueError(f"no tile for {n} (target {target}, multiple {mult})")
    return best


def _token_tile(l):
    return _pick_tile(l, 544, BF16_ROWS)


def _sigmoid(x):
    return 1.0 / (1.0 + jnp.exp(-x))


def _gate_sigmoid(x):
    return 0.5 * jnp.tanh(0.5 * x) + 0.5


def _params(*sem):
    return pltpu.CompilerParams(dimension_semantics=sem, vmem_limit_bytes=VMEM_LIMIT)


def _ada_kernel(c_ref, w_ref, b_ref, o_ref):
    c = c_ref[...]
    s = c * _sigmoid(c)
    o_ref[0] = jnp.dot(s, w_ref[0], precision=lax.Precision.HIGHEST,
                       preferred_element_type=F32) + b_ref[0]


def _ada_call(cvec, ada_w, ada_b):
    depth, d, n = ada_w.shape
    rows = cvec.shape[0]
    tn = _pick_tile(n, 1024, LANES)
    return pl.pallas_call(
        _ada_kernel,
        grid=(depth, n // tn),
        in_specs=[
            pl.BlockSpec((rows, d), lambda l, j: (0, 0)),
            pl.BlockSpec((1, d, tn), lambda l, j: (l, 0, j)),
            pl.BlockSpec((1, 1, tn), lambda l, j: (l, 0, j)),
        ],
        out_specs=pl.BlockSpec((1, rows, tn), lambda l, j: (l, 0, j)),
        out_shape=jax.ShapeDtypeStruct((depth, rows, n), F32),
        compiler_params=_params("arbitrary", "arbitrary"),
    )(cvec, ada_w, ada_b.reshape(depth, 1, n))


def _segment_mod(mod_ref, idx, d):
    mc = mod_ref[0, 0:1, idx * d:(idx + 1) * d]
    ml = mod_ref[0, 1:2, idx * d:(idx + 1) * d]
    return jnp.where(pl.program_id(1) == 0, mc, ml), ml


def _by_segment(x, ctx_len, head_fn, rest_fn):
    if ctx_len == x.shape[0]:
        return head_fn(x)
    return jnp.concatenate([head_fn(x[:ctx_len]), rest_fn(x[ctx_len:])], axis=0)


def _modulated_norm(x, nw, mod_ref, i_shift, i_scale, ctx_len):
    d = x.shape[-1]
    ms = jnp.mean(x * x, axis=-1, keepdims=True)
    xn = x * lax.rsqrt(ms + EPS)
    shifts = _segment_mod(mod_ref, i_shift, d)
    gains = [nw * (1.0 + s) for s in _segment_mod(mod_ref, i_scale, d)]
    return _by_segment(xn, ctx_len, lambda a: a * gains[0] + shifts[0], lambda a: a * gains[1] + shifts[1])


def _gated_residual(x, y, mod_ref, i_gate, ctx_len):
    gates = _segment_mod(mod_ref, i_gate, x.shape[-1])
    return x + _by_segment(y, ctx_len, lambda a: a * gates[0], lambda a: a * gates[1])


def _rotate(x, cos, sin_signed, first_half):
    w = x.shape[-1]
    quarter = NA_HD // 4
    partner = jnp.where(first_half, pltpu.roll(x, w - quarter, 1), pltpu.roll(x, quarter, 1))
    return x * cos + partner * sin_signed


def _inproj_kernel(x_ref, mod_ref, nw_ref, w_ref, cos_ref, sin_ref, qw_ref, kw_ref, bd_ref, lb_ref, hm_ref,
                   z_ref, lf_ref, of_ref, st_ref, *, ctx_len):
    @pl.when(pl.program_id(1) == 0)
    def _():
        st_ref[...] = jnp.zeros_like(st_ref)

    xn = _modulated_norm(x_ref[0], nw_ref[...], mod_ref, 0, 1, ctx_len).astype(BF16)
    lane = lax.broadcasted_iota(jnp.int32, (1, NA_W), 1)
    first_half = (lane % (NA_HD // 2)) < (NA_HD // 4)

    def project(u0, n_units):
        return jnp.dot(xn, w_ref[:, u0 * UNIT:(u0 + n_units) * UNIT], preferred_element_type=F32)

    def put(col, val):
        z_ref[0, :, col * UNIT:col * UNIT + val.shape[1]] = val.astype(z_ref.dtype)

    def head_prep(y, sq, w):
        ms = jnp.dot(sq, bd_ref[...], preferred_element_type=F32) * (1.0 / NA_HD)
        return _rotate(y * lax.rsqrt(ms + EPS) * w, cos_ref[...], sin_ref[...], first_half)

    r_silu = project(WU_SILU, 2)
    r_forget = project(WU_FORGET, 2)
    r_raw = project(WU_RAW, 2)
    silu = r_silu * _gate_sigmoid(r_silu)
    put(COL_HQ, silu)
    lb = lb_ref[...]
    lf = jnp.log(lb + (1.0 - lb) * _sigmoid(r_forget))
    lf_ref[0] = lf
    put(COL_HI, r_raw)
    r_qk = project(WU_QK, 2)

    t = HG_CHUNK
    masks = [hm_ref[m] != 0.0 for m in range(HG_LEVELS + 1)]

    def chunk(c):
        rows = slice(c * t, (c + 1) * t)
        return _hgrn_chunk(silu[rows, :HG_W], lf[rows, :HG_W], r_raw[rows, :HG_W], masks, False)

    parts = [chunk(0)]
    r_gate = project(WU_GATES, 2)
    sq = (r_qk * r_qk).astype(BF16)
    parts.append(chunk(1))
    q_ready = head_prep(r_qk[:, :UNIT], sq[:, :UNIT], qw_ref[...]) * (NA_HD ** -0.5 * LOG2E)
    k_ready = head_prep(r_qk[:, UNIT:], sq[:, UNIT:], kw_ref[...])
    for n in range(2):
        r_next = project(WU_GATES + 2 * (n + 1), 2)
        put(COL_GA + 2 * n, _gate_sigmoid(r_gate))
        r_gate = r_next
        parts.append(chunk(2 + n))
    put(COL_NQ, q_ready)
    put(COL_NK, k_ready)
    r_conv = project(WU_CONV, 2)
    put(COL_GC, _gate_sigmoid(r_gate))

    def emit(rows, cols, o_f):
        of_ref[0, rows, cols] = o_f

    _hgrn_scan(parts, list(range(len(parts))), st_ref, emit)
    r_cb = project(WU_CB, 1)
    put(COL_CU, r_conv[:, :UNIT] * r_conv[:, UNIT:])
    put(COL_CB, r_cb)


def _inproj_call(xs, mod, nw, w, cos, sin, qw, kw, bd, lb, ctx_len):
    b, l, d = xs.shape
    tm = HG_CHUNKS_PER_STEP * HG_CHUNK

    def const(shape):
        return pl.BlockSpec(shape, lambda bi, ti: tuple(0 for _ in shape))

    def rows(width):
        return pl.BlockSpec((1, tm, width), lambda bi, ti: (bi, ti, 0))

    tab = pl.BlockSpec((tm, NA_W), lambda bi, ti: (ti, 0))
    return pl.pallas_call(
        functools.partial(_inproj_kernel, ctx_len=ctx_len),
        grid=(b, l // tm),
        in_specs=[
            rows(d),
            pl.BlockSpec((1, 2, N_MOD * d), lambda bi, ti: (bi, 0, 0)),
            const((1, d)),
            pl.BlockSpec(w.shape, lambda bi, ti: (0, 0), pipeline_mode=pl.Buffered(1)),
            tab, tab, const((1, NA_W)), const((1, NA_W)), const((NA_W, NA_W)), const((1, 2 * HG_W)),
            const((HG_LEVELS + 1, HG_CHUNK, HG_CHUNK)),
        ],
        out_specs=[rows(Z_UNITS * UNIT), rows(2 * HG_W), rows(HG_W)],
        out_shape=[jax.ShapeDtypeStruct((b, l, Z_UNITS * UNIT), BF16),
                   jax.ShapeDtypeStruct((b, l, 2 * HG_W), F32),
                   jax.ShapeDtypeStruct((b, l, HG_W), F32)],
        scratch_shapes=[pltpu.VMEM((HG_HEADS, HG_DK, HG_DK), F32)],
        compiler_params=_params("arbitrary", "arbitrary"),
    )(xs, mod, nw, w, cos, sin, qw, kw, bd, lb, _hgrn_masks(False))


def _boundary_rows(gc, h, reverse):
    t, w = gc.shape
    off = h if reverse else h - 1
    if 2 * h >= SUBLANES:
        g3 = gc.reshape(t // (2 * h), 2 * h, w)
        return jnp.broadcast_to(g3[:, off:off + 1, :], g3.shape).reshape(t, w)
    g3 = gc.reshape(t // SUBLANES, SUBLANES, w)
    sub = lax.broadcasted_iota(jnp.int32, (1, SUBLANES, 1), 1)
    out = None
    for start in range(0, SUBLANES, 2 * h):
        piece = jnp.broadcast_to(g3[:, start + off:start + off + 1, :], g3.shape)
        out = piece if out is None else jnp.where(sub >= start, piece, out)
    return out.reshape(t, w)


def _scan_cumsum(lf, row, reverse):
    t = lf.shape[0]
    within = row & (SUBLANES - 1)
    g = lf
    for dist in (1, 2, 4):
        if reverse:
            g = g + jnp.where(within < SUBLANES - dist, pltpu.roll(g, t - dist, 0), 0.0)
        else:
            g = g + jnp.where(within >= dist, pltpu.roll(g, dist, 0), 0.0)
    groups = list(range(t // SUBLANES))
    pieces = {}
    carry = None
    for gi in (reversed(groups) if reverse else groups):
        piece = g[gi * SUBLANES:(gi + 1) * SUBLANES]
        if carry is not None:
            piece = piece + carry
        pieces[gi] = piece
        carry = piece[0:1] if reverse else piece[SUBLANES - 1:SUBLANES]
    return jnp.concatenate([pieces[gi] for gi in groups], axis=0)


def _level_operand(qs, kk, gc, h, reverse):
    pieces = []
    for s0 in range(0, gc.shape[0], 2 * h):
        first, second = slice(s0, s0 + h), slice(s0 + h, s0 + 2 * h)
        if reverse:
            gb = gc[s0 + h:s0 + h + 1]
            pieces += [qs[first] * jnp.exp(gc[first] - gb), kk[second] * jnp.exp(gb - gc[second])]
        else:
            gb = gc[s0 + h - 1:s0 + h]
            pieces += [kk[first] * jnp.exp(gb - gc[first]), qs[second] * jnp.exp(gc[second] - gb)]
    return jnp.concatenate(pieces, axis=0).astype(BF16)


def _hgrn_chunk(qs, lf, v, masks, reverse):
    t = HG_CHUNK
    forget = jnp.exp(lf)
    kk = 1.0 - forget
    row = lax.broadcasted_iota(jnp.int32, (t, 1), 0)
    gc = _scan_cumsum(lf, row, reverse)

    def nt_dot(a, b):
        return lax.dot_general(a, b, (((1,), (1,)), ((), ())), preferred_element_type=F32)

    def head(a, hh):
        return a[:, hh * HG_DK:(hh + 1) * HG_DK]

    qs_b = qs.astype(BF16)
    kk_b = kk.astype(BF16)
    attn = [jnp.where(masks[0], nt_dot(head(qs_b, hh), head(kk_b, hh)), 0.0) for hh in range(HG_HEADS)]
    qbit = 0 if reverse else 1
    for lvl in range(HG_LEVELS):
        is_q = ((row >> lvl) & 1) == qbit
        if lvl == 0:
            both = jnp.where(is_q, qs * forget, kk).astype(BF16)
        elif (1 << lvl) >= SUBLANES:
            both = _level_operand(qs, kk, gc, 1 << lvl, reverse)
        else:
            e = jnp.exp(-jnp.abs(gc - _boundary_rows(gc, 1 << lvl, reverse)))
            both = (jnp.where(is_q, qs, kk) * e).astype(BF16)
        for hh in range(HG_HEADS):
            attn[hh] = jnp.where(masks[lvl + 1], nt_dot(head(both, hh), head(both, hh)), attn[hh])

    g_end = gc[0:1, :] if reverse else gc[t - 1:t, :]
    qd = (qs * jnp.exp(gc)).astype(BF16)
    kd = (kk * jnp.exp(g_end - gc)).astype(BF16)
    d_end = jnp.exp(g_end)
    v_b = v.astype(BF16)

    intra = []
    update = []
    for hh in range(HG_HEADS):
        intra.append(jnp.dot(attn[hh].astype(BF16), head(v_b, hh), preferred_element_type=F32))
        vt = head(v, hh).T.astype(BF16)
        update.append(jnp.dot(vt, head(kd, hh), preferred_element_type=F32))
    return intra, qd, d_end, update


def _hgrn_scan(parts, subs, st_ref, emit):
    t = HG_CHUNK
    states = [st_ref[hh] for hh in range(HG_HEADS)]
    for sub, (intra, qd, d_end, update) in zip(subs, parts):
        rows = slice(sub * t, (sub + 1) * t)
        for hh in range(HG_HEADS):
            cols = slice(hh * HG_DK, (hh + 1) * HG_DK)
            o_h = intra[hh] + lax.dot_general(qd[:, cols], states[hh].astype(BF16), (((1,), (1,)), ((), ())),
                                              preferred_element_type=F32)
            states[hh] = states[hh] * d_end[:, cols] + update[hh]
            emit(rows, cols, o_h)
    for hh in range(HG_HEADS):
        st_ref[hh] = states[hh]


def _hgrn_bwd_kernel(q_ref, lf_ref, i_ref, hm_ref, of_ref, g_ref, onw_ref, o_ref, st_ref, *, n_sub):
    t = HG_CHUNK

    @pl.when(pl.program_id(1) == 0)
    def _():
        st_ref[...] = jnp.zeros_like(st_ref)

    subs = list(range(n_sub - 1, -1, -1))
    masks = [hm_ref[m] != 0.0 for m in range(HG_LEVELS + 1)]
    parts = []
    for sub in subs:
        rows = slice(sub * t, (sub + 1) * t)
        parts.append(_hgrn_chunk(q_ref[0, rows, :].astype(F32), lf_ref[0, rows, :],
                                 i_ref[0, rows, :].astype(F32), masks, True))

    def emit(rows, cols, o_b):
        o_h = o_b + of_ref[0, rows, cols]
        ms = jnp.mean(o_h * o_h, axis=-1, keepdims=True)
        y = o_h * lax.rsqrt(ms + EPS) * onw_ref[...]
        o_ref[0, rows, cols] = (y * g_ref[0, rows, cols].astype(F32)).astype(o_ref.dtype)

    _hgrn_scan(parts, subs, st_ref, emit)


def _hgrn_masks(reverse):
    t = HG_CHUNK
    rt = np.arange(t)[:, None]
    rs = np.arange(t)[None, :]
    qbit = 0 if reverse else 1
    masks = [rt == rs]
    for lvl in range(HG_LEVELS):
        masks.append(((rt >> (lvl + 1)) == (rs >> (lvl + 1))) & (((rt >> lvl) & 1) == qbit)
                     & (((rs >> lvl) & 1) != qbit))
    return jnp.asarray(np.stack(masks), F32)


def _hgrn_bwd_call(z, lf, o_fwd, onw, ctx_len):
    b, l, _ = z.shape
    t = HG_CHUNK
    rows = HG_CHUNKS_PER_STEP * t
    n_blocks = l // rows
    n_ctx = ctx_len // rows

    def block(i):
        return jnp.where(i < n_ctx, n_ctx - 1 - i, n_blocks - 1 + n_ctx - i)

    def zspec(col):
        return pl.BlockSpec((1, rows, UNIT), lambda bi, i: (bi, block(i), col))

    row_spec = pl.BlockSpec((1, rows, HG_W), lambda bi, i: (bi, block(i), 0))
    return pl.pallas_call(
        functools.partial(_hgrn_bwd_kernel, n_sub=HG_CHUNKS_PER_STEP),
        grid=(b, n_blocks),
        in_specs=[zspec(COL_HQ), zspec(1), zspec(COL_HI),
                  pl.BlockSpec((HG_LEVELS + 1, t, t), lambda bi, i: (0, 0, 0)),
                  row_spec, zspec(COL_HG), pl.BlockSpec((1, HG_DK), lambda bi, i: (0, 0))],
        out_specs=row_spec,
        out_shape=jax.ShapeDtypeStruct((b, l, HG_W), BF16),
        scratch_shapes=[pltpu.VMEM((HG_HEADS, HG_DK, HG_DK), F32)],
        compiler_params=_params("arbitrary", "arbitrary"),
    )(z, lf, z, _hgrn_masks(True), o_fwd, z, onw)


NA_ROWS_PER_STEP = 4
NA_QUERIES = NA_ROWS_PER_STEP * GRID_W
NA_BAND_ROWS = NA_WIN_R + NA_ROWS_PER_STEP
NA_BAND = NA_BAND_ROWS * GRID_W
NA_PATTERNS = 3


def _band_start(group_row, rows):
    return jnp.clip(group_row - NA_WIN_R // 2, 0, rows - NA_BAND_ROWS)


def _natten_kernel(q_ref, k_ref, v_ref, bm_ref, o_ref, *, ctx_len, rows):
    i = pl.program_id(1)
    n_ctx_blocks = ctx_len // NA_QUERIES
    pair_w = 2 * NA_HD
    lane = lax.broadcasted_iota(jnp.int32, (1, pair_w), 1)
    low = lane < NA_HD

    def nt_dot(a, b):
        return lax.dot_general(a, b, (((1,), (1,)), ((), ())), preferred_element_type=F32)

    def attend(local):
        if local:
            r = (i - n_ctx_blocks) * NA_ROWS_PER_STEP
            start = pl.multiple_of(ctx_len + _band_start(r, rows) * GRID_W, GRID_W)

        def pair_cols(ref, row_slice, p):
            return ref[0, row_slice, p * pair_w:(p + 1) * pair_w]

        def gather(ref, p):
            ctx_part = pair_cols(ref, slice(0, ctx_len), p)
            if not local:
                return ctx_part
            return jnp.concatenate([pair_cols(ref, pl.ds(start, NA_BAND), p), ctx_part], axis=0)

        def scores(h):
            p, hh = divmod(h, 2)
            q2 = pair_cols(q_ref, slice(None), p)
            qm = jnp.where(low if hh == 0 else ~low, q2, jnp.zeros_like(q2))
            s = nt_dot(qm, gather(k_ref, p))
            return s + bm_ref[0, h] if local else s

        s_next = scores(0)
        halves = []
        for h in range(NA_HEADS):
            s = s_next
            if h + 1 < NA_HEADS:
                s_next = scores(h + 1)
            e = jnp.exp2(s - jnp.max(s, axis=-1, keepdims=True))
            o = jnp.dot(e.astype(BF16), gather(v_ref, h // 2), preferred_element_type=F32)
            halves.append(o / jnp.sum(e, axis=-1, keepdims=True))
            if h % 2 == 1:
                p = h // 2
                o_ref[0, :, p * pair_w:(p + 1) * pair_w] = jnp.where(low, *halves).astype(o_ref.dtype)
                halves = []

    @pl.when(i < n_ctx_blocks)
    def _():
        attend(False)

    @pl.when(i >= n_ctx_blocks)
    def _():
        attend(True)


def _natten_call(z, bias, ctx_len):
    b, l, _ = z.shape
    rows = (l - ctx_len) // GRID_W
    n_ctx_blocks = ctx_len // NA_QUERIES

    def pattern(i):
        r = jnp.maximum(i - n_ctx_blocks, 0) * NA_ROWS_PER_STEP
        return (r - _band_start(r, rows)) // NA_ROWS_PER_STEP

    def full(col):
        return pl.BlockSpec((1, l, NA_W), lambda bi, i: (bi, 0, col), pipeline_mode=pl.Buffered(1))

    return pl.pallas_call(
        functools.partial(_natten_kernel, ctx_len=ctx_len, rows=rows),
        grid=(b, l // NA_QUERIES),
        in_specs=[pl.BlockSpec((1, NA_QUERIES, NA_W), lambda bi, i: (bi, i, COL_NQ)),
                  full(COL_NK), full(COL_NV),
                  pl.BlockSpec((1, NA_HEADS, NA_QUERIES, NA_BAND + ctx_len),
                               lambda bi, i: (pattern(i), 0, 0, 0))],
        out_specs=pl.BlockSpec((1, NA_QUERIES, NA_W), lambda bi, i: (bi, i, 0)),
        out_shape=jax.ShapeDtypeStruct((b, l, NA_W), BF16),
        compiler_params=_params("arbitrary", "arbitrary"),
    )(z, z, z, bias)


def _bias_kernel(cols_ref, o_ref):
    g = pl.program_id(0)
    for a in range(NA_ROWS_PER_STEP):
        r_rel = NA_ROWS_PER_STEP * g + a
        r0_rel = jnp.clip(r_rel - NA_WIN_R // 2, 0, NA_BAND_ROWS - NA_WIN_R)
        for j in range(NA_BAND_ROWS):
            in_window = (j >= r0_rel) & (j < r0_rel + NA_WIN_R)
            plane = cols_ref[0, jnp.clip(j - r_rel + NA_WIN_R - 1, 0, 2 * NA_WIN_R - 2)]
            o_ref[0, 0, a * GRID_W:(a + 1) * GRID_W, j * GRID_W:(j + 1) * GRID_W] = jnp.where(
                in_window, plane, NEG_BIG)
    o_ref[0, 0, :, NA_BAND:] = jnp.zeros((NA_QUERIES, o_ref.shape[3] - NA_BAND), F32)


def _bias_table(rpb, ctx_len):
    qc = np.arange(GRID_W)[:, None]
    kc = np.arange(GRID_W)[None, :]
    c0 = np.clip(qc - NA_WIN_C // 2, 0, GRID_W - NA_WIN_C)
    col_ok = (kc >= c0) & (kc < c0 + NA_WIN_C)
    n_ci = 2 * NA_WIN_C - 1
    n_ri = 2 * NA_WIN_R - 1
    onehot = (kc - qc + NA_WIN_C - 1)[:, :, None] == np.arange(n_ci)
    cols = jnp.einsum('hrc,qkc->hrqk', rpb.astype(F32), jnp.asarray(onehot, F32),
                      precision=lax.Precision.HIGHEST)
    cols = jnp.where(col_ok[None, None], cols * LOG2E, NEG_BIG)
    n_keys = NA_BAND + ctx_len
    return pl.pallas_call(
        _bias_kernel,
        grid=(NA_PATTERNS, NA_HEADS),
        in_specs=[pl.BlockSpec((1, n_ri, GRID_W, GRID_W), lambda g, h: (h, 0, 0, 0))],
        out_specs=pl.BlockSpec((1, 1, NA_QUERIES, n_keys), lambda g, h: (g, h, 0, 0)),
        out_shape=jax.ShapeDtypeStruct((NA_PATTERNS, NA_HEADS, NA_QUERIES, n_keys), F32),
        compiler_params=_params("arbitrary", "arbitrary"),
    )(cols)


def _rope_tables(ctx_len, s):
    tpos = np.arange(s)
    pos = np.stack([tpos // GRID_W, tpos % GRID_W], axis=-1).astype(np.float32)
    half = NA_HD // 2
    inv = (ROPE_THETA ** (-jnp.arange(0, half, 2, dtype=F32) / half))
    ang = jnp.asarray(pos)[:, :, None] * inv
    cos = jnp.cos(ang)
    sin = jnp.sin(ang)
    cos_h = jnp.concatenate([cos, cos], axis=-1).reshape(s, NA_HD)
    sin_h = jnp.concatenate([-sin, sin], axis=-1).reshape(s, NA_HD)
    cos_t = jnp.tile(cos_h, (1, NA_HEADS))
    sin_t = jnp.tile(sin_h, (1, NA_HEADS))
    cos_t = jnp.concatenate([jnp.ones((ctx_len, NA_W), F32), cos_t], axis=0)
    sin_t = jnp.concatenate([jnp.zeros((ctx_len, NA_W), F32), sin_t], axis=0)
    return cos_t, sin_t


def _merge_kernel(x_ref, mod_ref, ya_ref, yb_ref, cb_ref, u_ref, up_ref, un_ref, ga_ref, gb_ref, gc_ref,
                  cw_ref, wa_ref, wb_ref, wc_ref, wo_ref, o_ref, *, ctx_len, seq_len, tm, d):
    ti = pl.program_id(1)
    row = ti * tm + lax.broadcasted_iota(jnp.int32, (tm, 1), 0)
    local = lax.broadcasted_iota(jnp.int32, (tm, 1), 0)

    u = u_ref[0].astype(F32)
    u_before = up_ref[0, BF16_ROWS - 1:BF16_ROWS, :].astype(F32)
    u_after = un_ref[0, 0:1, :].astype(F32)
    up = jnp.where(local == 0, u_before, pltpu.roll(u, 1, 0))
    un = jnp.where(local == tm - 1, u_after, pltpu.roll(u, tm - 1, 0))
    has_prev = (row != 0) & (row != ctx_len)
    has_next = (row != ctx_len - 1) & (row != seq_len - 1)
    cw = cw_ref[...]
    conv = (jnp.where(has_prev, up, 0.0) * cw[0:1, :] + u * cw[1:2, :]
            + jnp.where(has_next, un, 0.0) * cw[2:3, :])
    yc = (cb_ref[0].astype(F32) * conv).astype(BF16)

    mix = ga_ref[0].astype(F32) * jnp.dot(ya_ref[0], wa_ref[...], preferred_element_type=F32)
    mix = mix + gb_ref[0].astype(F32) * jnp.dot(yb_ref[0], wb_ref[...], preferred_element_type=F32)
    mix = mix + gc_ref[0].astype(F32) * jnp.dot(yc, wc_ref[...], preferred_element_type=F32)
    out = jnp.dot(mix.astype(BF16), wo_ref[...], preferred_element_type=F32)
    o_ref[0] = _gated_residual(x_ref[0], out, mod_ref, 2, ctx_len)


def _merge_call(xs, mod, ya, yb, z, cw, wa, wb, wc, wo, ctx_len):
    b, l, d = xs.shape
    tm = _token_tile(l)
    halo = tm // BF16_ROWS
    n_halo = l // BF16_ROWS

    def zspec(col, width=UNIT):
        return pl.BlockSpec((1, tm, width), lambda bi, ti: (bi, ti, col * UNIT // width))

    def prev(col):
        return pl.BlockSpec((1, BF16_ROWS, UNIT), lambda bi, ti: (bi, jnp.maximum(ti * halo - 1, 0), col))

    def nxt(col):
        return pl.BlockSpec((1, BF16_ROWS, UNIT),
                            lambda bi, ti: (bi, jnp.minimum((ti + 1) * halo, n_halo - 1), col))

    def const(shape):
        return pl.BlockSpec(shape, lambda bi, ti: tuple(0 for _ in shape))

    tok = pl.BlockSpec((1, tm, d), lambda bi, ti: (bi, ti, 0))
    half = pl.BlockSpec((1, tm, UNIT), lambda bi, ti: (bi, ti, 0))
    return pl.pallas_call(
        functools.partial(_merge_kernel, ctx_len=ctx_len, seq_len=l, tm=tm, d=d),
        grid=(b, l // tm),
        in_specs=[tok, pl.BlockSpec((1, 2, N_MOD * d), lambda bi, ti: (bi, 0, 0)), half, half,
                  zspec(COL_CB), zspec(COL_CU), prev(COL_CU), nxt(COL_CU),
                  zspec(COL_GA, d), zspec(COL_GB, d), zspec(COL_GC, d),
                  const(cw.shape), const(wa.shape), const(wb.shape), const(wc.shape), const(wo.shape)],
        out_specs=tok,
        out_shape=jax.ShapeDtypeStruct((b, l, d), F32),
        compiler_params=_params("arbitrary", "arbitrary"),
    )(xs, mod, ya, yb, z, z, z, z, z, z, z, cw, wa, wb, wc, wo)


def _mlp_kernel(x_ref, mod_ref, nw_ref, w1_ref, w2_ref, o_ref, *, ctx_len, tm, d, ff_chunk):
    x = x_ref[0]
    h = _modulated_norm(x, nw_ref[...], mod_ref, 3, 4, ctx_len).astype(BF16)
    acc = jnp.zeros((tm, d), F32)
    for c in range(w1_ref.shape[1] // ff_chunk):
        a = jnp.dot(h, w1_ref[:, c * ff_chunk:(c + 1) * ff_chunk], preferred_element_type=F32)
        a = jnp.maximum(a, 0.0)
        acc = acc + jnp.dot((a * a).astype(BF16), w2_ref[c * ff_chunk:(c + 1) * ff_chunk, :],
                            preferred_element_type=F32)
    o_ref[0] = _gated_residual(x, acc, mod_ref, 5, ctx_len)


def _mlp_call(xs, mod, nw, w1, w2, ctx_len):
    b, l, d = xs.shape
    tm = _token_tile(l)
    tok = pl.BlockSpec((1, tm, d), lambda bi, ti: (bi, ti, 0))
    return pl.pallas_call(
        functools.partial(_mlp_kernel, ctx_len=ctx_len, tm=tm, d=d, ff_chunk=1024),
        grid=(b, l // tm),
        in_specs=[tok, pl.BlockSpec((1, 2, N_MOD * d), lambda bi, ti: (bi, 0, 0)),
                  pl.BlockSpec((1, d), lambda bi, ti: (0, 0)),
                  pl.BlockSpec(w1.shape, lambda bi, ti: (0, 0), pipeline_mode=pl.Buffered(1)),
                  pl.BlockSpec(w2.shape, lambda bi, ti: (0, 0), pipeline_mode=pl.Buffered(1))],
        out_specs=tok,
        out_shape=jax.ShapeDtypeStruct((b, l, d), F32),
        compiler_params=_params("arbitrary", "arbitrary"),
    )(xs, mod, nw, w1, w2)


def _reorder_proj(w_in):
    return jnp.concatenate([w_in[..., u * UNIT:(u + 1) * UNIT] for u in W_UNIT_ORDER], axis=-1)


def kernel(x, c, ctx, c_ctx, ada_w, ada_b, norm1_w, norm2_w, w_in, hgrn_lb_logits, hgrn_onorm_w,
           q_norm_w, k_norm_w, natten_rpb, conv_w, w_branch_a, w_branch_b, w_branch_c, w_out,
           mlp_w1, mlp_w2):
    b, s, d = x.shape
    ctx_len = ctx.shape[1]
    depth = ada_w.shape[0]
    assert s % NA_QUERIES == 0 and s // GRID_W >= NA_BAND_ROWS and ctx_len % NA_QUERIES == 0
    assert ctx_len % (HG_CHUNKS_PER_STEP * HG_CHUNK) == 0 and s % (HG_CHUNKS_PER_STEP * HG_CHUNK) == 0
    assert w_in.shape[-1] == PROJ_UNITS * UNIT
    assert ctx_len <= _token_tile(ctx_len + s)

    lb_p = jax.nn.softmax(hgrn_lb_logits.astype(F32), axis=1)
    lower_bounds = jnp.cumsum(lb_p, axis=1) - lb_p[:, :1]

    pad_rows = -(b + 1) % SUBLANES
    cvec = jnp.concatenate([c_ctx[None, :], c, jnp.zeros((pad_rows, d), F32)], axis=0)
    mod_all = _ada_call(cvec, ada_w, ada_b)
    mod_sel = jnp.stack([jnp.broadcast_to(mod_all[:, 0:1], (depth, b, N_MOD * d)),
                         mod_all[:, 1:b + 1]], axis=2)

    cos_t, sin_t = _rope_tables(ctx_len, s)
    bd = jnp.asarray(np.kron(np.eye(NA_HEADS), np.ones((NA_HD, NA_HD))), BF16)
    w_in_b = _reorder_proj(w_in).astype(BF16)

    xs = jnp.concatenate([ctx, x], axis=1)
    for l in range(depth):
        mod = mod_sel[l]
        z, lf, o_f = _inproj_call(xs, mod, norm1_w[l][None, :], w_in_b[l], cos_t, sin_t,
                                  jnp.tile(q_norm_w[l], NA_HEADS)[None, :],
                                  jnp.tile(k_norm_w[l], NA_HEADS)[None, :], bd,
                                  lower_bounds[:, l].reshape(1, 2 * HG_W), ctx_len)
        ya = _hgrn_bwd_call(z, lf, o_f, hgrn_onorm_w[l][None, :], ctx_len)
        yb = _natten_call(z, _bias_table(natten_rpb[l], ctx_len), ctx_len)
        xs = _merge_call(xs, mod, ya, yb, z, conv_w[l], w_branch_a[l].astype(BF16),
                         w_branch_b[l].astype(BF16), w_branch_c[l].astype(BF16),
                         w_out[l].astype(BF16), ctx_len)
        xs = _mlp_call(xs, mod, norm2_w[l][None, :], mlp_w1[l].astype(BF16),
                       mlp_w2[l].astype(BF16), ctx_len)
    return xs[:, ctx_len:, :]
```

```python
import functools

import numpy as np
import jax
import jax.numpy as jnp
from jax import lax
from jax.experimental import pallas as pl
from jax.experimental.pallas import tpu as pltpu

GRID_W = 64
HG_HEADS = 4
HG_DK = 128
HG_W = HG_HEADS * HG_DK
HG_CHUNK = 64
HG_LEVELS = 6
HG_CHUNKS_PER_STEP = 4
NA_HEADS = 8
NA_HD = 64
NA_W = NA_HEADS * NA_HD
NA_WIN_R = 8
NA_WIN_C = 16
ROPE_THETA = 10000.0
SC_W = 512
N_MOD = 6
EPS = 1e-6
NEG_BIG = -1e30
LOG2E = 1.4426950408889634

LANES = 128
SUBLANES = 8
BF16_ROWS = 16
VMEM_LIMIT = 56 * 1024 * 1024

UNIT = 512
PROJ_UNITS = 17
W_UNIT_ORDER = (11, 12, 13, 14, 15, 16, 0, 4, 5, 6, 1, 2, 9, 10, 3, 7, 8)
WU_GATES, WU_SILU, WU_QK, WU_FORGET, WU_CONV, WU_RAW, WU_CB = 0, 6, 8, 10, 12, 14, 16
COL_GA, COL_GB, COL_GC = 0, 2, 4
COL_HQ, COL_HG = 6, 7
COL_NQ, COL_NK = 8, 9
COL_HI, COL_NV = 10, 11
COL_CB, COL_CU = 12, 13
Z_UNITS = 14

F32 = jnp.float32
BF16 = jnp.bfloat16


def _pick_tile(n, target, mult):
    best = None
    for t in range(mult, min(n, target) + 1, mult):
        if n % t == 0:
            best = t
    if best is None:
        raise ValueError(f"no tile for {n} (target {target}, multiple {mult})")
    return best


PROJ_TILE_ROWS = 544
TOKEN_TILE_ROWS = 1088


def _token_tile(l, rows):
    return _pick_tile(l, rows, BF16_ROWS)


def _sigmoid(x):
    return 1.0 / (1.0 + jnp.exp(-x))


def _gate_sigmoid(x):
    return 0.5 * jnp.tanh(0.5 * x) + 0.5


def _params(*sem):
    return pltpu.CompilerParams(dimension_semantics=sem, vmem_limit_bytes=VMEM_LIMIT)


def _ada_kernel(c_ref, w_ref, b_ref, o_ref):
    c = c_ref[...]
    s = c * _sigmoid(c)
    o_ref[0] = jnp.dot(s, w_ref[0], precision=lax.Precision.HIGHEST,
                       preferred_element_type=F32) + b_ref[0]


def _ada_call(cvec, ada_w, ada_b):
    depth, d, n = ada_w.shape
    rows = cvec.shape[0]
    tn = _pick_tile(n, 1024, LANES)
    return pl.pallas_call(
        _ada_kernel,
        grid=(depth, n // tn),
        in_specs=[
            pl.BlockSpec((rows, d), lambda l, j: (0, 0)),
            pl.BlockSpec((1, d, tn), lambda l, j: (l, 0, j)),
            pl.BlockSpec((1, 1, tn), lambda l, j: (l, 0, j)),
        ],
        out_specs=pl.BlockSpec((1, rows, tn), lambda l, j: (l, 0, j)),
        out_shape=jax.ShapeDtypeStruct((depth, rows, n), F32),
        compiler_params=_params("arbitrary", "arbitrary"),
    )(cvec, ada_w, ada_b.reshape(depth, 1, n))


def _segment_mod(mod_ref, idx, d):
    mc = mod_ref[0, 0:1, idx * d:(idx + 1) * d]
    ml = mod_ref[0, 1:2, idx * d:(idx + 1) * d]
    return jnp.where(pl.program_id(1) == 0, mc, ml), ml


def _by_segment(x, ctx_len, head_fn, rest_fn):
    if ctx_len == x.shape[0]:
        return head_fn(x)
    if ctx_len == 0:
        return rest_fn(x)
    return jnp.concatenate([head_fn(x[:ctx_len]), rest_fn(x[ctx_len:])], axis=0)


def _modulated_norm(x, nw, mod_ref, i_shift, i_scale, ctx_len):
    d = x.shape[-1]
    ms = jnp.mean(x * x, axis=-1, keepdims=True)
    xn = x * lax.rsqrt(ms + EPS)
    shifts = _segment_mod(mod_ref, i_shift, d)
    gains = [nw * (1.0 + s) for s in _segment_mod(mod_ref, i_scale, d)]
    return _by_segment(xn, ctx_len, lambda a: a * gains[0] + shifts[0], lambda a: a * gains[1] + shifts[1])


def _gated_residual(x, y, mod_ref, i_gate, ctx_len):
    gates = _segment_mod(mod_ref, i_gate, x.shape[-1])
    return x + _by_segment(y, ctx_len, lambda a: a * gates[0], lambda a: a * gates[1])


def _rotate(x, cos, sin_signed, first_half):
    w = x.shape[-1]
    quarter = NA_HD // 4
    partner = jnp.where(first_half, pltpu.roll(x, w - quarter, 1), pltpu.roll(x, quarter, 1))
    return x * cos + partner * sin_signed


def _inproj_kernel(x_ref, mod_ref, nw_ref, w_ref, cos_ref, sin_ref, qw_ref, kw_ref, bd_ref, lb_ref,
                   z_ref, lf_ref, *, ctx_len):
    xn = _modulated_norm(x_ref[0], nw_ref[...], mod_ref, 0, 1, ctx_len).astype(BF16)
    lane = lax.broadcasted_iota(jnp.int32, (1, NA_W), 1)
    first_half = (lane % (NA_HD // 2)) < (NA_HD // 4)

    def project(u0, n_units):
        return jnp.dot(xn, w_ref[:, u0 * UNIT:(u0 + n_units) * UNIT], preferred_element_type=F32)

    def put(col, val):
        z_ref[0, :, col * UNIT:col * UNIT + val.shape[1]] = val.astype(z_ref.dtype)

    def head_prep(y, sq, w):
        ms = jnp.dot(sq, bd_ref[...], preferred_element_type=F32) * (1.0 / NA_HD)
        return _rotate(y * lax.rsqrt(ms + EPS) * w, cos_ref[...], sin_ref[...], first_half)

    def finish_forget(r):
        lb = lb_ref[...]
        lf_ref[0] = jnp.log(lb + (1.0 - lb) * _sigmoid(r))

    r_qk = project(WU_QK, 2)
    r_forget = project(WU_FORGET, 2)
    sq = (r_qk * r_qk).astype(BF16)
    r_silu = project(WU_SILU, 2)
    finish_forget(r_forget)
    q_ready = head_prep(r_qk[:, :UNIT], sq[:, :UNIT], qw_ref[...]) * (NA_HD ** -0.5 * LOG2E)
    k_ready = head_prep(r_qk[:, UNIT:], sq[:, UNIT:], kw_ref[...])
    r = project(WU_GATES, 2)
    put(COL_NQ, q_ready)
    put(COL_NK, k_ready)
    put(COL_HQ, r_silu * _gate_sigmoid(r_silu))
    plan = [(WU_GATES + 2, 2, lambda g: put(COL_GA, _gate_sigmoid(g))),
            (WU_GATES + 4, 2, lambda g: put(COL_GB, _gate_sigmoid(g))),
            (WU_CONV, 2, lambda g: put(COL_GC, _gate_sigmoid(g))),
            (WU_RAW, 2, lambda cx: put(COL_CU, cx[:, :UNIT] * cx[:, UNIT:])),
            (WU_CB, 1, lambda raw: put(COL_HI, raw))]
    for u0, n_units, finish_previous in plan:
        r_next = project(u0, n_units)
        finish_previous(r)
        r = r_next
    put(COL_CB, r)


def _inproj_call(xs, mod, nw, w, cos, sin, qw, kw, bd, lb, ctx_len):
    b, l, d = xs.shape
    tm = _token_tile(l, PROJ_TILE_ROWS)

    def const(shape):
        return pl.BlockSpec(shape, lambda bi, ti: (0, 0))

    tab = pl.BlockSpec((tm, NA_W), lambda bi, ti: (ti, 0))
    return pl.pallas_call(
        functools.partial(_inproj_kernel, ctx_len=ctx_len),
        grid=(b, l // tm),
        in_specs=[
            pl.BlockSpec((1, tm, d), lambda bi, ti: (bi, ti, 0)),
            pl.BlockSpec((1, 2, N_MOD * d), lambda bi, ti: (bi, 0, 0)),
            const((1, d)),
            pl.BlockSpec(w.shape, lambda bi, ti: (0, 0), pipeline_mode=pl.Buffered(1)),
            tab, tab, const((1, NA_W)), const((1, NA_W)), const((NA_W, NA_W)), const((1, 2 * HG_W)),
        ],
        out_specs=[pl.BlockSpec((1, tm, Z_UNITS * UNIT), lambda bi, ti: (bi, ti, 0)),
                   pl.BlockSpec((1, tm, 2 * HG_W), lambda bi, ti: (bi, ti, 0))],
        out_shape=[jax.ShapeDtypeStruct((b, l, Z_UNITS * UNIT), BF16),
                   jax.ShapeDtypeStruct((b, l, 2 * HG_W), F32)],
        compiler_params=_params("arbitrary", "arbitrary"),
    )(xs, mod, nw, w, cos, sin, qw, kw, bd, lb)


def _boundary_rows(gc, h, reverse):
    t, w = gc.shape
    off = h if reverse else h - 1
    if 2 * h >= SUBLANES:
        g3 = gc.reshape(t // (2 * h), 2 * h, w)
        return jnp.broadcast_to(g3[:, off:off + 1, :], g3.shape).reshape(t, w)
    g3 = gc.reshape(t // SUBLANES, SUBLANES, w)
    sub = lax.broadcasted_iota(jnp.int32, (1, SUBLANES, 1), 1)
    out = None
    for start in range(0, SUBLANES, 2 * h):
        piece = jnp.broadcast_to(g3[:, start + off:start + off + 1, :], g3.shape)
        out = piece if out is None else jnp.where(sub >= start, piece, out)
    return out.reshape(t, w)


def _scan_cumsum(lf, row, reverse):
    t = lf.shape[0]
    within = row & (SUBLANES - 1)
    g = lf
    for dist in (1, 2, 4):
        if reverse:
            g = g + jnp.where(within < SUBLANES - dist, pltpu.roll(g, t - dist, 0), 0.0)
        else:
            g = g + jnp.where(within >= dist, pltpu.roll(g, dist, 0), 0.0)
    groups = list(range(t // SUBLANES))
    pieces = {}
    carry = None
    for gi in (reversed(groups) if reverse else groups):
        piece = g[gi * SUBLANES:(gi + 1) * SUBLANES]
        if carry is not None:
            piece = piece + carry
        pieces[gi] = piece
        carry = piece[0:1] if reverse else piece[SUBLANES - 1:SUBLANES]
    return jnp.concatenate([pieces[gi] for gi in groups], axis=0)


def _level_operand(qs, kk, gc, h, reverse):
    pieces = []
    for s0 in range(0, gc.shape[0], 2 * h):
        first, second = slice(s0, s0 + h), slice(s0 + h, s0 + 2 * h)
        if reverse:
            gb = gc[s0 + h:s0 + h + 1]
            pieces += [qs[first] * jnp.exp(gc[first] - gb), kk[second] * jnp.exp(gb - gc[second])]
        else:
            gb = gc[s0 + h - 1:s0 + h]
            pieces += [kk[first] * jnp.exp(gb - gc[first]), qs[second] * jnp.exp(gc[second] - gb)]
    return jnp.concatenate(pieces, axis=0).astype(BF16)


def _nt_dot(a, b):
    return lax.dot_general(a, b, (((1,), (1,)), ((), ())), preferred_element_type=F32)


def _hg_head(a, hh):
    return a[:, hh * HG_DK:(hh + 1) * HG_DK]


def _chunk_operands(qs, lf, v, reverse):
    t = HG_CHUNK
    forget = jnp.exp(lf)
    kk = 1.0 - forget
    row = lax.broadcasted_iota(jnp.int32, (t, 1), 0)
    gc = _scan_cumsum(lf, row, reverse)
    qbit = 0 if reverse else 1
    levels = [(qs.astype(BF16), kk.astype(BF16))]
    for lvl in range(HG_LEVELS):
        is_q = ((row >> lvl) & 1) == qbit
        if lvl == 0:
            both = jnp.where(is_q, qs * forget, kk).astype(BF16)
        elif (1 << lvl) >= SUBLANES:
            both = _level_operand(qs, kk, gc, 1 << lvl, reverse)
        else:
            e = jnp.exp(-jnp.abs(gc - _boundary_rows(gc, 1 << lvl, reverse)))
            both = (jnp.where(is_q, qs, kk) * e).astype(BF16)
        levels.append((both, both))
    g_end = gc[0:1, :] if reverse else gc[t - 1:t, :]
    return dict(levels=levels,
                qd=(qs * jnp.exp(gc)).astype(BF16),
                kd=(kk * jnp.exp(g_end - gc)).astype(BF16),
                d_end=jnp.exp(g_end),
                v_b=v.astype(BF16),
                vt=[_hg_head(v, hh).T.astype(BF16) for hh in range(HG_HEADS)])


def _chunk_products(ops):
    scores = [[_nt_dot(_hg_head(a, hh), _hg_head(b, hh)) for hh in range(HG_HEADS)] for a, b in ops["levels"]]
    update = [jnp.dot(ops["vt"][hh], _hg_head(ops["kd"], hh), preferred_element_type=F32)
              for hh in range(HG_HEADS)]
    return scores, update


def _chunk_attn(scores, masks):
    attn = []
    for hh in range(HG_HEADS):
        a = jnp.where(masks[0], scores[0][hh], 0.0)
        for lvl in range(HG_LEVELS):
            a = jnp.where(masks[lvl + 1], scores[lvl + 1][hh], a)
        attn.append(a.astype(BF16))
    return attn


def _chunk_intra(attn, ops):
    return [jnp.dot(attn[hh], _hg_head(ops["v_b"], hh), preferred_element_type=F32) for hh in range(HG_HEADS)]


def _hgrn_block(inputs, masks, reverse, between=None):
    n = len(inputs)
    ops, products, attn, intra = ([None] * n for _ in range(4))
    for stage in range(n + 3):
        if between is not None:
            between(stage)
        if 0 <= stage - 3 < n:
            intra[stage - 3] = _chunk_intra(attn[stage - 3], ops[stage - 3])
        if 0 <= stage - 1 < n:
            products[stage - 1] = _chunk_products(ops[stage - 1])
        if 0 <= stage - 2 < n:
            attn[stage - 2] = _chunk_attn(products[stage - 2][0], masks)
        if stage < n:
            ops[stage] = _chunk_operands(*inputs[stage], reverse)
    return [(intra[c], ops[c]["qd"], ops[c]["d_end"], products[c][1]) for c in range(n)]


def _hgrn_scan(parts, subs, st_ref, emit):
    t = HG_CHUNK
    states = [st_ref[hh] for hh in range(HG_HEADS)]
    for sub, (intra, qd, d_end, update) in zip(subs, parts):
        rows = slice(sub * t, (sub + 1) * t)
        for hh in range(HG_HEADS):
            cols = slice(hh * HG_DK, (hh + 1) * HG_DK)
            o_h = intra[hh] + lax.dot_general(qd[:, cols], states[hh].astype(BF16), (((1,), (1,)), ((), ())),
                                              preferred_element_type=F32)
            states[hh] = states[hh] * d_end[:, cols] + update[hh]
            emit(rows, cols, o_h)
    for hh in range(HG_HEADS):
        st_ref[hh] = states[hh]


def _hgrn_kernel(*refs, reverse, n_sub):
    if reverse:
        q_ref, lf_ref, i_ref, hm_ref, of_ref, g_ref, onw_ref, o_ref, st_ref = refs
    else:
        q_ref, lf_ref, i_ref, hm_ref, o_ref, st_ref = refs
    t = HG_CHUNK

    @pl.when(pl.program_id(1) == 0)
    def _():
        st_ref[...] = jnp.zeros_like(st_ref)

    subs = list(range(n_sub - 1, -1, -1) if reverse else range(n_sub))
    masks = [hm_ref[m] != 0.0 for m in range(HG_LEVELS + 1)]
    inputs = []
    for sub in subs:
        rows = slice(sub * t, (sub + 1) * t)
        inputs.append((q_ref[0, rows, :].astype(F32), lf_ref[0, rows, :], i_ref[0, rows, :].astype(F32)))
    parts = _hgrn_block(inputs, masks, reverse)

    def emit(rows, cols, o_h):
        if reverse:
            o_h = o_h + of_ref[0, rows, cols]
            ms = jnp.mean(o_h * o_h, axis=-1, keepdims=True)
            y = o_h * lax.rsqrt(ms + EPS) * onw_ref[...]
            o_ref[0, rows, cols] = (y * g_ref[0, rows, cols].astype(F32)).astype(o_ref.dtype)
        else:
            o_ref[0, rows, cols] = o_h

    _hgrn_scan(parts, subs, st_ref, emit)


def _hgrn_masks(reverse):
    t = HG_CHUNK
    rt = np.arange(t)[:, None]
    rs = np.arange(t)[None, :]
    qbit = 0 if reverse else 1
    masks = [rt == rs]
    for lvl in range(HG_LEVELS):
        masks.append(((rt >> (lvl + 1)) == (rs >> (lvl + 1))) & (((rt >> lvl) & 1) == qbit)
                     & (((rs >> lvl) & 1) != qbit))
    return jnp.asarray(np.stack(masks), F32)


def _hgrn_call(z, lf, ctx_len, reverse, o_fwd=None, onw=None):
    b, l, _ = z.shape
    t = HG_CHUNK
    rows = HG_CHUNKS_PER_STEP * t
    n_blocks = l // rows
    n_ctx = ctx_len // rows

    if reverse:
        def block(i):
            return jnp.where(i < n_ctx, n_ctx - 1 - i, n_blocks - 1 + n_ctx - i)
    else:
        def block(i):
            return i

    def zspec(col):
        return pl.BlockSpec((1, rows, UNIT), lambda bi, i: (bi, block(i), col))

    row_spec = pl.BlockSpec((1, rows, HG_W), lambda bi, i: (bi, block(i), 0))
    in_specs = [zspec(COL_HQ), zspec(1 if reverse else 0), zspec(COL_HI),
                pl.BlockSpec((HG_LEVELS + 1, t, t), lambda bi, i: (0, 0, 0))]
    args = [z, lf, z, _hgrn_masks(reverse)]
    if reverse:
        in_specs += [row_spec, zspec(COL_HG), pl.BlockSpec((1, HG_DK), lambda bi, i: (0, 0))]
        args += [o_fwd, z, onw]
    return pl.pallas_call(
        functools.partial(_hgrn_kernel, reverse=reverse, n_sub=HG_CHUNKS_PER_STEP),
        grid=(b, n_blocks),
        in_specs=in_specs,
        out_specs=row_spec,
        out_shape=jax.ShapeDtypeStruct((b, l, HG_W), BF16 if reverse else F32),
        scratch_shapes=[pltpu.VMEM((HG_HEADS, HG_DK, HG_DK), F32)],
        compiler_params=_params("arbitrary", "arbitrary"),
    )(*args)


NA_ROWS_PER_STEP = 4
NA_QUERIES = NA_ROWS_PER_STEP * GRID_W
NA_BAND_ROWS = NA_WIN_R + NA_ROWS_PER_STEP
NA_BAND = NA_BAND_ROWS * GRID_W
NA_PATTERNS = 3


def _band_start(group_row, rows):
    return jnp.clip(group_row - NA_WIN_R // 2, 0, rows - NA_BAND_ROWS)


def _natten_kernel(q_ref, k_ref, v_ref, bm_ref, o_ref, *, ctx_len, rows):
    i = pl.program_id(1)
    n_ctx_blocks = ctx_len // NA_QUERIES
    pair_w = 2 * NA_HD
    lane = lax.broadcasted_iota(jnp.int32, (1, pair_w), 1)
    low = lane < NA_HD

    def nt_dot(a, b):
        return lax.dot_general(a, b, (((1,), (1,)), ((), ())), preferred_element_type=F32)

    def attend(local):
        if local:
            r = (i - n_ctx_blocks) * NA_ROWS_PER_STEP
            start = pl.multiple_of(ctx_len + _band_start(r, rows) * GRID_W, GRID_W)

        def pair_cols(ref, row_slice, p):
            return ref[0, row_slice, p * pair_w:(p + 1) * pair_w]

        def gather(ref, p):
            ctx_part = pair_cols(ref, slice(0, ctx_len), p)
            if not local:
                return ctx_part
            return jnp.concatenate([pair_cols(ref, pl.ds(start, NA_BAND), p), ctx_part], axis=0)

        def scores(h):
            p, hh = divmod(h, 2)
            q2 = pair_cols(q_ref, slice(None), p)
            qm = jnp.where(low if hh == 0 else ~low, q2, jnp.zeros_like(q2))
            s = nt_dot(qm, gather(k_ref, p))
            return s + bm_ref[0, h] if local else s

        s_next = scores(0)
        halves = []
        for h in range(NA_HEADS):
            s = s_next
            if h + 1 < NA_HEADS:
                s_next = scores(h + 1)
            e = jnp.exp2(s - jnp.max(s, axis=-1, keepdims=True))
            o = jnp.dot(e.astype(BF16), gather(v_ref, h // 2), preferred_element_type=F32)
            halves.append(o / jnp.sum(e, axis=-1, keepdims=True))
            if h % 2 == 1:
                p = h // 2
                o_ref[0, :, p * pair_w:(p + 1) * pair_w] = jnp.where(low, *halves).astype(o_ref.dtype)
                halves = []

    @pl.when(i < n_ctx_blocks)
    def _():
        attend(False)

    @pl.when(i >= n_ctx_blocks)
    def _():
        attend(True)


def _natten_call(z, bias, ctx_len):
    b, l, _ = z.shape
    rows = (l - ctx_len) // GRID_W
    n_ctx_blocks = ctx_len // NA_QUERIES

    def pattern(i):
        r = jnp.maximum(i - n_ctx_blocks, 0) * NA_ROWS_PER_STEP
        return (r - _band_start(r, rows)) // NA_ROWS_PER_STEP

    def full(col):
        return pl.BlockSpec((1, l, NA_W), lambda bi, i: (bi, 0, col), pipeline_mode=pl.Buffered(1))

    return pl.pallas_call(
        functools.partial(_natten_kernel, ctx_len=ctx_len, rows=rows),
        grid=(b, l // NA_QUERIES),
        in_specs=[pl.BlockSpec((1, NA_QUERIES, NA_W), lambda bi, i: (bi, i, COL_NQ)),
                  full(COL_NK), full(COL_NV),
                  pl.BlockSpec((1, NA_HEADS, NA_QUERIES, NA_BAND + ctx_len),
                               lambda bi, i: (pattern(i), 0, 0, 0))],
        out_specs=pl.BlockSpec((1, NA_QUERIES, NA_W), lambda bi, i: (bi, i, 0)),
        out_shape=jax.ShapeDtypeStruct((b, l, NA_W), BF16),
        compiler_params=_params("arbitrary", "arbitrary"),
    )(z, z, z, bias)


def _bias_kernel(cols_ref, o_ref):
    g = pl.program_id(0)
    for a in range(NA_ROWS_PER_STEP):
        r_rel = NA_ROWS_PER_STEP * g + a
        r0_rel = jnp.clip(r_rel - NA_WIN_R // 2, 0, NA_BAND_ROWS - NA_WIN_R)
        for j in range(NA_BAND_ROWS):
            in_window = (j >= r0_rel) & (j < r0_rel + NA_WIN_R)
            plane = cols_ref[0, jnp.clip(j - r_rel + NA_WIN_R - 1, 0, 2 * NA_WIN_R - 2)]
            o_ref[0, 0, a * GRID_W:(a + 1) * GRID_W, j * GRID_W:(j + 1) * GRID_W] = jnp.where(
                in_window, plane, NEG_BIG)
    o_ref[0, 0, :, NA_BAND:] = jnp.zeros((NA_QUERIES, o_ref.shape[3] - NA_BAND), F32)


def _bias_table(rpb, ctx_len):
    qc = np.arange(GRID_W)[:, None]
    kc = np.arange(GRID_W)[None, :]
    c0 = np.clip(qc - NA_WIN_C // 2, 0, GRID_W - NA_WIN_C)
    col_ok = (kc >= c0) & (kc < c0 + NA_WIN_C)
    n_ci = 2 * NA_WIN_C - 1
    n_ri = 2 * NA_WIN_R - 1
    onehot = (kc - qc + NA_WIN_C - 1)[:, :, None] == np.arange(n_ci)
    cols = jnp.einsum('hrc,qkc->hrqk', rpb.astype(F32), jnp.asarray(onehot, F32),
                      precision=lax.Precision.HIGHEST)
    cols = jnp.where(col_ok[None, None], cols * LOG2E, NEG_BIG)
    n_keys = NA_BAND + ctx_len
    return pl.pallas_call(
        _bias_kernel,
        grid=(NA_PATTERNS, NA_HEADS),
        in_specs=[pl.BlockSpec((1, n_ri, GRID_W, GRID_W), lambda g, h: (h, 0, 0, 0))],
        out_specs=pl.BlockSpec((1, 1, NA_QUERIES, n_keys), lambda g, h: (g, h, 0, 0)),
        out_shape=jax.ShapeDtypeStruct((NA_PATTERNS, NA_HEADS, NA_QUERIES, n_keys), F32),
        compiler_params=_params("arbitrary", "arbitrary"),
    )(cols)


def _rope_tables(ctx_len, s):
    tpos = np.arange(s)
    pos = np.stack([tpos // GRID_W, tpos % GRID_W], axis=-1).astype(np.float32)
    half = NA_HD // 2
    inv = (ROPE_THETA ** (-jnp.arange(0, half, 2, dtype=F32) / half))
    ang = jnp.asarray(pos)[:, :, None] * inv
    cos = jnp.cos(ang)
    sin = jnp.sin(ang)
    cos_h = jnp.concatenate([cos, cos], axis=-1).reshape(s, NA_HD)
    sin_h = jnp.concatenate([-sin, sin], axis=-1).reshape(s, NA_HD)
    cos_t = jnp.tile(cos_h, (1, NA_HEADS))
    sin_t = jnp.tile(sin_h, (1, NA_HEADS))
    cos_t = jnp.concatenate([jnp.ones((ctx_len, NA_W), F32), cos_t], axis=0)
    sin_t = jnp.concatenate([jnp.zeros((ctx_len, NA_W), F32), sin_t], axis=0)
    return cos_t, sin_t


def _merge_kernel(x_ref, mod_ref, ya_ref, yb_ref, cb_ref, u_ref, up_ref, un_ref, ga_ref, gb_ref, gc_ref,
                  cw_ref, wa_ref, wb_ref, wc_ref, wo_ref, o_ref, *, ctx_len, seq_len, tm, d):
    ti = pl.program_id(1)
    row = ti * tm + lax.broadcasted_iota(jnp.int32, (tm, 1), 0)
    local = lax.broadcasted_iota(jnp.int32, (tm, 1), 0)

    u = u_ref[0].astype(F32)
    u_before = up_ref[0, BF16_ROWS - 1:BF16_ROWS, :].astype(F32)
    u_after = un_ref[0, 0:1, :].astype(F32)
    up = jnp.where(local == 0, u_before, pltpu.roll(u, 1, 0))
    un = jnp.where(local == tm - 1, u_after, pltpu.roll(u, tm - 1, 0))
    has_prev = (row != 0) & (row != ctx_len)
    has_next = (row != ctx_len - 1) & (row != seq_len - 1)
    cw = cw_ref[...]
    conv = (jnp.where(has_prev, up, 0.0) * cw[0:1, :] + u * cw[1:2, :]
            + jnp.where(has_next, un, 0.0) * cw[2:3, :])
    yc = (cb_ref[0].astype(F32) * conv).astype(BF16)

    mix = ga_ref[0].astype(F32) * jnp.dot(ya_ref[0], wa_ref[...], preferred_element_type=F32)
    mix = mix + gb_ref[0].astype(F32) * jnp.dot(yb_ref[0], wb_ref[...], preferred_element_type=F32)
    mix = mix + gc_ref[0].astype(F32) * jnp.dot(yc, wc_ref[...], preferred_element_type=F32)
    out = jnp.dot(mix.astype(BF16), wo_ref[...], preferred_element_type=F32)
    o_ref[0] = _gated_residual(x_ref[0], out, mod_ref, 2, ctx_len)


def _merge_call(xs, mod, ya, yb, z, cw, wa, wb, wc, wo, ctx_len):
    b, l, d = xs.shape
    tm = _token_tile(l, TOKEN_TILE_ROWS)
    halo = tm // BF16_ROWS
    n_halo = l // BF16_ROWS

    def zspec(col, width=UNIT):
        return pl.BlockSpec((1, tm, width), lambda bi, ti: (bi, ti, col * UNIT // width))

    def prev(col):
        return pl.BlockSpec((1, BF16_ROWS, UNIT), lambda bi, ti: (bi, jnp.maximum(ti * halo - 1, 0), col))

    def nxt(col):
        return pl.BlockSpec((1, BF16_ROWS, UNIT),
                            lambda bi, ti: (bi, jnp.minimum((ti + 1) * halo, n_halo - 1), col))

    def const(shape):
        return pl.BlockSpec(shape, lambda bi, ti: tuple(0 for _ in shape))

    tok = pl.BlockSpec((1, tm, d), lambda bi, ti: (bi, ti, 0))
    half = pl.BlockSpec((1, tm, UNIT), lambda bi, ti: (bi, ti, 0))
    return pl.pallas_call(
        functools.partial(_merge_kernel, ctx_len=ctx_len, seq_len=l, tm=tm, d=d),
        grid=(b, l // tm),
        in_specs=[tok, pl.BlockSpec((1, 2, N_MOD * d), lambda bi, ti: (bi, 0, 0)), half, half,
                  zspec(COL_CB), zspec(COL_CU), prev(COL_CU), nxt(COL_CU),
                  zspec(COL_GA, d), zspec(COL_GB, d), zspec(COL_GC, d),
                  const(cw.shape), const(wa.shape), const(wb.shape), const(wc.shape), const(wo.shape)],
        out_specs=tok,
        out_shape=jax.ShapeDtypeStruct((b, l, d), F32),
        compiler_params=_params("arbitrary", "arbitrary"),
    )(xs, mod, ya, yb, z, z, z, z, z, z, z, cw, wa, wb, wc, wo)


def _mlp_kernel(x_ref, mod_ref, nw_ref, w1_ref, w2_ref, o_ref, *, ctx_len, tm, d, ff_chunk):
    x = x_ref[0]
    h = _modulated_norm(x, nw_ref[...], mod_ref, 3, 4, ctx_len).astype(BF16)
    acc = jnp.zeros((tm, d), F32)
    for c in range(w1_ref.shape[1] // ff_chunk):
        a = jnp.dot(h, w1_ref[:, c * ff_chunk:(c + 1) * ff_chunk], preferred_element_type=F32)
        a = jnp.maximum(a, 0.0)
        acc = acc + jnp.dot((a * a).astype(BF16), w2_ref[c * ff_chunk:(c + 1) * ff_chunk, :],
                            preferred_element_type=F32)
    o_ref[0] = _gated_residual(x, acc, mod_ref, 5, ctx_len)


def _mlp_call(xs, mod, nw, w1, w2, ctx_len, latent_only=False):
    b, l, d = xs.shape
    first = ctx_len if latent_only else 0
    tm = _token_tile(l - first, TOKEN_TILE_ROWS)
    if latent_only:
        x_spec = pl.BlockSpec((pl.Element(1), pl.Element(tm), pl.Element(d)),
                              lambda bi, ti: (bi, pl.multiple_of(first + ti * tm, BF16_ROWS), 0))
    else:
        x_spec = pl.BlockSpec((1, tm, d), lambda bi, ti: (bi, ti, 0))
    return pl.pallas_call(
        functools.partial(_mlp_kernel, ctx_len=0 if latent_only else ctx_len, tm=tm, d=d, ff_chunk=1024),
        grid=(b, (l - first) // tm),
        in_specs=[x_spec, pl.BlockSpec((1, 2, N_MOD * d), lambda bi, ti: (bi, 0, 0)),
                  pl.BlockSpec((1, d), lambda bi, ti: (0, 0)),
                  pl.BlockSpec(w1.shape, lambda bi, ti: (0, 0), pipeline_mode=pl.Buffered(1)),
                  pl.BlockSpec(w2.shape, lambda bi, ti: (0, 0), pipeline_mode=pl.Buffered(1))],
        out_specs=pl.BlockSpec((1, tm, d), lambda bi, ti: (bi, ti, 0)),
        out_shape=jax.ShapeDtypeStruct((b, l - first, d), F32),
        compiler_params=_params("arbitrary", "arbitrary"),
    )(xs, mod, nw, w1, w2)


def _reorder_proj(w_in):
    return jnp.concatenate([w_in[..., u * UNIT:(u + 1) * UNIT] for u in W_UNIT_ORDER], axis=-1)


def kernel(x, c, ctx, c_ctx, ada_w, ada_b, norm1_w, norm2_w, w_in, hgrn_lb_logits, hgrn_onorm_w,
           q_norm_w, k_norm_w, natten_rpb, conv_w, w_branch_a, w_branch_b, w_branch_c, w_out,
           mlp_w1, mlp_w2):
    b, s, d = x.shape
    ctx_len = ctx.shape[1]
    depth = ada_w.shape[0]
    assert s % NA_QUERIES == 0 and s // GRID_W >= NA_BAND_ROWS and ctx_len % NA_QUERIES == 0
    assert ctx_len % (HG_CHUNKS_PER_STEP * HG_CHUNK) == 0 and s % (HG_CHUNKS_PER_STEP * HG_CHUNK) == 0
    assert w_in.shape[-1] == PROJ_UNITS * UNIT
    assert ctx_len <= _token_tile(ctx_len + s, PROJ_TILE_ROWS)

    lb_p = jax.nn.softmax(hgrn_lb_logits.astype(F32), axis=1)
    lower_bounds = jnp.cumsum(lb_p, axis=1) - lb_p[:, :1]

    pad_rows = -(b + 1) % SUBLANES
    cvec = jnp.concatenate([c_ctx[None, :], c, jnp.zeros((pad_rows, d), F32)], axis=0)
    mod_all = _ada_call(cvec, ada_w, ada_b)
    mod_sel = jnp.stack([jnp.broadcast_to(mod_all[:, 0:1], (depth, b, N_MOD * d)),
                         mod_all[:, 1:b + 1]], axis=2)

    cos_t, sin_t = _rope_tables(ctx_len, s)
    bd = jnp.asarray(np.kron(np.eye(NA_HEADS), np.ones((NA_HD, NA_HD))), BF16)
    w_in_b = _reorder_proj(w_in).astype(BF16)

    xs = jnp.concatenate([ctx, x], axis=1)
    for l in range(depth):
        mod = mod_sel[l]
        z, lf = _inproj_call(xs, mod, norm1_w[l][None, :], w_in_b[l], cos_t, sin_t,
                             jnp.tile(q_norm_w[l], NA_HEADS)[None, :],
                             jnp.tile(k_norm_w[l], NA_HEADS)[None, :], bd,
                             lower_bounds[:, l].reshape(1, 2 * HG_W), ctx_len)
        o_f = _hgrn_call(z, lf, ctx_len, False)
        ya = _hgrn_call(z, lf, ctx_len, True, o_f, hgrn_onorm_w[l][None, :])
        yb = _natten_call(z, _bias_table(natten_rpb[l], ctx_len), ctx_len)
        xs = _merge_call(xs, mod, ya, yb, z, conv_w[l], w_branch_a[l].astype(BF16),
                         w_branch_b[l].astype(BF16), w_branch_c[l].astype(BF16),
                         w_out[l].astype(BF16), ctx_len)
        xs = _mlp_call(xs, mod, norm2_w[l][None, :], mlp_w1[l].astype(BF16),
                       mlp_w2[l].astype(BF16), ctx_len, latent_only=l == depth - 1)
    return xs
```

```python
import functools

import numpy as np
import jax
import jax.numpy as jnp
from jax import lax
from jax.experimental import pallas as pl
from jax.experimental.pallas import tpu as pltpu

GRID_W = 64
HG_HEADS = 4
HG_DK = 128
HG_W = HG_HEADS * HG_DK
HG_CHUNK = 64
HG_LEVELS = 6
HG_CHUNKS_PER_STEP = 4
NA_HEADS = 8
NA_HD = 64
NA_W = NA_HEADS * NA_HD
NA_WIN_R = 8
NA_WIN_C = 16
ROPE_THETA = 10000.0
SC_W = 512
N_MOD = 6
EPS = 1e-6
NEG_BIG = -1e30
LOG2E = 1.4426950408889634

LANES = 128
SUBLANES = 8
BF16_ROWS = 16
VMEM_LIMIT = 56 * 1024 * 1024

UNIT = 512
PROJ_UNITS = 17
W_UNIT_ORDER = (11, 12, 13, 14, 15, 16, 0, 4, 5, 6, 1, 2, 9, 10, 3, 7, 8)
WU_GATES, WU_SILU, WU_QK, WU_FORGET, WU_CONV, WU_RAW, WU_CB = 0, 6, 8, 10, 12, 14, 16
COL_GA, COL_GB, COL_GC = 0, 2, 4
COL_HQ, COL_HG = 6, 7
COL_NQ, COL_NK = 8, 9
COL_HI, COL_NV = 10, 11
COL_CB, COL_CU = 12, 13
Z_UNITS = 14

F32 = jnp.float32
BF16 = jnp.bfloat16


def _pick_tile(n, target, mult):
    best = None
    for t in range(mult, min(n, target) + 1, mult):
        if n % t == 0:
            best = t
    if best is None:
        raise ValueError(f"no tile for {n} (target {target}, multiple {mult})")
    return best


PROJ_TILE_ROWS = 544
TOKEN_TILE_ROWS = 1088


def _token_tile(l, rows):
    return _pick_tile(l, rows, BF16_ROWS)


def _sigmoid(x):
    return 1.0 / (1.0 + jnp.exp(-x))


def _gate_sigmoid(x):
    return 0.5 * jnp.tanh(0.5 * x) + 0.5


def _params(*sem):
    return pltpu.CompilerParams(dimension_semantics=sem, vmem_limit_bytes=VMEM_LIMIT)


def _ada_kernel(c_ref, w_ref, b_ref, o_ref):
    c = c_ref[...]
    s = c * _sigmoid(c)
    o_ref[0] = jnp.dot(s, w_ref[0], precision=lax.Precision.HIGHEST,
                       preferred_element_type=F32) + b_ref[0]


def _ada_call(cvec, ada_w, ada_b):
    depth, d, n = ada_w.shape
    rows = cvec.shape[0]
    tn = _pick_tile(n, 1024, LANES)
    return pl.pallas_call(
        _ada_kernel,
        grid=(depth, n // tn),
        in_specs=[
            pl.BlockSpec((rows, d), lambda l, j: (0, 0)),
            pl.BlockSpec((1, d, tn), lambda l, j: (l, 0, j)),
            pl.BlockSpec((1, 1, tn), lambda l, j: (l, 0, j)),
        ],
        out_specs=pl.BlockSpec((1, rows, tn), lambda l, j: (l, 0, j)),
        out_shape=jax.ShapeDtypeStruct((depth, rows, n), F32),
        compiler_params=_params("arbitrary", "arbitrary"),
    )(cvec, ada_w, ada_b.reshape(depth, 1, n))


def _segment_mod(mod_ref, idx, d):
    mc = mod_ref[0, 0:1, idx * d:(idx + 1) * d]
    ml = mod_ref[0, 1:2, idx * d:(idx + 1) * d]
    return jnp.where(pl.program_id(1) == 0, mc, ml), ml


def _by_segment(x, ctx_len, head_fn, rest_fn):
    if ctx_len == x.shape[0]:
        return head_fn(x)
    if ctx_len == 0:
        return rest_fn(x)
    return jnp.concatenate([head_fn(x[:ctx_len]), rest_fn(x[ctx_len:])], axis=0)


def _modulated_norm(x, nw, mod_ref, i_shift, i_scale, ctx_len):
    d = x.shape[-1]
    ms = jnp.mean(x * x, axis=-1, keepdims=True)
    xn = x * lax.rsqrt(ms + EPS)
    shifts = _segment_mod(mod_ref, i_shift, d)
    gains = [nw * (1.0 + s) for s in _segment_mod(mod_ref, i_scale, d)]
    return _by_segment(xn, ctx_len, lambda a: a * gains[0] + shifts[0], lambda a: a * gains[1] + shifts[1])


def _gated_residual(x, y, mod_ref, i_gate, ctx_len):
    gates = _segment_mod(mod_ref, i_gate, x.shape[-1])
    return x + _by_segment(y, ctx_len, lambda a: a * gates[0], lambda a: a * gates[1])


def _rotate(x, cos, sin_signed, first_half):
    w = x.shape[-1]
    quarter = NA_HD // 4
    partner = jnp.where(first_half, pltpu.roll(x, w - quarter, 1), pltpu.roll(x, quarter, 1))
    return x * cos + partner * sin_signed


def _inproj_kernel(x_ref, mod_ref, nw_ref, w_ref, cos_ref, sin_ref, qw_ref, kw_ref, bd_ref, lb_ref,
                   z_ref, lf_ref, *, ctx_len):
    xn = _modulated_norm(x_ref[0], nw_ref[...], mod_ref, 0, 1, ctx_len).astype(BF16)
    lane = lax.broadcasted_iota(jnp.int32, (1, NA_W), 1)
    first_half = (lane % (NA_HD // 2)) < (NA_HD // 4)

    def project(u0, n_units):
        return jnp.dot(xn, w_ref[:, u0 * UNIT:(u0 + n_units) * UNIT], preferred_element_type=F32)

    def put(col, val):
        z_ref[0, :, col * UNIT:col * UNIT + val.shape[1]] = val.astype(z_ref.dtype)

    def head_prep(y, sq, w):
        ms = jnp.dot(sq, bd_ref[...], preferred_element_type=F32) * (1.0 / NA_HD)
        return _rotate(y * lax.rsqrt(ms + EPS) * w, cos_ref[...], sin_ref[...], first_half)

    def finish_forget(r):
        lb = lb_ref[...]
        lf_ref[0] = jnp.log(lb + (1.0 - lb) * _sigmoid(r))

    r_qk = project(WU_QK, 2)
    r_forget = project(WU_FORGET, 2)
    sq = (r_qk * r_qk).astype(BF16)
    r_silu = project(WU_SILU, 2)
    finish_forget(r_forget)
    q_ready = head_prep(r_qk[:, :UNIT], sq[:, :UNIT], qw_ref[...]) * (NA_HD ** -0.5 * LOG2E)
    k_ready = head_prep(r_qk[:, UNIT:], sq[:, UNIT:], kw_ref[...])
    r = project(WU_GATES, 2)
    put(COL_NQ, q_ready)
    put(COL_NK, k_ready)
    put(COL_HQ, r_silu * _gate_sigmoid(r_silu))
    plan = [(WU_GATES + 2, 2, lambda g: put(COL_GA, _gate_sigmoid(g))),
            (WU_GATES + 4, 2, lambda g: put(COL_GB, _gate_sigmoid(g))),
            (WU_CONV, 2, lambda g: put(COL_GC, _gate_sigmoid(g))),
            (WU_RAW, 2, lambda cx: put(COL_CU, cx[:, :UNIT] * cx[:, UNIT:])),
            (WU_CB, 1, lambda raw: put(COL_HI, raw))]
    for u0, n_units, finish_previous in plan:
        r_next = project(u0, n_units)
        finish_previous(r)
        r = r_next
    put(COL_CB, r)


def _inproj_call(xs, mod, nw, w, cos, sin, qw, kw, bd, lb, ctx_len):
    b, l, d = xs.shape
    tm = _token_tile(l, PROJ_TILE_ROWS)

    def const(shape):
        return pl.BlockSpec(shape, lambda bi, ti: (0, 0))

    tab = pl.BlockSpec((tm, NA_W), lambda bi, ti: (ti, 0))
    return pl.pallas_call(
        functools.partial(_inproj_kernel, ctx_len=ctx_len),
        grid=(b, l // tm),
        in_specs=[
            pl.BlockSpec((1, tm, d), lambda bi, ti: (bi, ti, 0)),
            pl.BlockSpec((1, 2, N_MOD * d), lambda bi, ti: (bi, 0, 0)),
            const((1, d)),
            pl.BlockSpec(w.shape, lambda bi, ti: (0, 0), pipeline_mode=pl.Buffered(1)),
            tab, tab, const((1, NA_W)), const((1, NA_W)), const((NA_W, NA_W)), const((1, 2 * HG_W)),
        ],
        out_specs=[pl.BlockSpec((1, tm, Z_UNITS * UNIT), lambda bi, ti: (bi, ti, 0)),
                   pl.BlockSpec((1, tm, 2 * HG_W), lambda bi, ti: (bi, ti, 0))],
        out_shape=[jax.ShapeDtypeStruct((b, l, Z_UNITS * UNIT), BF16),
                   jax.ShapeDtypeStruct((b, l, 2 * HG_W), F32)],
        compiler_params=_params("arbitrary", "arbitrary"),
    )(xs, mod, nw, w, cos, sin, qw, kw, bd, lb)


def _boundary_rows(gc, h, reverse):
    t, w = gc.shape
    off = h if reverse else h - 1
    if 2 * h >= SUBLANES:
        g3 = gc.reshape(t // (2 * h), 2 * h, w)
        return jnp.broadcast_to(g3[:, off:off + 1, :], g3.shape).reshape(t, w)
    g3 = gc.reshape(t // SUBLANES, SUBLANES, w)
    sub = lax.broadcasted_iota(jnp.int32, (1, SUBLANES, 1), 1)
    out = None
    for start in range(0, SUBLANES, 2 * h):
        piece = jnp.broadcast_to(g3[:, start + off:start + off + 1, :], g3.shape)
        out = piece if out is None else jnp.where(sub >= start, piece, out)
    return out.reshape(t, w)


def _scan_cumsum(lf, row, reverse):
    t = lf.shape[0]
    within = row & (SUBLANES - 1)
    g = lf
    for dist in (1, 2, 4):
        if reverse:
            g = g + jnp.where(within < SUBLANES - dist, pltpu.roll(g, t - dist, 0), 0.0)
        else:
            g = g + jnp.where(within >= dist, pltpu.roll(g, dist, 0), 0.0)
    groups = list(range(t // SUBLANES))
    pieces = {}
    carry = None
    for gi in (reversed(groups) if reverse else groups):
        piece = g[gi * SUBLANES:(gi + 1) * SUBLANES]
        if carry is not None:
            piece = piece + carry
        pieces[gi] = piece
        carry = piece[0:1] if reverse else piece[SUBLANES - 1:SUBLANES]
    return jnp.concatenate([pieces[gi] for gi in groups], axis=0)


def _level_operand(qs, kk, gc, h, reverse):
    pieces = []
    for s0 in range(0, gc.shape[0], 2 * h):
        first, second = slice(s0, s0 + h), slice(s0 + h, s0 + 2 * h)
        if reverse:
            gb = gc[s0 + h:s0 + h + 1]
            pieces += [qs[first] * jnp.exp(gc[first] - gb), kk[second] * jnp.exp(gb - gc[second])]
        else:
            gb = gc[s0 + h - 1:s0 + h]
            pieces += [kk[first] * jnp.exp(gb - gc[first]), qs[second] * jnp.exp(gc[second] - gb)]
    return jnp.concatenate(pieces, axis=0).astype(BF16)


def _nt_dot(a, b):
    return lax.dot_general(a, b, (((1,), (1,)), ((), ())), preferred_element_type=F32)


def _hg_head(a, hh):
    return a[:, hh * HG_DK:(hh + 1) * HG_DK]


def _chunk_operands(qs, lf, v, reverse):
    t = HG_CHUNK
    forget = jnp.exp(lf)
    kk = 1.0 - forget
    row = lax.broadcasted_iota(jnp.int32, (t, 1), 0)
    gc = _scan_cumsum(lf, row, reverse)
    qbit = 0 if reverse else 1
    levels = [(qs.astype(BF16), kk.astype(BF16))]
    for lvl in range(HG_LEVELS):
        is_q = ((row >> lvl) & 1) == qbit
        if lvl == 0:
            both = jnp.where(is_q, qs * forget, kk).astype(BF16)
        elif (1 << lvl) >= SUBLANES:
            both = _level_operand(qs, kk, gc, 1 << lvl, reverse)
        else:
            e = jnp.exp(-jnp.abs(gc - _boundary_rows(gc, 1 << lvl, reverse)))
            both = (jnp.where(is_q, qs, kk) * e).astype(BF16)
        levels.append((both, both))
    g_end = gc[0:1, :] if reverse else gc[t - 1:t, :]
    return dict(levels=levels,
                qd=(qs * jnp.exp(gc)).astype(BF16),
                kd=(kk * jnp.exp(g_end - gc)).astype(BF16),
                d_end=jnp.exp(g_end),
                v_b=v.astype(BF16),
                vt=[_hg_head(v, hh).T.astype(BF16) for hh in range(HG_HEADS)])


def _chunk_products(ops):
    scores = [[_nt_dot(_hg_head(a, hh), _hg_head(b, hh)) for hh in range(HG_HEADS)] for a, b in ops["levels"]]
    update = [jnp.dot(ops["vt"][hh], _hg_head(ops["kd"], hh), preferred_element_type=F32)
              for hh in range(HG_HEADS)]
    return scores, update


def _chunk_attn(scores, masks):
    attn = []
    for hh in range(HG_HEADS):
        a = jnp.where(masks[0], scores[0][hh], 0.0)
        for lvl in range(HG_LEVELS):
            a = jnp.where(masks[lvl + 1], scores[lvl + 1][hh], a)
        attn.append(a.astype(BF16))
    return attn


def _chunk_intra(attn, ops):
    return [jnp.dot(attn[hh], _hg_head(ops["v_b"], hh), preferred_element_type=F32) for hh in range(HG_HEADS)]


def _hgrn_block(inputs, masks, reverse, between=None):
    n = len(inputs)
    ops, products, attn, intra = ([None] * n for _ in range(4))
    for stage in range(n + 3):
        if between is not None:
            between(stage)
        if 0 <= stage - 3 < n:
            intra[stage - 3] = _chunk_intra(attn[stage - 3], ops[stage - 3])
        if 0 <= stage - 1 < n:
            products[stage - 1] = _chunk_products(ops[stage - 1])
        if 0 <= stage - 2 < n:
            attn[stage - 2] = _chunk_attn(products[stage - 2][0], masks)
        if stage < n:
            ops[stage] = _chunk_operands(*inputs[stage], reverse)
    return [(intra[c], ops[c]["qd"], ops[c]["d_end"], products[c][1]) for c in range(n)]


def _hgrn_scan(parts, subs, st_ref, emit):
    t = HG_CHUNK
    states = [st_ref[hh] for hh in range(HG_HEADS)]
    for sub, (intra, qd, d_end, update) in zip(subs, parts):
        rows = slice(sub * t, (sub + 1) * t)
        for hh in range(HG_HEADS):
            cols = slice(hh * HG_DK, (hh + 1) * HG_DK)
            o_h = intra[hh] + lax.dot_general(qd[:, cols], states[hh].astype(BF16), (((1,), (1,)), ((), ())),
                                              preferred_element_type=F32)
            states[hh] = states[hh] * d_end[:, cols] + update[hh]
            emit(rows, cols, o_h)
    for hh in range(HG_HEADS):
        st_ref[hh] = states[hh]


def _hgrn_kernel(*refs, reverse, n_sub):
    if reverse:
        q_ref, lf_ref, i_ref, hm_ref, of_ref, g_ref, onw_ref, o_ref, st_ref = refs
    else:
        q_ref, lf_ref, i_ref, hm_ref, o_ref, st_ref = refs
    t = HG_CHUNK

    @pl.when(pl.program_id(1) == 0)
    def _():
        st_ref[...] = jnp.zeros_like(st_ref)

    subs = list(range(n_sub - 1, -1, -1) if reverse else range(n_sub))
    masks = [hm_ref[m] != 0.0 for m in range(HG_LEVELS + 1)]
    inputs = []
    for sub in subs:
        rows = slice(sub * t, (sub + 1) * t)
        inputs.append((q_ref[0, rows, :].astype(F32), lf_ref[0, rows, :], i_ref[0, rows, :].astype(F32)))
    parts = _hgrn_block(inputs, masks, reverse)

    def emit(rows, cols, o_h):
        if reverse:
            o_h = o_h + of_ref[0, rows, cols]
            ms = jnp.mean(o_h * o_h, axis=-1, keepdims=True)
            y = o_h * lax.rsqrt(ms + EPS) * onw_ref[...]
            o_ref[0, rows, cols] = (y * g_ref[0, rows, cols].astype(F32)).astype(o_ref.dtype)
        else:
            o_ref[0, rows, cols] = o_h

    _hgrn_scan(parts, subs, st_ref, emit)


def _hgrn_masks(reverse):
    t = HG_CHUNK
    rt = np.arange(t)[:, None]
    rs = np.arange(t)[None, :]
    qbit = 0 if reverse else 1
    masks = [rt == rs]
    for lvl in range(HG_LEVELS):
        masks.append(((rt >> (lvl + 1)) == (rs >> (lvl + 1))) & (((rt >> lvl) & 1) == qbit)
                     & (((rs >> lvl) & 1) != qbit))
    return jnp.asarray(np.stack(masks), F32)


def _hgrn_call(z, lf, ctx_len, reverse, o_fwd=None, onw=None):
    b, l, _ = z.shape
    t = HG_CHUNK
    rows = HG_CHUNKS_PER_STEP * t
    n_blocks = l // rows
    n_ctx = ctx_len // rows

    if reverse:
        def block(i):
            return jnp.where(i < n_ctx, n_ctx - 1 - i, n_blocks - 1 + n_ctx - i)
    else:
        def block(i):
            return i

    def zspec(col):
        return pl.BlockSpec((1, rows, UNIT), lambda bi, i: (bi, block(i), col))

    row_spec = pl.BlockSpec((1, rows, HG_W), lambda bi, i: (bi, block(i), 0))
    in_specs = [zspec(COL_HQ), zspec(1 if reverse else 0), zspec(COL_HI),
                pl.BlockSpec((HG_LEVELS + 1, t, t), lambda bi, i: (0, 0, 0))]
    args = [z, lf, z, _hgrn_masks(reverse)]
    if reverse:
        in_specs += [row_spec, zspec(COL_HG), pl.BlockSpec((1, HG_DK), lambda bi, i: (0, 0))]
        args += [o_fwd, z, onw]
    return pl.pallas_call(
        functools.partial(_hgrn_kernel, reverse=reverse, n_sub=HG_CHUNKS_PER_STEP),
        grid=(b, n_blocks),
        in_specs=in_specs,
        out_specs=row_spec,
        out_shape=jax.ShapeDtypeStruct((b, l, HG_W), BF16 if reverse else F32),
        scratch_shapes=[pltpu.VMEM((HG_HEADS, HG_DK, HG_DK), F32)],
        compiler_params=_params("arbitrary", "arbitrary"),
    )(*args)


NA_ROWS_PER_STEP = 4
NA_QUERIES = NA_ROWS_PER_STEP * GRID_W
NA_BAND_ROWS = NA_WIN_R + NA_ROWS_PER_STEP
NA_BAND = NA_BAND_ROWS * GRID_W
NA_PATTERNS = 3


def _band_start(group_row, rows):
    return jnp.clip(group_row - NA_WIN_R // 2, 0, rows - NA_BAND_ROWS)


def _natten_kernel(q_ref, k_ref, v_ref, bm_ref, o_ref, *, ctx_len, rows):
    i = pl.program_id(1)
    n_ctx_blocks = ctx_len // NA_QUERIES
    pair_w = 2 * NA_HD
    lane = lax.broadcasted_iota(jnp.int32, (1, pair_w), 1)
    low = lane < NA_HD

    def nt_dot(a, b):
        return lax.dot_general(a, b, (((1,), (1,)), ((), ())), preferred_element_type=F32)

    def attend(local):
        if local:
            r = (i - n_ctx_blocks) * NA_ROWS_PER_STEP
            start = pl.multiple_of(ctx_len + _band_start(r, rows) * GRID_W, GRID_W)
            pattern = (r - _band_start(r, rows)) // NA_ROWS_PER_STEP

        def pair_cols(ref, row_slice, p):
            return ref[0, row_slice, p * pair_w:(p + 1) * pair_w]

        def gather(ref, p):
            ctx_part = pair_cols(ref, slice(0, ctx_len), p)
            if not local:
                return ctx_part
            return jnp.concatenate([pair_cols(ref, pl.ds(start, NA_BAND), p), ctx_part], axis=0)

        def scores(h):
            p, hh = divmod(h, 2)
            q2 = pair_cols(q_ref, slice(None), p)
            qm = jnp.where(low if hh == 0 else ~low, q2, jnp.zeros_like(q2))
            s = nt_dot(qm, gather(k_ref, p))
            return s + bm_ref[pattern, h] if local else s

        s_next = scores(0)
        halves = []
        for h in range(NA_HEADS):
            s = s_next
            if h + 1 < NA_HEADS:
                s_next = scores(h + 1)
            e = jnp.exp2((s - jnp.max(s, axis=-1, keepdims=True)).astype(BF16))
            vals = gather(v_ref, h // 2)
            if local:
                vals = jnp.where(low if h % 2 == 0 else ~low, vals, jnp.ones_like(vals))
                o = jnp.dot(e, vals, preferred_element_type=F32)
                halves.append(o / pltpu.roll(o, NA_HD, 1))
            else:
                o = jnp.dot(e, vals, preferred_element_type=F32)
                halves.append(o / jnp.sum(e.astype(F32), axis=-1, keepdims=True))
            if h % 2 == 1:
                p = h // 2
                o_ref[0, :, p * pair_w:(p + 1) * pair_w] = jnp.where(low, *halves).astype(o_ref.dtype)
                halves = []

    @pl.when(i < n_ctx_blocks)
    def _():
        attend(False)

    @pl.when(i >= n_ctx_blocks)
    def _():
        attend(True)


def _natten_call(z, bias, ctx_len):
    b, l, _ = z.shape
    rows = (l - ctx_len) // GRID_W

    def full(col):
        return pl.BlockSpec((1, l, NA_W), lambda bi, i: (bi, 0, col))

    return pl.pallas_call(
        functools.partial(_natten_kernel, ctx_len=ctx_len, rows=rows),
        grid=(b, l // NA_QUERIES),
        in_specs=[pl.BlockSpec((1, NA_QUERIES, NA_W), lambda bi, i: (bi, i, COL_NQ)),
                  full(COL_NK), full(COL_NV),
                  pl.BlockSpec(bias.shape, lambda bi, i: (0, 0, 0, 0), pipeline_mode=pl.Buffered(1))],
        out_specs=pl.BlockSpec((1, NA_QUERIES, NA_W), lambda bi, i: (bi, i, 0)),
        out_shape=jax.ShapeDtypeStruct((b, l, NA_W), BF16),
        compiler_params=_params("arbitrary", "arbitrary"),
    )(z, z, z, bias)


def _bias_kernel(cols_ref, o_ref):
    g = pl.program_id(0)
    for a in range(NA_ROWS_PER_STEP):
        r_rel = NA_ROWS_PER_STEP * g + a
        r0_rel = jnp.clip(r_rel - NA_WIN_R // 2, 0, NA_BAND_ROWS - NA_WIN_R)
        for j in range(NA_BAND_ROWS):
            in_window = (j >= r0_rel) & (j < r0_rel + NA_WIN_R)
            plane = cols_ref[0, jnp.clip(j - r_rel + NA_WIN_R - 1, 0, 2 * NA_WIN_R - 2)]
            o_ref[0, 0, a * GRID_W:(a + 1) * GRID_W, j * GRID_W:(j + 1) * GRID_W] = jnp.where(
                in_window, plane, NEG_BIG)
    o_ref[0, 0, :, NA_BAND:] = jnp.zeros((NA_QUERIES, o_ref.shape[3] - NA_BAND), F32)


def _bias_table(rpb, ctx_len):
    qc = np.arange(GRID_W)[:, None]
    kc = np.arange(GRID_W)[None, :]
    c0 = np.clip(qc - NA_WIN_C // 2, 0, GRID_W - NA_WIN_C)
    col_ok = (kc >= c0) & (kc < c0 + NA_WIN_C)
    n_ci = 2 * NA_WIN_C - 1
    n_ri = 2 * NA_WIN_R - 1
    onehot = (kc - qc + NA_WIN_C - 1)[:, :, None] == np.arange(n_ci)
    cols = jnp.einsum('hrc,qkc->hrqk', rpb.astype(F32), jnp.asarray(onehot, F32),
                      precision=lax.Precision.HIGHEST)
    cols = jnp.where(col_ok[None, None], cols * LOG2E, NEG_BIG)
    n_keys = NA_BAND + ctx_len
    return pl.pallas_call(
        _bias_kernel,
        grid=(NA_PATTERNS, NA_HEADS),
        in_specs=[pl.BlockSpec((1, n_ri, GRID_W, GRID_W), lambda g, h: (h, 0, 0, 0))],
        out_specs=pl.BlockSpec((1, 1, NA_QUERIES, n_keys), lambda g, h: (g, h, 0, 0)),
        out_shape=jax.ShapeDtypeStruct((NA_PATTERNS, NA_HEADS, NA_QUERIES, n_keys), F32),
        compiler_params=_params("arbitrary", "arbitrary"),
    )(cols)


def _rope_tables(ctx_len, s):
    tpos = np.arange(s)
    pos = np.stack([tpos // GRID_W, tpos % GRID_W], axis=-1).astype(np.float32)
    half = NA_HD // 2
    inv = (ROPE_THETA ** (-jnp.arange(0, half, 2, dtype=F32) / half))
    ang = jnp.asarray(pos)[:, :, None] * inv
    cos = jnp.cos(ang)
    sin = jnp.sin(ang)
    cos_h = jnp.concatenate([cos, cos], axis=-1).reshape(s, NA_HD)
    sin_h = jnp.concatenate([-sin, sin], axis=-1).reshape(s, NA_HD)
    cos_t = jnp.tile(cos_h, (1, NA_HEADS))
    sin_t = jnp.tile(sin_h, (1, NA_HEADS))
    cos_t = jnp.concatenate([jnp.ones((ctx_len, NA_W), F32), cos_t], axis=0)
    sin_t = jnp.concatenate([jnp.zeros((ctx_len, NA_W), F32), sin_t], axis=0)
    return cos_t, sin_t


def _merge_kernel(x_ref, mod_ref, ya_ref, yb_ref, cb_ref, u_ref, up_ref, un_ref, ga_ref, gb_ref, gc_ref,
                  cw_ref, wa_ref, wb_ref, wc_ref, wo_ref, o_ref, *, ctx_len, seq_len, tm, d):
    ti = pl.program_id(1)
    row = ti * tm + lax.broadcasted_iota(jnp.int32, (tm, 1), 0)
    local = lax.broadcasted_iota(jnp.int32, (tm, 1), 0)

    u = u_ref[0].astype(F32)
    u_before = up_ref[0, BF16_ROWS - 1:BF16_ROWS, :].astype(F32)
    u_after = un_ref[0, 0:1, :].astype(F32)
    up = jnp.where(local == 0, u_before, pltpu.roll(u, 1, 0))
    un = jnp.where(local == tm - 1, u_after, pltpu.roll(u, tm - 1, 0))
    has_prev = (row != 0) & (row != ctx_len)
    has_next = (row != ctx_len - 1) & (row != seq_len - 1)
    cw = cw_ref[...]
    conv = (jnp.where(has_prev, up, 0.0) * cw[0:1, :] + u * cw[1:2, :]
            + jnp.where(has_next, un, 0.0) * cw[2:3, :])
    yc = (cb_ref[0].astype(F32) * conv).astype(BF16)

    mix = ga_ref[0].astype(F32) * jnp.dot(ya_ref[0], wa_ref[...], preferred_element_type=F32)
    mix = mix + gb_ref[0].astype(F32) * jnp.dot(yb_ref[0], wb_ref[...], preferred_element_type=F32)
    mix = mix + gc_ref[0].astype(F32) * jnp.dot(yc, wc_ref[...], preferred_element_type=F32)
    out = jnp.dot(mix.astype(BF16), wo_ref[...], preferred_element_type=F32)
    o_ref[0] = _gated_residual(x_ref[0], out, mod_ref, 2, ctx_len)


def _merge_call(xs, mod, ya, yb, z, cw, wa, wb, wc, wo, ctx_len):
    b, l, d = xs.shape
    tm = _token_tile(l, TOKEN_TILE_ROWS)
    halo = tm // BF16_ROWS
    n_halo = l // BF16_ROWS

    def zspec(col, width=UNIT):
        return pl.BlockSpec((1, tm, width), lambda bi, ti: (bi, ti, col * UNIT // width))

    def prev(col):
        return pl.BlockSpec((1, BF16_ROWS, UNIT), lambda bi, ti: (bi, jnp.maximum(ti * halo - 1, 0), col))

    def nxt(col):
        return pl.BlockSpec((1, BF16_ROWS, UNIT),
                            lambda bi, ti: (bi, jnp.minimum((ti + 1) * halo, n_halo - 1), col))

    def const(shape):
        return pl.BlockSpec(shape, lambda bi, ti: tuple(0 for _ in shape))

    tok = pl.BlockSpec((1, tm, d), lambda bi, ti: (bi, ti, 0))
    half = pl.BlockSpec((1, tm, UNIT), lambda bi, ti: (bi, ti, 0))
    return pl.pallas_call(
        functools.partial(_merge_kernel, ctx_len=ctx_len, seq_len=l, tm=tm, d=d),
        grid=(b, l // tm),
        in_specs=[tok, pl.BlockSpec((1, 2, N_MOD * d), lambda bi, ti: (bi, 0, 0)), half, half,
                  zspec(COL_CB), zspec(COL_CU), prev(COL_CU), nxt(COL_CU),
                  zspec(COL_GA, d), zspec(COL_GB, d), zspec(COL_GC, d),
                  const(cw.shape), const(wa.shape), const(wb.shape), const(wc.shape), const(wo.shape)],
        out_specs=tok,
        out_shape=jax.ShapeDtypeStruct((b, l, d), F32),
        compiler_params=_params("arbitrary", "arbitrary"),
    )(xs, mod, ya, yb, z, z, z, z, z, z, z, cw, wa, wb, wc, wo)


def _mlp_kernel(x_ref, mod_ref, nw_ref, w1_ref, w2_ref, o_ref, *, ctx_len, tm, d, ff_chunk):
    x = x_ref[0]
    h = _modulated_norm(x, nw_ref[...], mod_ref, 3, 4, ctx_len).astype(BF16)
    acc = jnp.zeros((tm, d), F32)
    for c in range(w1_ref.shape[1] // ff_chunk):
        a = jnp.dot(h, w1_ref[:, c * ff_chunk:(c + 1) * ff_chunk], preferred_element_type=F32)
        a = jnp.maximum(a, 0.0)
        acc = acc + jnp.dot((a * a).astype(BF16), w2_ref[c * ff_chunk:(c + 1) * ff_chunk, :],
                            preferred_element_type=F32)
    o_ref[0] = _gated_residual(x, acc, mod_ref, 5, ctx_len)


def _mlp_call(xs, mod, nw, w1, w2, ctx_len, latent_only=False):
    b, l, d = xs.shape
    first = ctx_len if latent_only else 0
    tm = _token_tile(l - first, TOKEN_TILE_ROWS)
    if latent_only:
        x_spec = pl.BlockSpec((pl.Element(1), pl.Element(tm), pl.Element(d)),
                              lambda bi, ti: (bi, pl.multiple_of(first + ti * tm, BF16_ROWS), 0))
    else:
        x_spec = pl.BlockSpec((1, tm, d), lambda bi, ti: (bi, ti, 0))
    return pl.pallas_call(
        functools.partial(_mlp_kernel, ctx_len=0 if latent_only else ctx_len, tm=tm, d=d, ff_chunk=1024),
        grid=(b, (l - first) // tm),
        in_specs=[x_spec, pl.BlockSpec((1, 2, N_MOD * d), lambda bi, ti: (bi, 0, 0)),
                  pl.BlockSpec((1, d), lambda bi, ti: (0, 0)),
                  pl.BlockSpec(w1.shape, lambda bi, ti: (0, 0), pipeline_mode=pl.Buffered(1)),
                  pl.BlockSpec(w2.shape, lambda bi, ti: (0, 0), pipeline_mode=pl.Buffered(1))],
        out_specs=pl.BlockSpec((1, tm, d), lambda bi, ti: (bi, ti, 0)),
        out_shape=jax.ShapeDtypeStruct((b, l - first, d), F32),
        compiler_params=_params("arbitrary", "arbitrary"),
    )(xs, mod, nw, w1, w2)


def _reorder_proj(w_in):
    return jnp.concatenate([w_in[..., u * UNIT:(u + 1) * UNIT] for u in W_UNIT_ORDER], axis=-1)


def kernel(x, c, ctx, c_ctx, ada_w, ada_b, norm1_w, norm2_w, w_in, hgrn_lb_logits, hgrn_onorm_w,
           q_norm_w, k_norm_w, natten_rpb, conv_w, w_branch_a, w_branch_b, w_branch_c, w_out,
           mlp_w1, mlp_w2):
    b, s, d = x.shape
    ctx_len = ctx.shape[1]
    depth = ada_w.shape[0]
    assert s % NA_QUERIES == 0 and s // GRID_W >= NA_BAND_ROWS and ctx_len % NA_QUERIES == 0
    assert ctx_len % (HG_CHUNKS_PER_STEP * HG_CHUNK) == 0 and s % (HG_CHUNKS_PER_STEP * HG_CHUNK) == 0
    assert w_in.shape[-1] == PROJ_UNITS * UNIT
    assert ctx_len <= _token_tile(ctx_len + s, PROJ_TILE_ROWS)

    lb_p = jax.nn.softmax(hgrn_lb_logits.astype(F32), axis=1)
    lower_bounds = jnp.cumsum(lb_p, axis=1) - lb_p[:, :1]

    pad_rows = -(b + 1) % SUBLANES
    cvec = jnp.concatenate([c_ctx[None, :], c, jnp.zeros((pad_rows, d), F32)], axis=0)
    mod_all = _ada_call(cvec, ada_w, ada_b)
    mod_sel = jnp.stack([jnp.broadcast_to(mod_all[:, 0:1], (depth, b, N_MOD * d)),
                         mod_all[:, 1:b + 1]], axis=2)

    cos_t, sin_t = _rope_tables(ctx_len, s)
    bd = jnp.asarray(np.kron(np.eye(NA_HEADS), np.ones((NA_HD, NA_HD))), BF16)
    w_in_b = _reorder_proj(w_in).astype(BF16)

    xs = jnp.concatenate([ctx, x], axis=1)
    for l in range(depth):
        mod = mod_sel[l]
        z, lf = _inproj_call(xs, mod, norm1_w[l][None, :], w_in_b[l], cos_t, sin_t,
                             jnp.tile(q_norm_w[l], NA_HEADS)[None, :],
                             jnp.tile(k_norm_w[l], NA_HEADS)[None, :], bd,
                             lower_bounds[:, l].reshape(1, 2 * HG_W), ctx_len)
        o_f = _hgrn_call(z, lf, ctx_len, False)
        ya = _hgrn_call(z, lf, ctx_len, True, o_f, hgrn_onorm_w[l][None, :])
        yb = _natten_call(z, _bias_table(natten_rpb[l], ctx_len), ctx_len)
        xs = _merge_call(xs, mod, ya, yb, z, conv_w[l], w_branch_a[l].astype(BF16),
                         w_branch_b[l].astype(BF16), w_branch_c[l].astype(BF16),
                         w_out[l].astype(BF16), ctx_len)
        xs = _mlp_call(xs, mod, norm2_w[l][None, :], mlp_w1[l].astype(BF16),
                       mlp_w2[l].astype(BF16), ctx_len, latent_only=l == depth - 1)
    return xs
```

```python
import functools

import numpy as np
import jax
import jax.numpy as jnp
from jax import lax
from jax.experimental import pallas as pl
from jax.experimental.pallas import tpu as pltpu

GRID_W = 64
HG_HEADS = 4
HG_DK = 128
HG_W = HG_HEADS * HG_DK
HG_CHUNK = 64
HG_LEVELS = 6
HG_CHUNKS_PER_STEP = 4
HG_SEQS_PER_STEP = 2
NA_HEADS = 8
NA_HD = 64
NA_W = NA_HEADS * NA_HD
NA_WIN_R = 8
NA_WIN_C = 16
ROPE_THETA = 10000.0
SC_W = 512
N_MOD = 6
EPS = 1e-6
NEG_BIG = -1e30
LOG2E = 1.4426950408889634

LANES = 128
SUBLANES = 8
BF16_ROWS = 16
VMEM_LIMIT = 56 * 1024 * 1024

UNIT = 512
PROJ_UNITS = 17
W_UNIT_ORDER = (11, 12, 13, 14, 15, 16, 0, 4, 5, 6, 1, 2, 9, 10, 3, 7, 8)
WU_GATES, WU_SILU, WU_QK, WU_FORGET, WU_CONV, WU_RAW, WU_CB = 0, 6, 8, 10, 12, 14, 16
COL_GA, COL_GB, COL_GC = 0, 2, 4
COL_HQ, COL_HG = 6, 7
COL_NQ, COL_NK = 8, 9
COL_HI, COL_NV = 10, 11
COL_CB, COL_CU = 12, 13
Z_UNITS = 14

F32 = jnp.float32
BF16 = jnp.bfloat16


def _pick_tile(n, target, mult):
    best = None
    for t in range(mult, min(n, target) + 1, mult):
        if n % t == 0:
            best = t
    if best is None:
        raise ValueError(f"no tile for {n} (target {target}, multiple {mult})")
    return best


PROJ_TILE_ROWS = 544
TOKEN_TILE_ROWS = 1088


def _token_tile(l, rows):
    return _pick_tile(l, rows, BF16_ROWS)


def _sigmoid(x):
    return 1.0 / (1.0 + jnp.exp(-x))


def _gate_sigmoid(x):
    return 0.5 * jnp.tanh(0.5 * x) + 0.5


def _params(*sem):
    return pltpu.CompilerParams(dimension_semantics=sem, vmem_limit_bytes=VMEM_LIMIT)


def _ada_kernel(c_ref, w_ref, b_ref, o_ref):
    c = c_ref[...]
    s = c * _sigmoid(c)
    o_ref[0] = jnp.dot(s, w_ref[0], precision=lax.Precision.HIGHEST,
                       preferred_element_type=F32) + b_ref[0]


def _ada_call(cvec, ada_w, ada_b):
    depth, d, n = ada_w.shape
    rows = cvec.shape[0]
    tn = _pick_tile(n, 1024, LANES)
    return pl.pallas_call(
        _ada_kernel,
        grid=(depth, n // tn),
        in_specs=[
            pl.BlockSpec((rows, d), lambda l, j: (0, 0)),
            pl.BlockSpec((1, d, tn), lambda l, j: (l, 0, j)),
            pl.BlockSpec((1, 1, tn), lambda l, j: (l, 0, j)),
        ],
        out_specs=pl.BlockSpec((1, rows, tn), lambda l, j: (l, 0, j)),
        out_shape=jax.ShapeDtypeStruct((depth, rows, n), F32),
        compiler_params=_params("arbitrary", "arbitrary"),
    )(cvec, ada_w, ada_b.reshape(depth, 1, n))


def _segment_mod(mod_ref, idx, d):
    mc = mod_ref[0, 0:1, idx * d:(idx + 1) * d]
    ml = mod_ref[0, 1:2, idx * d:(idx + 1) * d]
    return jnp.where(pl.program_id(1) == 0, mc, ml), ml


def _by_segment(x, ctx_len, head_fn, rest_fn):
    if ctx_len == x.shape[0]:
        return head_fn(x)
    if ctx_len == 0:
        return rest_fn(x)
    return jnp.concatenate([head_fn(x[:ctx_len]), rest_fn(x[ctx_len:])], axis=0)


def _modulated_norm(x, nw, mod_ref, i_shift, i_scale, ctx_len):
    d = x.shape[-1]
    ms = jnp.mean(x * x, axis=-1, keepdims=True)
    xn = x * lax.rsqrt(ms + EPS)
    shifts = _segment_mod(mod_ref, i_shift, d)
    gains = [nw * (1.0 + s) for s in _segment_mod(mod_ref, i_scale, d)]
    return _by_segment(xn, ctx_len, lambda a: a * gains[0] + shifts[0], lambda a: a * gains[1] + shifts[1])


def _gated_residual(x, y, mod_ref, i_gate, ctx_len):
    gates = _segment_mod(mod_ref, i_gate, x.shape[-1])
    return x + _by_segment(y, ctx_len, lambda a: a * gates[0], lambda a: a * gates[1])


def _rotate(x, cos, sin_signed, first_half):
    w = x.shape[-1]
    quarter = NA_HD // 4
    partner = jnp.where(first_half, pltpu.roll(x, w - quarter, 1), pltpu.roll(x, quarter, 1))
    return x * cos + partner * sin_signed


def _inproj_kernel(x_ref, mod_ref, nw_ref, w_ref, cos_ref, sin_ref, qw_ref, kw_ref, bd_ref, lb_ref,
                   z_ref, lf_ref, *, ctx_len):
    xn = _modulated_norm(x_ref[0], nw_ref[...], mod_ref, 0, 1, ctx_len).astype(BF16)
    lane = lax.broadcasted_iota(jnp.int32, (1, NA_W), 1)
    first_half = (lane % (NA_HD // 2)) < (NA_HD // 4)

    def project(u0, n_units):
        return jnp.dot(xn, w_ref[:, u0 * UNIT:(u0 + n_units) * UNIT], preferred_element_type=F32)

    def put(col, val):
        z_ref[0, :, col * UNIT:col * UNIT + val.shape[1]] = val.astype(z_ref.dtype)

    def head_prep(y, sq, w):
        ms = jnp.dot(sq, bd_ref[...], preferred_element_type=F32) * (1.0 / NA_HD)
        return _rotate(y * lax.rsqrt(ms + EPS) * w, cos_ref[...], sin_ref[...], first_half)

    def finish_forget(r):
        lb = lb_ref[...]
        lf_ref[0] = jnp.log(lb + (1.0 - lb) * _sigmoid(r))

    r_qk = project(WU_QK, 2)
    r_forget = project(WU_FORGET, 2)
    sq = (r_qk * r_qk).astype(BF16)
    r_silu = project(WU_SILU, 2)
    finish_forget(r_forget)
    q_ready = head_prep(r_qk[:, :UNIT], sq[:, :UNIT], qw_ref[...]) * (NA_HD ** -0.5 * LOG2E)
    k_ready = head_prep(r_qk[:, UNIT:], sq[:, UNIT:], kw_ref[...])
    r = project(WU_GATES, 2)
    put(COL_NQ, q_ready)
    put(COL_NK, k_ready)
    put(COL_HQ, r_silu * _gate_sigmoid(r_silu))
    plan = [(WU_GATES + 2, 2, lambda g: put(COL_GA, _gate_sigmoid(g))),
            (WU_GATES + 4, 2, lambda g: put(COL_GB, _gate_sigmoid(g))),
            (WU_CONV, 2, lambda g: put(COL_GC, _gate_sigmoid(g))),
            (WU_RAW, 2, lambda cx: put(COL_CU, cx[:, :UNIT] * cx[:, UNIT:])),
            (WU_CB, 1, lambda raw: put(COL_HI, raw))]
    for u0, n_units, finish_previous in plan:
        r_next = project(u0, n_units)
        finish_previous(r)
        r = r_next
    put(COL_CB, r)


def _inproj_call(xs, mod, nw, w, cos, sin, qw, kw, bd, lb, ctx_len):
    b, l, d = xs.shape
    tm = _token_tile(l, PROJ_TILE_ROWS)

    def const(shape):
        return pl.BlockSpec(shape, lambda bi, ti: (0, 0))

    tab = pl.BlockSpec((tm, NA_W), lambda bi, ti: (ti, 0))
    return pl.pallas_call(
        functools.partial(_inproj_kernel, ctx_len=ctx_len),
        grid=(b, l // tm),
        in_specs=[
            pl.BlockSpec((1, tm, d), lambda bi, ti: (bi, ti, 0)),
            pl.BlockSpec((1, 2, N_MOD * d), lambda bi, ti: (bi, 0, 0)),
            const((1, d)),
            pl.BlockSpec(w.shape, lambda bi, ti: (0, 0), pipeline_mode=pl.Buffered(1)),
            tab, tab, const((1, NA_W)), const((1, NA_W)), const((NA_W, NA_W)), const((1, 2 * HG_W)),
        ],
        out_specs=[pl.BlockSpec((1, tm, Z_UNITS * UNIT), lambda bi, ti: (bi, ti, 0)),
                   pl.BlockSpec((1, tm, 2 * HG_W), lambda bi, ti: (bi, ti, 0))],
        out_shape=[jax.ShapeDtypeStruct((b, l, Z_UNITS * UNIT), BF16),
                   jax.ShapeDtypeStruct((b, l, 2 * HG_W), F32)],
        compiler_params=_params("arbitrary", "arbitrary"),
    )(xs, mod, nw, w, cos, sin, qw, kw, bd, lb)


def _boundary_rows(gc, h, reverse):
    t, w = gc.shape
    off = h if reverse else h - 1
    if 2 * h >= SUBLANES:
        g3 = gc.reshape(t // (2 * h), 2 * h, w)
        return jnp.broadcast_to(g3[:, off:off + 1, :], g3.shape).reshape(t, w)
    g3 = gc.reshape(t // SUBLANES, SUBLANES, w)
    sub = lax.broadcasted_iota(jnp.int32, (1, SUBLANES, 1), 1)
    out = None
    for start in range(0, SUBLANES, 2 * h):
        piece = jnp.broadcast_to(g3[:, start + off:start + off + 1, :], g3.shape)
        out = piece if out is None else jnp.where(sub >= start, piece, out)
    return out.reshape(t, w)


def _scan_cumsum(lf, row, reverse):
    t = lf.shape[0]
    within = row & (SUBLANES - 1)
    g = lf
    for dist in (1, 2, 4):
        if reverse:
            g = g + jnp.where(within < SUBLANES - dist, pltpu.roll(g, t - dist, 0), 0.0)
        else:
            g = g + jnp.where(within >= dist, pltpu.roll(g, dist, 0), 0.0)
    groups = list(range(t // SUBLANES))
    pieces = {}
    carry = None
    for gi in (reversed(groups) if reverse else groups):
        piece = g[gi * SUBLANES:(gi + 1) * SUBLANES]
        if carry is not None:
            piece = piece + carry
        pieces[gi] = piece
        carry = piece[0:1] if reverse else piece[SUBLANES - 1:SUBLANES]
    return jnp.concatenate([pieces[gi] for gi in groups], axis=0)


def _level_operand(qs, kk, gc, h, reverse):
    pieces = []
    for s0 in range(0, gc.shape[0], 2 * h):
        first, second = slice(s0, s0 + h), slice(s0 + h, s0 + 2 * h)
        if reverse:
            gb = gc[s0 + h:s0 + h + 1]
            pieces += [qs[first] * jnp.exp(gc[first] - gb), kk[second] * jnp.exp(gb - gc[second])]
        else:
            gb = gc[s0 + h - 1:s0 + h]
            pieces += [kk[first] * jnp.exp(gb - gc[first]), qs[second] * jnp.exp(gc[second] - gb)]
    return jnp.concatenate(pieces, axis=0).astype(BF16)


def _nt_dot(a, b):
    return lax.dot_general(a, b, (((1,), (1,)), ((), ())), preferred_element_type=F32)


def _hg_head(a, hh):
    return a[:, hh * HG_DK:(hh + 1) * HG_DK]


def _chunk_operands(qs, lf, v, reverse):
    t = HG_CHUNK
    forget = jnp.exp(lf)
    kk = 1.0 - forget
    row = lax.broadcasted_iota(jnp.int32, (t, 1), 0)
    gc = _scan_cumsum(lf, row, reverse)
    qbit = 0 if reverse else 1
    levels = [(qs.astype(BF16), kk.astype(BF16))]
    for lvl in range(HG_LEVELS):
        is_q = ((row >> lvl) & 1) == qbit
        if lvl == 0:
            both = jnp.where(is_q, qs * forget, kk).astype(BF16)
        elif (1 << lvl) >= SUBLANES:
            both = _level_operand(qs, kk, gc, 1 << lvl, reverse)
        else:
            e = jnp.exp(-jnp.abs(gc - _boundary_rows(gc, 1 << lvl, reverse)))
            both = (jnp.where(is_q, qs, kk) * e).astype(BF16)
        levels.append((both, both))
    g_end = gc[0:1, :] if reverse else gc[t - 1:t, :]
    return dict(levels=levels,
                qd=(qs * jnp.exp(gc)).astype(BF16),
                kd=(kk * jnp.exp(g_end - gc)).astype(BF16),
                d_end=jnp.exp(g_end),
                v_b=v.astype(BF16),
                vt=[_hg_head(v, hh).T.astype(BF16) for hh in range(HG_HEADS)])


def _chunk_products(ops):
    scores = [[_nt_dot(_hg_head(a, hh), _hg_head(b, hh)) for hh in range(HG_HEADS)] for a, b in ops["levels"]]
    update = [jnp.dot(ops["vt"][hh], _hg_head(ops["kd"], hh), preferred_element_type=F32)
              for hh in range(HG_HEADS)]
    return scores, update


def _chunk_attn(scores, masks):
    attn = []
    for hh in range(HG_HEADS):
        a = jnp.where(masks[0], scores[0][hh], 0.0)
        for lvl in range(HG_LEVELS):
            a = jnp.where(masks[lvl + 1], scores[lvl + 1][hh], a)
        attn.append(a.astype(BF16))
    return attn


def _chunk_intra(attn, ops):
    return [jnp.dot(attn[hh], _hg_head(ops["v_b"], hh), preferred_element_type=F32) for hh in range(HG_HEADS)]


def _hgrn_block(inputs, masks, reverse, between=None):
    n = len(inputs)
    ops, products, attn, intra = ([None] * n for _ in range(4))
    for stage in range(n + 3):
        if between is not None:
            between(stage)
        if 0 <= stage - 3 < n:
            intra[stage - 3] = _chunk_intra(attn[stage - 3], ops[stage - 3])
        if 0 <= stage - 1 < n:
            products[stage - 1] = _chunk_products(ops[stage - 1])
        if 0 <= stage - 2 < n:
            attn[stage - 2] = _chunk_attn(products[stage - 2][0], masks)
        if stage < n:
            ops[stage] = _chunk_operands(*inputs[stage], reverse)
    return [(intra[c], ops[c]["qd"], ops[c]["d_end"], products[c][1]) for c in range(n)]


def _hgrn_scan(parts, subs, st_ref, emit):
    t = HG_CHUNK
    states = [st_ref[hh] for hh in range(HG_HEADS)]
    for sub, (intra, qd, d_end, update) in zip(subs, parts):
        rows = slice(sub * t, (sub + 1) * t)
        for hh in range(HG_HEADS):
            cols = slice(hh * HG_DK, (hh + 1) * HG_DK)
            o_h = intra[hh] + lax.dot_general(qd[:, cols], states[hh].astype(BF16), (((1,), (1,)), ((), ())),
                                              preferred_element_type=F32)
            states[hh] = states[hh] * d_end[:, cols] + update[hh]
            emit(rows, cols, o_h)
    for hh in range(HG_HEADS):
        st_ref[hh] = states[hh]


def _hgrn_kernel(*refs, reverse, n_sub, n_seq):
    if reverse:
        q_ref, lf_ref, i_ref, hm_ref, of_ref, g_ref, onw_ref, o_ref, st_ref = refs
    else:
        q_ref, lf_ref, i_ref, hm_ref, o_ref, st_ref = refs
    t = HG_CHUNK

    @pl.when(pl.program_id(1) == 0)
    def _():
        st_ref[...] = jnp.zeros_like(st_ref)

    subs = list(range(n_sub - 1, -1, -1) if reverse else range(n_sub))
    masks = [hm_ref[m] != 0.0 for m in range(HG_LEVELS + 1)]
    inputs = []
    for sub in subs:
        rows = slice(sub * t, (sub + 1) * t)
        for bb in range(n_seq):
            inputs.append((q_ref[bb, rows, :].astype(F32), lf_ref[bb, rows, :], i_ref[bb, rows, :].astype(F32)))
    parts = _hgrn_block(inputs, masks, reverse)

    for bb in range(n_seq):
        def emit(rows, cols, o_h, bb=bb):
            if reverse:
                o_h = o_h + of_ref[bb, rows, cols]
                ms = jnp.mean(o_h * o_h, axis=-1, keepdims=True)
                y = o_h * lax.rsqrt(ms + EPS) * onw_ref[...]
                o_ref[bb, rows, cols] = (y * g_ref[bb, rows, cols].astype(F32)).astype(o_ref.dtype)
            else:
                o_ref[bb, rows, cols] = o_h

        _hgrn_scan(parts[bb::n_seq], subs, st_ref.at[bb], emit)


def _hgrn_masks(reverse):
    t = HG_CHUNK
    rt = np.arange(t)[:, None]
    rs = np.arange(t)[None, :]
    qbit = 0 if reverse else 1
    masks = [rt == rs]
    for lvl in range(HG_LEVELS):
        masks.append(((rt >> (lvl + 1)) == (rs >> (lvl + 1))) & (((rt >> lvl) & 1) == qbit)
                     & (((rs >> lvl) & 1) != qbit))
    return jnp.asarray(np.stack(masks), F32)


def _hgrn_call(z, lf, ctx_len, reverse, o_fwd=None, onw=None):
    b, l, _ = z.shape
    t = HG_CHUNK
    rows = HG_CHUNKS_PER_STEP * t
    n_blocks = l // rows
    n_ctx = ctx_len // rows

    if reverse:
        def block(i):
            return jnp.where(i < n_ctx, n_ctx - 1 - i, n_blocks - 1 + n_ctx - i)
    else:
        def block(i):
            return i

    n_seq = HG_SEQS_PER_STEP if b % HG_SEQS_PER_STEP == 0 else 1

    def zspec(col):
        return pl.BlockSpec((n_seq, rows, UNIT), lambda bi, i: (bi, block(i), col))

    row_spec = pl.BlockSpec((n_seq, rows, HG_W), lambda bi, i: (bi, block(i), 0))
    in_specs = [zspec(COL_HQ), zspec(1 if reverse else 0), zspec(COL_HI),
                pl.BlockSpec((HG_LEVELS + 1, t, t), lambda bi, i: (0, 0, 0))]
    args = [z, lf, z, _hgrn_masks(reverse)]
    if reverse:
        in_specs += [row_spec, zspec(COL_HG), pl.BlockSpec((1, HG_DK), lambda bi, i: (0, 0))]
        args += [o_fwd, z, onw]
    return pl.pallas_call(
        functools.partial(_hgrn_kernel, reverse=reverse, n_sub=HG_CHUNKS_PER_STEP, n_seq=n_seq),
        grid=(b // n_seq, n_blocks),
        in_specs=in_specs,
        out_specs=row_spec,
        out_shape=jax.ShapeDtypeStruct((b, l, HG_W), BF16 if reverse else F32),
        scratch_shapes=[pltpu.VMEM((n_seq, HG_HEADS, HG_DK, HG_DK), F32)],
        compiler_params=_params("arbitrary", "arbitrary"),
    )(*args)


NA_ROWS_PER_STEP = 4
NA_QUERIES = NA_ROWS_PER_STEP * GRID_W
NA_BAND_ROWS = NA_WIN_R + NA_ROWS_PER_STEP
NA_BAND = NA_BAND_ROWS * GRID_W
NA_PATTERNS = 3


def _band_start(group_row, rows):
    return jnp.clip(group_row - NA_WIN_R // 2, 0, rows - NA_BAND_ROWS)


def _natten_kernel(q_ref, k_ref, v_ref, bm_ref, o_ref, *, ctx_len, rows):
    i = pl.program_id(1)
    n_ctx_blocks = ctx_len // NA_QUERIES
    pair_w = 2 * NA_HD
    lane = lax.broadcasted_iota(jnp.int32, (1, pair_w), 1)
    low = lane < NA_HD

    def nt_dot(a, b):
        return lax.dot_general(a, b, (((1,), (1,)), ((), ())), preferred_element_type=F32)

    def attend(local):
        if local:
            r = (i - n_ctx_blocks) * NA_ROWS_PER_STEP
            start = pl.multiple_of(ctx_len + _band_start(r, rows) * GRID_W, GRID_W)
            pattern = (r - _band_start(r, rows)) // NA_ROWS_PER_STEP

        def pair_cols(ref, row_slice, p):
            return ref[0, row_slice, p * pair_w:(p + 1) * pair_w]

        def gather(ref, p):
            ctx_part = pair_cols(ref, slice(0, ctx_len), p)
            if not local:
                return ctx_part
            return jnp.concatenate([pair_cols(ref, pl.ds(start, NA_BAND), p), ctx_part], axis=0)

        def scores(h):
            p, hh = divmod(h, 2)
            q2 = pair_cols(q_ref, slice(None), p)
            qm = jnp.where(low if hh == 0 else ~low, q2, jnp.zeros_like(q2))
            s = nt_dot(qm, gather(k_ref, p))
            return s + bm_ref[pattern, h] if local else s

        s_next = scores(0)
        halves = []
        for h in range(NA_HEADS):
            s = s_next
            if h + 1 < NA_HEADS:
                s_next = scores(h + 1)
            e = jnp.exp2((s - jnp.max(s, axis=-1, keepdims=True)).astype(BF16))
            vals = gather(v_ref, h // 2)
            if local:
                vals = jnp.where(low if h % 2 == 0 else ~low, vals, jnp.ones_like(vals))
                o = jnp.dot(e, vals, preferred_element_type=F32)
                halves.append(o / pltpu.roll(o, NA_HD, 1))
            else:
                o = jnp.dot(e, vals, preferred_element_type=F32)
                halves.append(o / jnp.sum(e.astype(F32), axis=-1, keepdims=True))
            if h % 2 == 1:
                p = h // 2
                o_ref[0, :, p * pair_w:(p + 1) * pair_w] = jnp.where(low, *halves).astype(o_ref.dtype)
                halves = []

    @pl.when(i < n_ctx_blocks)
    def _():
        attend(False)

    @pl.when(i >= n_ctx_blocks)
    def _():
        attend(True)


def _natten_call(z, bias, ctx_len):
    b, l, _ = z.shape
    rows = (l - ctx_len) // GRID_W

    def full(col):
        return pl.BlockSpec((1, l, NA_W), lambda bi, i: (bi, 0, col))

    return pl.pallas_call(
        functools.partial(_natten_kernel, ctx_len=ctx_len, rows=rows),
        grid=(b, l // NA_QUERIES),
        in_specs=[pl.BlockSpec((1, NA_QUERIES, NA_W), lambda bi, i: (bi, i, COL_NQ)),
                  full(COL_NK), full(COL_NV),
                  pl.BlockSpec(bias.shape, lambda bi, i: (0, 0, 0, 0), pipeline_mode=pl.Buffered(1))],
        out_specs=pl.BlockSpec((1, NA_QUERIES, NA_W), lambda bi, i: (bi, i, 0)),
        out_shape=jax.ShapeDtypeStruct((b, l, NA_W), BF16),
        compiler_params=_params("arbitrary", "arbitrary"),
    )(z, z, z, bias)


def _bias_kernel(cols_ref, o_ref):
    g = pl.program_id(0)
    for a in range(NA_ROWS_PER_STEP):
        r_rel = NA_ROWS_PER_STEP * g + a
        r0_rel = jnp.clip(r_rel - NA_WIN_R // 2, 0, NA_BAND_ROWS - NA_WIN_R)
        for j in range(NA_BAND_ROWS):
            in_window = (j >= r0_rel) & (j < r0_rel + NA_WIN_R)
            plane = cols_ref[0, jnp.clip(j - r_rel + NA_WIN_R - 1, 0, 2 * NA_WIN_R - 2)]
            o_ref[0, 0, a * GRID_W:(a + 1) * GRID_W, j * GRID_W:(j + 1) * GRID_W] = jnp.where(
                in_window, plane, NEG_BIG)
    o_ref[0, 0, :, NA_BAND:] = jnp.zeros((NA_QUERIES, o_ref.shape[3] - NA_BAND), F32)


def _bias_table(rpb, ctx_len):
    qc = np.arange(GRID_W)[:, None]
    kc = np.arange(GRID_W)[None, :]
    c0 = np.clip(qc - NA_WIN_C // 2, 0, GRID_W - NA_WIN_C)
    col_ok = (kc >= c0) & (kc < c0 + NA_WIN_C)
    n_ci = 2 * NA_WIN_C - 1
    n_ri = 2 * NA_WIN_R - 1
    onehot = (kc - qc + NA_WIN_C - 1)[:, :, None] == np.arange(n_ci)
    cols = jnp.einsum('hrc,qkc->hrqk', rpb.astype(F32), jnp.asarray(onehot, F32),
                      precision=lax.Precision.HIGHEST)
    cols = jnp.where(col_ok[None, None], cols * LOG2E, NEG_BIG)
    n_keys = NA_BAND + ctx_len
    return pl.pallas_call(
        _bias_kernel,
        grid=(NA_PATTERNS, NA_HEADS),
        in_specs=[pl.BlockSpec((1, n_ri, GRID_W, GRID_W), lambda g, h: (h, 0, 0, 0))],
        out_specs=pl.BlockSpec((1, 1, NA_QUERIES, n_keys), lambda g, h: (g, h, 0, 0)),
        out_shape=jax.ShapeDtypeStruct((NA_PATTERNS, NA_HEADS, NA_QUERIES, n_keys), F32),
        compiler_params=_params("arbitrary", "arbitrary"),
    )(cols)


def _rope_tables(ctx_len, s):
    tpos = np.arange(s)
    pos = np.stack([tpos // GRID_W, tpos % GRID_W], axis=-1).astype(np.float32)
    half = NA_HD // 2
    inv = (ROPE_THETA ** (-jnp.arange(0, half, 2, dtype=F32) / half))
    ang = jnp.asarray(pos)[:, :, None] * inv
    cos = jnp.cos(ang)
    sin = jnp.sin(ang)
    cos_h = jnp.concatenate([cos, cos], axis=-1).reshape(s, NA_HD)
    sin_h = jnp.concatenate([-sin, sin], axis=-1).reshape(s, NA_HD)
    cos_t = jnp.tile(cos_h, (1, NA_HEADS))
    sin_t = jnp.tile(sin_h, (1, NA_HEADS))
    cos_t = jnp.concatenate([jnp.ones((ctx_len, NA_W), F32), cos_t], axis=0)
    sin_t = jnp.concatenate([jnp.zeros((ctx_len, NA_W), F32), sin_t], axis=0)
    return cos_t, sin_t


def _merge_kernel(x_ref, mod_ref, ya_ref, yb_ref, cb_ref, u_ref, up_ref, un_ref, ga_ref, gb_ref, gc_ref,
                  cw_ref, wa_ref, wb_ref, wc_ref, wo_ref, o_ref, *, ctx_len, seq_len, tm, d):
    ti = pl.program_id(1)
    row = ti * tm + lax.broadcasted_iota(jnp.int32, (tm, 1), 0)
    local = lax.broadcasted_iota(jnp.int32, (tm, 1), 0)

    u = u_ref[0].astype(F32)
    u_before = up_ref[0, BF16_ROWS - 1:BF16_ROWS, :].astype(F32)
    u_after = un_ref[0, 0:1, :].astype(F32)
    up = jnp.where(local == 0, u_before, pltpu.roll(u, 1, 0))
    un = jnp.where(local == tm - 1, u_after, pltpu.roll(u, tm - 1, 0))
    has_prev = (row != 0) & (row != ctx_len)
    has_next = (row != ctx_len - 1) & (row != seq_len - 1)
    cw = cw_ref[...]
    conv = (jnp.where(has_prev, up, 0.0) * cw[0:1, :] + u * cw[1:2, :]
            + jnp.where(has_next, un, 0.0) * cw[2:3, :])
    yc = (cb_ref[0].astype(F32) * conv).astype(BF16)

    mix = ga_ref[0].astype(F32) * jnp.dot(ya_ref[0], wa_ref[...], preferred_element_type=F32)
    mix = mix + gb_ref[0].astype(F32) * jnp.dot(yb_ref[0], wb_ref[...], preferred_element_type=F32)
    mix = mix + gc_ref[0].astype(F32) * jnp.dot(yc, wc_ref[...], preferred_element_type=F32)
    out = jnp.dot(mix.astype(BF16), wo_ref[...], preferred_element_type=F32)
    o_ref[0] = _gated_residual(x_ref[0], out, mod_ref, 2, ctx_len)


def _merge_call(xs, mod, ya, yb, z, cw, wa, wb, wc, wo, ctx_len):
    b, l, d = xs.shape
    tm = _token_tile(l, TOKEN_TILE_ROWS)
    halo = tm // BF16_ROWS
    n_halo = l // BF16_ROWS

    def zspec(col, width=UNIT):
        return pl.BlockSpec((1, tm, width), lambda bi, ti: (bi, ti, col * UNIT // width))

    def prev(col):
        return pl.BlockSpec((1, BF16_ROWS, UNIT), lambda bi, ti: (bi, jnp.maximum(ti * halo - 1, 0), col))

    def nxt(col):
        return pl.BlockSpec((1, BF16_ROWS, UNIT),
                            lambda bi, ti: (bi, jnp.minimum((ti + 1) * halo, n_halo - 1), col))

    def const(shape):
        return pl.BlockSpec(shape, lambda bi, ti: tuple(0 for _ in shape))

    tok = pl.BlockSpec((1, tm, d), lambda bi, ti: (bi, ti, 0))
    half = pl.BlockSpec((1, tm, UNIT), lambda bi, ti: (bi, ti, 0))
    return pl.pallas_call(
        functools.partial(_merge_kernel, ctx_len=ctx_len, seq_len=l, tm=tm, d=d),
        grid=(b, l // tm),
        in_specs=[tok, pl.BlockSpec((1, 2, N_MOD * d), lambda bi, ti: (bi, 0, 0)), half, half,
                  zspec(COL_CB), zspec(COL_CU), prev(COL_CU), nxt(COL_CU),
                  zspec(COL_GA, d), zspec(COL_GB, d), zspec(COL_GC, d),
                  const(cw.shape), const(wa.shape), const(wb.shape), const(wc.shape), const(wo.shape)],
        out_specs=tok,
        out_shape=jax.ShapeDtypeStruct((b, l, d), F32),
        compiler_params=_params("arbitrary", "arbitrary"),
    )(xs, mod, ya, yb, z, z, z, z, z, z, z, cw, wa, wb, wc, wo)


def _mlp_kernel(x_ref, mod_ref, nw_ref, w1_ref, w2_ref, o_ref, *, ctx_len, tm, d, ff_chunk):
    x = x_ref[0]
    h = _modulated_norm(x, nw_ref[...], mod_ref, 3, 4, ctx_len).astype(BF16)
    acc = jnp.zeros((tm, d), F32)
    for c in range(w1_ref.shape[1] // ff_chunk):
        a = jnp.dot(h, w1_ref[:, c * ff_chunk:(c + 1) * ff_chunk], preferred_element_type=F32)
        a = jnp.maximum(a, 0.0)
        acc = acc + jnp.dot((a * a).astype(BF16), w2_ref[c * ff_chunk:(c + 1) * ff_chunk, :],
                            preferred_element_type=F32)
    o_ref[0] = _gated_residual(x, acc, mod_ref, 5, ctx_len)


def _mlp_call(xs, mod, nw, w1, w2, ctx_len, latent_only=False):
    b, l, d = xs.shape
    first = ctx_len if latent_only else 0
    tm = _token_tile(l - first, TOKEN_TILE_ROWS)
    if latent_only:
        x_spec = pl.BlockSpec((pl.Element(1), pl.Element(tm), pl.Element(d)),
                              lambda bi, ti: (bi, pl.multiple_of(first + ti * tm, BF16_ROWS), 0))
    else:
        x_spec = pl.BlockSpec((1, tm, d), lambda bi, ti: (bi, ti, 0))
    return pl.pallas_call(
        functools.partial(_mlp_kernel, ctx_len=0 if latent_only else ctx_len, tm=tm, d=d, ff_chunk=1024),
        grid=(b, (l - first) // tm),
        in_specs=[x_spec, pl.BlockSpec((1, 2, N_MOD * d), lambda bi, ti: (bi, 0, 0)),
                  pl.BlockSpec((1, d), lambda bi, ti: (0, 0)),
                  pl.BlockSpec(w1.shape, lambda bi, ti: (0, 0), pipeline_mode=pl.Buffered(1)),
                  pl.BlockSpec(w2.shape, lambda bi, ti: (0, 0), pipeline_mode=pl.Buffered(1))],
        out_specs=pl.BlockSpec((1, tm, d), lambda bi, ti: (bi, ti, 0)),
        out_shape=jax.ShapeDtypeStruct((b, l - first, d), F32),
        compiler_params=_params("arbitrary", "arbitrary"),
    )(xs, mod, nw, w1, w2)


def _reorder_proj(w_in):
    return jnp.concatenate([w_in[..., u * UNIT:(u + 1) * UNIT] for u in W_UNIT_ORDER], axis=-1)


def kernel(x, c, ctx, c_ctx, ada_w, ada_b, norm1_w, norm2_w, w_in, hgrn_lb_logits, hgrn_onorm_w,
           q_norm_w, k_norm_w, natten_rpb, conv_w, w_branch_a, w_branch_b, w_branch_c, w_out,
           mlp_w1, mlp_w2):
    b, s, d = x.shape
    ctx_len = ctx.shape[1]
    depth = ada_w.shape[0]
    assert s % NA_QUERIES == 0 and s // GRID_W >= NA_BAND_ROWS and ctx_len % NA_QUERIES == 0
    assert ctx_len % (HG_CHUNKS_PER_STEP * HG_CHUNK) == 0 and s % (HG_CHUNKS_PER_STEP * HG_CHUNK) == 0
    assert w_in.shape[-1] == PROJ_UNITS * UNIT
    assert ctx_len <= _token_tile(ctx_len + s, PROJ_TILE_ROWS)

    lb_p = jax.nn.softmax(hgrn_lb_logits.astype(F32), axis=1)
    lower_bounds = jnp.cumsum(lb_p, axis=1) - lb_p[:, :1]

    pad_rows = -(b + 1) % SUBLANES
    cvec = jnp.concatenate([c_ctx[None, :], c, jnp.zeros((pad_rows, d), F32)], axis=0)
    mod_all = _ada_call(cvec, ada_w, ada_b)
    mod_sel = jnp.stack([jnp.broadcast_to(mod_all[:, 0:1], (depth, b, N_MOD * d)),
                         mod_all[:, 1:b + 1]], axis=2)

    cos_t, sin_t = _rope_tables(ctx_len, s)
    bd = jnp.asarray(np.kron(np.eye(NA_HEADS), np.ones((NA_HD, NA_HD))), BF16)
    w_in_b = _reorder_proj(w_in).astype(BF16)

    xs = jnp.concatenate([ctx, x], axis=1)
    for l in range(depth):
        mod = mod_sel[l]
        z, lf = _inproj_call(xs, mod, norm1_w[l][None, :], w_in_b[l], cos_t, sin_t,
                             jnp.tile(q_norm_w[l], NA_HEADS)[None, :],
                             jnp.tile(k_norm_w[l], NA_HEADS)[None, :], bd,
                             lower_bounds[:, l].reshape(1, 2 * HG_W), ctx_len)
        o_f = _hgrn_call(z, lf, ctx_len, False)
        ya = _hgrn_call(z, lf, ctx_len, True, o_f, hgrn_onorm_w[l][None, :])
        yb = _natten_call(z, _bias_table(natten_rpb[l], ctx_len), ctx_len)
        xs = _merge_call(xs, mod, ya, yb, z, conv_w[l], w_branch_a[l].astype(BF16),
                         w_branch_b[l].astype(BF16), w_branch_c[l].astype(BF16),
                         w_out[l].astype(BF16), ctx_len)
        xs = _mlp_call(xs, mod, norm2_w[l][None, :], mlp_w1[l].astype(BF16),
                       mlp_w2[l].astype(BF16), ctx_len, latent_only=l == depth - 1)
    return xs
```

```python
import functools

import numpy as np
import jax
import jax.numpy as jnp
from jax import lax
from jax.experimental import pallas as pl
from jax.experimental.pallas import tpu as pltpu

GRID_W = 64
HG_HEADS = 4
HG_DK = 128
HG_W = HG_HEADS * HG_DK
HG_CHUNK = 64
HG_LEVELS = 6
HG_CHUNKS_PER_STEP = 4
HG_SEQS_PER_STEP = 4
NA_HEADS = 8
NA_HD = 64
NA_W = NA_HEADS * NA_HD
NA_WIN_R = 8
NA_WIN_C = 16
ROPE_THETA = 10000.0
N_MOD = 6
EPS = 1e-6
NEG_BIG = -1e30
LOG2E = 1.4426950408889634

LANES = 128
SUBLANES = 8
BF16_ROWS = 16
VMEM_LIMIT = 56 * 1024 * 1024

UNIT = 512
PROJ_UNITS = 17
W_UNIT_ORDER = (11, 12, 13, 14, 15, 16, 0, 4, 5, 6, 1, 2, 9, 10, 3, 7, 8)
WU_GATES, WU_SILU, WU_QK, WU_FORGET, WU_CONV, WU_RAW, WU_CB = 0, 6, 8, 10, 12, 14, 16
COL_GA, COL_GB, COL_GC = 0, 2, 4
COL_HQ, COL_HG = 6, 7
COL_NQ, COL_NK = 8, 9
COL_HI, COL_NV = 10, 11
COL_CB, COL_CU = 12, 13
Z_UNITS = 14

F32 = jnp.float32
BF16 = jnp.bfloat16


def _pick_tile(n, target, mult):
    best = None
    for t in range(mult, min(n, target) + 1, mult):
        if n % t == 0:
            best = t
    if best is None:
        raise ValueError(f"no tile for {n} (target {target}, multiple {mult})")
    return best


PROJ_TILE_ROWS = 544
TOKEN_TILE_ROWS = 1088
MLP_FF_CHUNK = 1024


def _token_tile(l, rows):
    return _pick_tile(l, rows, BF16_ROWS)


def _sigmoid(x):
    return 1.0 / (1.0 + jnp.exp(-x))


def _gate_sigmoid(x):
    return 0.5 * jnp.tanh(0.5 * x) + 0.5


def _params(*sem):
    return pltpu.CompilerParams(dimension_semantics=sem, vmem_limit_bytes=VMEM_LIMIT)


def _ada_kernel(c_ref, w_ref, b_ref, o_ref):
    c = c_ref[...]
    s = c * _sigmoid(c)
    o_ref[0] = jnp.dot(s, w_ref[0], precision=lax.Precision.HIGHEST,
                       preferred_element_type=F32) + b_ref[0]


def _ada_call(cvec, ada_w, ada_b):
    depth, d, n = ada_w.shape
    rows = cvec.shape[0]
    tn = _pick_tile(n, 1024, LANES)
    return pl.pallas_call(
        _ada_kernel,
        grid=(depth, n // tn),
        in_specs=[
            pl.BlockSpec((rows, d), lambda l, j: (0, 0)),
            pl.BlockSpec((1, d, tn), lambda l, j: (l, 0, j)),
            pl.BlockSpec((1, 1, tn), lambda l, j: (l, 0, j)),
        ],
        out_specs=pl.BlockSpec((1, rows, tn), lambda l, j: (l, 0, j)),
        out_shape=jax.ShapeDtypeStruct((depth, rows, n), F32),
        compiler_params=_params("arbitrary", "arbitrary"),
    )(cvec, ada_w, ada_b.reshape(depth, 1, n))


def _segment_mod(mod_ref, idx, d):
    mc = mod_ref[0, 0:1, idx * d:(idx + 1) * d]
    ml = mod_ref[0, 1:2, idx * d:(idx + 1) * d]
    return jnp.where(pl.program_id(1) == 0, mc, ml), ml


def _by_segment(x, ctx_len, head_fn, rest_fn):
    if ctx_len == x.shape[0]:
        return head_fn(x)
    if ctx_len == 0:
        return rest_fn(x)
    return jnp.concatenate([head_fn(x[:ctx_len]), rest_fn(x[ctx_len:])], axis=0)


def _modulated_norm(x, nw, mod_ref, i_shift, i_scale, ctx_len):
    d = x.shape[-1]
    ms = jnp.mean(x * x, axis=-1, keepdims=True)
    xn = x * lax.rsqrt(ms + EPS)
    shifts = _segment_mod(mod_ref, i_shift, d)
    gains = [nw * (1.0 + s) for s in _segment_mod(mod_ref, i_scale, d)]
    return _by_segment(xn, ctx_len, lambda a: a * gains[0] + shifts[0], lambda a: a * gains[1] + shifts[1])


def _gated_residual(x, y, mod_ref, i_gate, ctx_len):
    gates = _segment_mod(mod_ref, i_gate, x.shape[-1])
    return x + _by_segment(y, ctx_len, lambda a: a * gates[0], lambda a: a * gates[1])


def _rotate(x, cos, sin_signed, first_half):
    w = x.shape[-1]
    quarter = NA_HD // 4
    partner = jnp.where(first_half, pltpu.roll(x, w - quarter, 1), pltpu.roll(x, quarter, 1))
    return x * cos + partner * sin_signed


def _inproj_kernel(x_ref, mod_ref, nw_ref, w_ref, cos_ref, sin_ref, qw_ref, kw_ref, bd_ref, lb_ref,
                   z_ref, lf_ref, *, ctx_len):
    xn = _modulated_norm(x_ref[0], nw_ref[...], mod_ref, 0, 1, ctx_len).astype(BF16)
    lane = lax.broadcasted_iota(jnp.int32, (1, NA_W), 1)
    first_half = (lane % (NA_HD // 2)) < (NA_HD // 4)

    def project(u0, n_units):
        return jnp.dot(xn, w_ref[:, u0 * UNIT:(u0 + n_units) * UNIT], preferred_element_type=F32)

    def put(col, val):
        z_ref[0, :, col * UNIT:col * UNIT + val.shape[1]] = val.astype(z_ref.dtype)

    def head_prep(y, sq, w):
        ms = jnp.dot(sq, bd_ref[...], preferred_element_type=F32) * (1.0 / NA_HD)
        return _rotate(y * lax.rsqrt(ms + EPS) * w, cos_ref[...], sin_ref[...], first_half)

    def finish_forget(r):
        lb = lb_ref[...]
        lf_ref[0] = jnp.log(lb + (1.0 - lb) * _sigmoid(r))

    r_qk = project(WU_QK, 2)
    r_forget = project(WU_FORGET, 2)
    sq = (r_qk * r_qk).astype(BF16)
    r_silu = project(WU_SILU, 2)
    finish_forget(r_forget)
    q_ready = head_prep(r_qk[:, :UNIT], sq[:, :UNIT], qw_ref[...]) * (NA_HD ** -0.5 * LOG2E)
    k_ready = head_prep(r_qk[:, UNIT:], sq[:, UNIT:], kw_ref[...])
    r = project(WU_GATES, 2)
    put(COL_NQ, q_ready)
    put(COL_NK, k_ready)
    put(COL_HQ, r_silu * _gate_sigmoid(r_silu))
    plan = [(WU_GATES + 2, 2, lambda g: put(COL_GA, _gate_sigmoid(g))),
            (WU_GATES + 4, 2, lambda g: put(COL_GB, _gate_sigmoid(g))),
            (WU_CONV, 2, lambda g: put(COL_GC, _gate_sigmoid(g))),
            (WU_RAW, 2, lambda cx: put(COL_CU, cx[:, :UNIT] * cx[:, UNIT:])),
            (WU_CB, 1, lambda raw: put(COL_HI, raw))]
    for u0, n_units, finish_previous in plan:
        r_next = project(u0, n_units)
        finish_previous(r)
        r = r_next
    put(COL_CB, r)


def _inproj_call(xs, mod, nw, w, cos, sin, qw, kw, bd, lb, ctx_len):
    b, l, d = xs.shape
    tm = _token_tile(l, PROJ_TILE_ROWS)

    def const(shape):
        return pl.BlockSpec(shape, lambda bi, ti: (0, 0))

    tab = pl.BlockSpec((tm, NA_W), lambda bi, ti: (ti, 0))
    return pl.pallas_call(
        functools.partial(_inproj_kernel, ctx_len=ctx_len),
        grid=(b, l // tm),
        in_specs=[
            pl.BlockSpec((1, tm, d), lambda bi, ti: (bi, ti, 0)),
            pl.BlockSpec((1, 2, N_MOD * d), lambda bi, ti: (bi, 0, 0)),
            const((1, d)),
            pl.BlockSpec(w.shape, lambda bi, ti: (0, 0), pipeline_mode=pl.Buffered(1)),
            tab, tab, const((1, NA_W)), const((1, NA_W)), const((NA_W, NA_W)), const((1, 2 * HG_W)),
        ],
        out_specs=[pl.BlockSpec((1, tm, Z_UNITS * UNIT), lambda bi, ti: (bi, ti, 0)),
                   pl.BlockSpec((1, tm, 2 * HG_W), lambda bi, ti: (bi, ti, 0))],
        out_shape=[jax.ShapeDtypeStruct((b, l, Z_UNITS * UNIT), BF16),
                   jax.ShapeDtypeStruct((b, l, 2 * HG_W), F32)],
        compiler_params=_params("arbitrary", "arbitrary"),
    )(xs, mod, nw, w, cos, sin, qw, kw, bd, lb)


def _boundary_rows(gc, h, reverse):
    t, w = gc.shape
    off = h if reverse else h - 1
    if 2 * h >= SUBLANES:
        g3 = gc.reshape(t // (2 * h), 2 * h, w)
        return jnp.broadcast_to(g3[:, off:off + 1, :], g3.shape).reshape(t, w)
    g3 = gc.reshape(t // SUBLANES, SUBLANES, w)
    sub = lax.broadcasted_iota(jnp.int32, (1, SUBLANES, 1), 1)
    out = None
    for start in range(0, SUBLANES, 2 * h):
        piece = jnp.broadcast_to(g3[:, start + off:start + off + 1, :], g3.shape)
        out = piece if out is None else jnp.where(sub >= start, piece, out)
    return out.reshape(t, w)


def _scan_cumsum(lf, row, reverse):
    t = lf.shape[0]
    within = row & (SUBLANES - 1)
    g = lf
    for dist in (1, 2, 4):
        if reverse:
            g = g + jnp.where(within < SUBLANES - dist, pltpu.roll(g, t - dist, 0), 0.0)
        else:
            g = g + jnp.where(within >= dist, pltpu.roll(g, dist, 0), 0.0)
    groups = list(range(t // SUBLANES))
    pieces = {}
    carry = None
    for gi in (reversed(groups) if reverse else groups):
        piece = g[gi * SUBLANES:(gi + 1) * SUBLANES]
        if carry is not None:
            piece = piece + carry
        pieces[gi] = piece
        carry = piece[0:1] if reverse else piece[SUBLANES - 1:SUBLANES]
    return jnp.concatenate([pieces[gi] for gi in groups], axis=0)


def _level_operand(qs, kk, gc, h, reverse):
    pieces = []
    for s0 in range(0, gc.shape[0], 2 * h):
        first, second = slice(s0, s0 + h), slice(s0 + h, s0 + 2 * h)
        if reverse:
            gb = gc[s0 + h:s0 + h + 1]
            pieces += [qs[first] * jnp.exp(gc[first] - gb), kk[second] * jnp.exp(gb - gc[second])]
        else:
            gb = gc[s0 + h - 1:s0 + h]
            pieces += [kk[first] * jnp.exp(gb - gc[first]), qs[second] * jnp.exp(gc[second] - gb)]
    return jnp.concatenate(pieces, axis=0).astype(BF16)


def _nt_dot(a, b):
    return lax.dot_general(a, b, (((1,), (1,)), ((), ())), preferred_element_type=F32)


def _hg_head(a, hh):
    return a[:, hh * HG_DK:(hh + 1) * HG_DK]


def _chunk_operands(qs, lf, v, reverse):
    t = HG_CHUNK
    forget = jnp.exp(lf)
    kk = 1.0 - forget
    row = lax.broadcasted_iota(jnp.int32, (t, 1), 0)
    gc = _scan_cumsum(lf, row, reverse)
    qbit = 0 if reverse else 1
    levels = [(qs.astype(BF16), kk.astype(BF16))]
    for lvl in range(HG_LEVELS):
        is_q = ((row >> lvl) & 1) == qbit
        if lvl == 0:
            both = jnp.where(is_q, qs * forget, kk).astype(BF16)
        elif (1 << lvl) >= SUBLANES:
            both = _level_operand(qs, kk, gc, 1 << lvl, reverse)
        else:
            e = jnp.exp(-jnp.abs(gc - _boundary_rows(gc, 1 << lvl, reverse)))
            both = (jnp.where(is_q, qs, kk) * e).astype(BF16)
        levels.append((both, both))
    g_end = gc[0:1, :] if reverse else gc[t - 1:t, :]
    return dict(levels=levels,
                qd=(qs * jnp.exp(gc)).astype(BF16),
                kd=(kk * jnp.exp(g_end - gc)).astype(BF16),
                d_end=jnp.exp(g_end),
                v_b=v.astype(BF16),
                vt=[_hg_head(v, hh).T.astype(BF16) for hh in range(HG_HEADS)])


def _chunk_products(ops):
    scores = [[_nt_dot(_hg_head(a, hh), _hg_head(b, hh)) for hh in range(HG_HEADS)] for a, b in ops["levels"]]
    update = [jnp.dot(ops["vt"][hh], _hg_head(ops["kd"], hh), preferred_element_type=F32)
              for hh in range(HG_HEADS)]
    return scores, update


def _chunk_attn(scores, masks):
    attn = []
    for hh in range(HG_HEADS):
        a = jnp.where(masks[0], scores[0][hh], 0.0)
        for lvl in range(HG_LEVELS):
            a = jnp.where(masks[lvl + 1], scores[lvl + 1][hh], a)
        attn.append(a.astype(BF16))
    return attn


def _chunk_intra(attn, ops):
    return [jnp.dot(attn[hh], _hg_head(ops["v_b"], hh), preferred_element_type=F32) for hh in range(HG_HEADS)]


def _hgrn_block(inputs, masks, reverse, between=None):
    n = len(inputs)
    ops, products, attn, intra = ([None] * n for _ in range(4))
    for stage in range(n + 3):
        if between is not None:
            between(stage)
        if 0 <= stage - 3 < n:
            intra[stage - 3] = _chunk_intra(attn[stage - 3], ops[stage - 3])
        if 0 <= stage - 1 < n:
            products[stage - 1] = _chunk_products(ops[stage - 1])
        if 0 <= stage - 2 < n:
            attn[stage - 2] = _chunk_attn(products[stage - 2][0], masks)
        if stage < n:
            ops[stage] = _chunk_operands(*inputs[stage], reverse)
    return [(intra[c], ops[c]["qd"], ops[c]["d_end"], products[c][1]) for c in range(n)]


def _hgrn_scan(parts, subs, st_ref, emit):
    t = HG_CHUNK
    states = [st_ref[hh] for hh in range(HG_HEADS)]
    for sub, (intra, qd, d_end, update) in zip(subs, parts):
        rows = slice(sub * t, (sub + 1) * t)
        for hh in range(HG_HEADS):
            cols = slice(hh * HG_DK, (hh + 1) * HG_DK)
            o_h = intra[hh] + lax.dot_general(qd[:, cols], states[hh].astype(BF16), (((1,), (1,)), ((), ())),
                                              preferred_element_type=F32)
            states[hh] = states[hh] * d_end[:, cols] + update[hh]
            emit(rows, cols, o_h)
    for hh in range(HG_HEADS):
        st_ref[hh] = states[hh]


def _hgrn_kernel(*refs, reverse, n_sub, n_seq):
    if reverse:
        q_ref, lf_ref, i_ref, hm_ref, of_ref, g_ref, onw_ref, o_ref, st_ref = refs
    else:
        q_ref, lf_ref, i_ref, hm_ref, o_ref, st_ref = refs
    t = HG_CHUNK

    @pl.when(pl.program_id(1) == 0)
    def _():
        st_ref[...] = jnp.zeros_like(st_ref)

    subs = list(range(n_sub - 1, -1, -1) if reverse else range(n_sub))
    masks = [hm_ref[m] != 0.0 for m in range(HG_LEVELS + 1)]
    inputs = []
    for sub in subs:
        rows = slice(sub * t, (sub + 1) * t)
        for bb in range(n_seq):
            inputs.append((q_ref[bb, rows, :].astype(F32), lf_ref[bb, rows, :], i_ref[bb, rows, :].astype(F32)))
    parts = _hgrn_block(inputs, masks, reverse)

    for bb in range(n_seq):
        def emit(rows, cols, o_h, bb=bb):
            if reverse:
                o_h = o_h + of_ref[bb, rows, cols]
                ms = jnp.mean(o_h * o_h, axis=-1, keepdims=True)
                y = o_h * lax.rsqrt(ms + EPS) * onw_ref[...]
                o_ref[bb, rows, cols] = (y * g_ref[bb, rows, cols].astype(F32)).astype(o_ref.dtype)
            else:
                o_ref[bb, rows, cols] = o_h

        _hgrn_scan(parts[bb::n_seq], subs, st_ref.at[bb], emit)


def _hgrn_masks(reverse):
    t = HG_CHUNK
    rt = np.arange(t)[:, None]
    rs = np.arange(t)[None, :]
    qbit = 0 if reverse else 1
    masks = [rt == rs]
    for lvl in range(HG_LEVELS):
        masks.append(((rt >> (lvl + 1)) == (rs >> (lvl + 1))) & (((rt >> lvl) & 1) == qbit)
                     & (((rs >> lvl) & 1) != qbit))
    return jnp.asarray(np.stack(masks), F32)


def _hgrn_call(z, lf, ctx_len, reverse, o_fwd=None, onw=None):
    b, l, _ = z.shape
    t = HG_CHUNK
    rows = HG_CHUNKS_PER_STEP * t
    n_blocks = l // rows
    n_ctx = ctx_len // rows

    if reverse:
        def block(i):
            return jnp.where(i < n_ctx, n_ctx - 1 - i, n_blocks - 1 + n_ctx - i)
    else:
        def block(i):
            return i

    n_seq = HG_SEQS_PER_STEP if b % HG_SEQS_PER_STEP == 0 else 1

    def zspec(col):
        return pl.BlockSpec((n_seq, rows, UNIT), lambda bi, i: (bi, block(i), col))

    row_spec = pl.BlockSpec((n_seq, rows, HG_W), lambda bi, i: (bi, block(i), 0))
    in_specs = [zspec(COL_HQ), zspec(1 if reverse else 0), zspec(COL_HI),
                pl.BlockSpec((HG_LEVELS + 1, t, t), lambda bi, i: (0, 0, 0))]
    args = [z, lf, z, _hgrn_masks(reverse)]
    if reverse:
        in_specs += [row_spec, zspec(COL_HG), pl.BlockSpec((1, HG_DK), lambda bi, i: (0, 0))]
        args += [o_fwd, z, onw]
    return pl.pallas_call(
        functools.partial(_hgrn_kernel, reverse=reverse, n_sub=HG_CHUNKS_PER_STEP, n_seq=n_seq),
        grid=(b // n_seq, n_blocks),
        in_specs=in_specs,
        out_specs=row_spec,
        out_shape=jax.ShapeDtypeStruct((b, l, HG_W), BF16 if reverse else F32),
        scratch_shapes=[pltpu.VMEM((n_seq, HG_HEADS, HG_DK, HG_DK), F32)],
        compiler_params=_params("arbitrary", "arbitrary"),
    )(*args)


NA_ROWS_PER_STEP = 4
NA_QUERIES = NA_ROWS_PER_STEP * GRID_W
NA_BAND_ROWS = NA_WIN_R + NA_ROWS_PER_STEP
NA_BAND = NA_BAND_ROWS * GRID_W
NA_PATTERNS = 3


def _band_start(group_row, rows):
    return jnp.clip(group_row - NA_WIN_R // 2, 0, rows - NA_BAND_ROWS)


def _natten_kernel(q_ref, k_ref, v_ref, bm_ref, o_ref, *, ctx_len, rows):
    i = pl.program_id(1)
    n_ctx_blocks = ctx_len // NA_QUERIES
    pair_w = 2 * NA_HD
    lane = lax.broadcasted_iota(jnp.int32, (1, pair_w), 1)
    low = lane < NA_HD

    def nt_dot(a, b):
        return lax.dot_general(a, b, (((1,), (1,)), ((), ())), preferred_element_type=F32)

    def attend(local):
        if local:
            r = (i - n_ctx_blocks) * NA_ROWS_PER_STEP
            start = pl.multiple_of(ctx_len + _band_start(r, rows) * GRID_W, GRID_W)
            pattern = (r - _band_start(r, rows)) // NA_ROWS_PER_STEP

        def pair_cols(ref, row_slice, p):
            return ref[0, row_slice, p * pair_w:(p + 1) * pair_w]

        def gather(ref, p):
            ctx_part = pair_cols(ref, slice(0, ctx_len), p)
            if not local:
                return ctx_part
            return jnp.concatenate([pair_cols(ref, pl.ds(start, NA_BAND), p), ctx_part], axis=0)

        def scores(h):
            p, hh = divmod(h, 2)
            q2 = pair_cols(q_ref, slice(None), p)
            qm = jnp.where(low if hh == 0 else ~low, q2, jnp.zeros_like(q2))
            s = nt_dot(qm, gather(k_ref, p))
            return s + bm_ref[pattern, h] if local else s

        s_next = scores(0)
        halves = []
        for h in range(NA_HEADS):
            s = s_next
            if h + 1 < NA_HEADS:
                s_next = scores(h + 1)
            e = jnp.exp2((s - jnp.max(s, axis=-1, keepdims=True)).astype(BF16))
            vals = gather(v_ref, h // 2)
            if local:
                vals = jnp.where(low if h % 2 == 0 else ~low, vals, jnp.ones_like(vals))
                o = jnp.dot(e, vals, preferred_element_type=F32)
                halves.append(o / pltpu.roll(o, NA_HD, 1))
            else:
                o = jnp.dot(e, vals, preferred_element_type=F32)
                halves.append(o / jnp.sum(e.astype(F32), axis=-1, keepdims=True))
            if h % 2 == 1:
                p = h // 2
                o_ref[0, :, p * pair_w:(p + 1) * pair_w] = jnp.where(low, *halves).astype(o_ref.dtype)
                halves = []

    @pl.when(i < n_ctx_blocks)
    def _():
        attend(False)

    @pl.when(i >= n_ctx_blocks)
    def _():
        attend(True)


def _natten_call(z, bias, ctx_len):
    b, l, _ = z.shape
    rows = (l - ctx_len) // GRID_W

    def full(col):
        return pl.BlockSpec((1, l, NA_W), lambda bi, i: (bi, 0, col))

    return pl.pallas_call(
        functools.partial(_natten_kernel, ctx_len=ctx_len, rows=rows),
        grid=(b, l // NA_QUERIES),
        in_specs=[pl.BlockSpec((1, NA_QUERIES, NA_W), lambda bi, i: (bi, i, COL_NQ)),
                  full(COL_NK), full(COL_NV),
                  pl.BlockSpec(bias.shape, lambda bi, i: (0, 0, 0, 0), pipeline_mode=pl.Buffered(1))],
        out_specs=pl.BlockSpec((1, NA_QUERIES, NA_W), lambda bi, i: (bi, i, 0)),
        out_shape=jax.ShapeDtypeStruct((b, l, NA_W), BF16),
        compiler_params=_params("arbitrary", "arbitrary"),
    )(z, z, z, bias)


def _bias_kernel(cols_ref, o_ref):
    g = pl.program_id(0)
    for a in range(NA_ROWS_PER_STEP):
        r_rel = NA_ROWS_PER_STEP * g + a
        r0_rel = jnp.clip(r_rel - NA_WIN_R // 2, 0, NA_BAND_ROWS - NA_WIN_R)
        for j in range(NA_BAND_ROWS):
            in_window = (j >= r0_rel) & (j < r0_rel + NA_WIN_R)
            plane = cols_ref[0, jnp.clip(j - r_rel + NA_WIN_R - 1, 0, 2 * NA_WIN_R - 2)]
            o_ref[0, 0, a * GRID_W:(a + 1) * GRID_W, j * GRID_W:(j + 1) * GRID_W] = jnp.where(
                in_window, plane, NEG_BIG)
    o_ref[0, 0, :, NA_BAND:] = jnp.zeros((NA_QUERIES, o_ref.shape[3] - NA_BAND), F32)


def _bias_table(rpb, ctx_len):
    qc = np.arange(GRID_W)[:, None]
    kc = np.arange(GRID_W)[None, :]
    c0 = np.clip(qc - NA_WIN_C // 2, 0, GRID_W - NA_WIN_C)
    col_ok = (kc >= c0) & (kc < c0 + NA_WIN_C)
    n_ci = 2 * NA_WIN_C - 1
    n_ri = 2 * NA_WIN_R - 1
    onehot = (kc - qc + NA_WIN_C - 1)[:, :, None] == np.arange(n_ci)
    cols = jnp.einsum('hrc,qkc->hrqk', rpb.astype(F32), jnp.asarray(onehot, F32),
                      precision=lax.Precision.HIGHEST)
    cols = jnp.where(col_ok[None, None], cols * LOG2E, NEG_BIG)
    n_keys = NA_BAND + ctx_len
    return pl.pallas_call(
        _bias_kernel,
        grid=(NA_PATTERNS, NA_HEADS),
        in_specs=[pl.BlockSpec((1, n_ri, GRID_W, GRID_W), lambda g, h: (h, 0, 0, 0))],
        out_specs=pl.BlockSpec((1, 1, NA_QUERIES, n_keys), lambda g, h: (g, h, 0, 0)),
        out_shape=jax.ShapeDtypeStruct((NA_PATTERNS, NA_HEADS, NA_QUERIES, n_keys), F32),
        compiler_params=_params("arbitrary", "arbitrary"),
    )(cols)


def _rope_tables(ctx_len, s):
    tpos = np.arange(s)
    pos = np.stack([tpos // GRID_W, tpos % GRID_W], axis=-1).astype(np.float32)
    half = NA_HD // 2
    inv = (ROPE_THETA ** (-jnp.arange(0, half, 2, dtype=F32) / half))
    ang = jnp.asarray(pos)[:, :, None] * inv
    cos = jnp.cos(ang)
    sin = jnp.sin(ang)
    cos_h = jnp.concatenate([cos, cos], axis=-1).reshape(s, NA_HD)
    sin_h = jnp.concatenate([-sin, sin], axis=-1).reshape(s, NA_HD)
    cos_t = jnp.tile(cos_h, (1, NA_HEADS))
    sin_t = jnp.tile(sin_h, (1, NA_HEADS))
    cos_t = jnp.concatenate([jnp.ones((ctx_len, NA_W), F32), cos_t], axis=0)
    sin_t = jnp.concatenate([jnp.zeros((ctx_len, NA_W), F32), sin_t], axis=0)
    return cos_t, sin_t


def _merge_kernel(x_ref, mod_ref, ya_ref, yb_ref, cb_ref, u_ref, up_ref, un_ref, ga_ref, gb_ref, gc_ref,
                  cw_ref, wa_ref, wb_ref, wc_ref, wo_ref, o_ref, *, ctx_len, seq_len, tm):
    ti = pl.program_id(1)
    row = ti * tm + lax.broadcasted_iota(jnp.int32, (tm, 1), 0)
    local = lax.broadcasted_iota(jnp.int32, (tm, 1), 0)

    u = u_ref[0].astype(F32)
    u_before = up_ref[0, BF16_ROWS - 1:BF16_ROWS, :].astype(F32)
    u_after = un_ref[0, 0:1, :].astype(F32)
    up = jnp.where(local == 0, u_before, pltpu.roll(u, 1, 0))
    un = jnp.where(local == tm - 1, u_after, pltpu.roll(u, tm - 1, 0))
    has_prev = (row != 0) & (row != ctx_len)
    has_next = (row != ctx_len - 1) & (row != seq_len - 1)
    cw = cw_ref[...]
    conv = (jnp.where(has_prev, up, 0.0) * cw[0:1, :] + u * cw[1:2, :]
            + jnp.where(has_next, un, 0.0) * cw[2:3, :])
    yc = (cb_ref[0].astype(F32) * conv).astype(BF16)

    mix = ga_ref[0].astype(F32) * jnp.dot(ya_ref[0], wa_ref[...], preferred_element_type=F32)
    mix = mix + gb_ref[0].astype(F32) * jnp.dot(yb_ref[0], wb_ref[...], preferred_element_type=F32)
    mix = mix + gc_ref[0].astype(F32) * jnp.dot(yc, wc_ref[...], preferred_element_type=F32)
    out = jnp.dot(mix.astype(BF16), wo_ref[...], preferred_element_type=F32)
    o_ref[0] = _gated_residual(x_ref[0], out, mod_ref, 2, ctx_len)


def _merge_call(xs, mod, ya, yb, z, cw, wa, wb, wc, wo, ctx_len):
    b, l, d = xs.shape
    tm = _token_tile(l, TOKEN_TILE_ROWS)
    halo = tm // BF16_ROWS
    n_halo = l // BF16_ROWS

    def zspec(col, width=UNIT):
        return pl.BlockSpec((1, tm, width), lambda bi, ti: (bi, ti, col * UNIT // width))

    def prev(col):
        return pl.BlockSpec((1, BF16_ROWS, UNIT), lambda bi, ti: (bi, jnp.maximum(ti * halo - 1, 0), col))

    def nxt(col):
        return pl.BlockSpec((1, BF16_ROWS, UNIT),
                            lambda bi, ti: (bi, jnp.minimum((ti + 1) * halo, n_halo - 1), col))

    def const(shape):
        return pl.BlockSpec(shape, lambda bi, ti: tuple(0 for _ in shape))

    tok = pl.BlockSpec((1, tm, d), lambda bi, ti: (bi, ti, 0))
    half = pl.BlockSpec((1, tm, UNIT), lambda bi, ti: (bi, ti, 0))
    return pl.pallas_call(
        functools.partial(_merge_kernel, ctx_len=ctx_len, seq_len=l, tm=tm),
        grid=(b, l // tm),
        in_specs=[tok, pl.BlockSpec((1, 2, N_MOD * d), lambda bi, ti: (bi, 0, 0)), half, half,
                  zspec(COL_CB), zspec(COL_CU), prev(COL_CU), nxt(COL_CU),
                  zspec(COL_GA, d), zspec(COL_GB, d), zspec(COL_GC, d),
                  const(cw.shape), const(wa.shape), const(wb.shape), const(wc.shape), const(wo.shape)],
        out_specs=tok,
        out_shape=jax.ShapeDtypeStruct((b, l, d), F32),
        compiler_params=_params("arbitrary", "arbitrary"),
    )(xs, mod, ya, yb, z, z, z, z, z, z, z, cw, wa, wb, wc, wo)


def _mlp_kernel(x_ref, mod_ref, nw_ref, w1_ref, w2_ref, o_ref, *, ctx_len):
    x = x_ref[0]
    h = _modulated_norm(x, nw_ref[...], mod_ref, 3, 4, ctx_len).astype(BF16)
    acc = jnp.zeros(x.shape, F32)
    for c in range(w1_ref.shape[1] // MLP_FF_CHUNK):
        cols = slice(c * MLP_FF_CHUNK, (c + 1) * MLP_FF_CHUNK)
        a = jnp.maximum(jnp.dot(h, w1_ref[:, cols], preferred_element_type=F32), 0.0)
        acc = acc + jnp.dot((a * a).astype(BF16), w2_ref[cols, :], preferred_element_type=F32)
    o_ref[0] = _gated_residual(x, acc, mod_ref, 5, ctx_len)


def _mlp_call(xs, mod, nw, w1, w2, ctx_len, latent_only=False):
    b, l, d = xs.shape
    first = ctx_len if latent_only else 0
    tm = _token_tile(l - first, TOKEN_TILE_ROWS)
    if latent_only:
        x_spec = pl.BlockSpec((pl.Element(1), pl.Element(tm), pl.Element(d)),
                              lambda bi, ti: (bi, pl.multiple_of(first + ti * tm, BF16_ROWS), 0))
    else:
        x_spec = pl.BlockSpec((1, tm, d), lambda bi, ti: (bi, ti, 0))
    return pl.pallas_call(
        functools.partial(_mlp_kernel, ctx_len=0 if latent_only else ctx_len),
        grid=(b, (l - first) // tm),
        in_specs=[x_spec, pl.BlockSpec((1, 2, N_MOD * d), lambda bi, ti: (bi, 0, 0)),
                  pl.BlockSpec((1, d), lambda bi, ti: (0, 0)),
                  pl.BlockSpec(w1.shape, lambda bi, ti: (0, 0), pipeline_mode=pl.Buffered(1)),
                  pl.BlockSpec(w2.shape, lambda bi, ti: (0, 0), pipeline_mode=pl.Buffered(1))],
        out_specs=pl.BlockSpec((1, tm, d), lambda bi, ti: (bi, ti, 0)),
        out_shape=jax.ShapeDtypeStruct((b, l - first, d), F32),
        compiler_params=_params("arbitrary", "arbitrary"),
    )(xs, mod, nw, w1, w2)


def _reorder_proj(w_in):
    return jnp.concatenate([w_in[..., u * UNIT:(u + 1) * UNIT] for u in W_UNIT_ORDER], axis=-1)


def kernel(x, c, ctx, c_ctx, ada_w, ada_b, norm1_w, norm2_w, w_in, hgrn_lb_logits, hgrn_onorm_w,
           q_norm_w, k_norm_w, natten_rpb, conv_w, w_branch_a, w_branch_b, w_branch_c, w_out,
           mlp_w1, mlp_w2):
    b, s, d = x.shape
    ctx_len = ctx.shape[1]
    depth = ada_w.shape[0]
    assert s % NA_QUERIES == 0 and s // GRID_W >= NA_BAND_ROWS and ctx_len % NA_QUERIES == 0
    assert ctx_len % (HG_CHUNKS_PER_STEP * HG_CHUNK) == 0 and s % (HG_CHUNKS_PER_STEP * HG_CHUNK) == 0
    assert w_in.shape[-1] == PROJ_UNITS * UNIT
    assert ctx_len <= _token_tile(ctx_len + s, PROJ_TILE_ROWS)

    lb_p = jax.nn.softmax(hgrn_lb_logits.astype(F32), axis=1)
    lower_bounds = jnp.cumsum(lb_p, axis=1) - lb_p[:, :1]

    pad_rows = -(b + 1) % SUBLANES
    cvec = jnp.concatenate([c_ctx[None, :], c, jnp.zeros((pad_rows, d), F32)], axis=0)
    mod_all = _ada_call(cvec, ada_w, ada_b)
    mod_sel = jnp.stack([jnp.broadcast_to(mod_all[:, 0:1], (depth, b, N_MOD * d)),
                         mod_all[:, 1:b + 1]], axis=2)

    cos_t, sin_t = _rope_tables(ctx_len, s)
    bd = jnp.asarray(np.kron(np.eye(NA_HEADS), np.ones((NA_HD, NA_HD))), BF16)
    w_in_b = _reorder_proj(w_in).astype(BF16)

    xs = jnp.concatenate([ctx, x], axis=1)
    for l in range(depth):
        mod = mod_sel[l]
        z, lf = _inproj_call(xs, mod, norm1_w[l][None, :], w_in_b[l], cos_t, sin_t,
                             jnp.tile(q_norm_w[l], NA_HEADS)[None, :],
                             jnp.tile(k_norm_w[l], NA_HEADS)[None, :], bd,
                             lower_bounds[:, l].reshape(1, 2 * HG_W), ctx_len)
        o_f = _hgrn_call(z, lf, ctx_len, False)
        ya = _hgrn_call(z, lf, ctx_len, True, o_f, hgrn_onorm_w[l][None, :])
        yb = _natten_call(z, _bias_table(natten_rpb[l], ctx_len), ctx_len)
        xs = _merge_call(xs, mod, ya, yb, z, conv_w[l], w_branch_a[l].astype(BF16),
                         w_branch_b[l].astype(BF16), w_branch_c[l].astype(BF16),
                         w_out[l].astype(BF16), ctx_len)
        xs = _mlp_call(xs, mod, norm2_w[l][None, :], mlp_w1[l].astype(BF16),
                       mlp_w2[l].astype(BF16), ctx_len, latent_only=l == depth - 1)
    return xs
```

```python
import functools

import numpy as np
import jax
import jax.numpy as jnp
from jax import lax
from jax.experimental import pallas as pl
from jax.experimental.pallas import tpu as pltpu

GRID_W = 64
HG_HEADS = 4
HG_DK = 128
HG_W = HG_HEADS * HG_DK
HG_CHUNK = 64
HG_LEVELS = 6
HG_CHUNKS_PER_STEP = 4
HG_SEQS_PER_STEP = 4
NA_HEADS = 8
NA_HD = 64
NA_W = NA_HEADS * NA_HD
NA_WIN_R = 8
NA_WIN_C = 16
ROPE_THETA = 10000.0
N_MOD = 6
EPS = 1e-6
NEG_BIG = -1e30
LOG2E = 1.4426950408889634

LANES = 128
SUBLANES = 8
BF16_ROWS = 16
VMEM_LIMIT = 56 * 1024 * 1024

UNIT = 512
PROJ_UNITS = 17
WU_HQ, WU_FORGET, WU_HI, WU_HG = 0, 1, 3, 4
WU_QK, WU_NV = 5, 7
WU_CB, WU_CONV = 8, 9
WU_GA, WU_GB, WU_GC = 11, 13, 15
COL_GA, COL_GB, COL_GC = 0, 2, 4
COL_HQ, COL_HG = 6, 7
COL_NQ, COL_NK = 8, 9
COL_HI, COL_NV = 10, 11
COL_CB, COL_CU = 12, 13
Z_UNITS = 14

F32 = jnp.float32
BF16 = jnp.bfloat16


def _pick_tile(n, target, mult):
    best = None
    for t in range(mult, min(n, target) + 1, mult):
        if n % t == 0:
            best = t
    if best is None:
        raise ValueError(f"no tile for {n} (target {target}, multiple {mult})")
    return best


PROJ_TILE_ROWS = 544
TOKEN_TILE_ROWS = 1088
MLP_FF_CHUNK = 1024


def _token_tile(l, rows):
    return _pick_tile(l, rows, BF16_ROWS)


def _sigmoid(x):
    return 1.0 / (1.0 + jnp.exp(-x))


def _gate_sigmoid(x):
    return 0.5 * jnp.tanh(0.5 * x) + 0.5


def _params(*sem):
    return pltpu.CompilerParams(dimension_semantics=sem, vmem_limit_bytes=VMEM_LIMIT)


def _ada_kernel(c_ref, w_ref, b_ref, o_ref):
    c = c_ref[...]
    s = c * _sigmoid(c)
    o_ref[0] = jnp.dot(s, w_ref[0], precision=lax.Precision.HIGHEST,
                       preferred_element_type=F32) + b_ref[0]


def _ada_call(cvec, ada_w, ada_b):
    depth, d, n = ada_w.shape
    rows = cvec.shape[0]
    tn = _pick_tile(n, 1024, LANES)
    return pl.pallas_call(
        _ada_kernel,
        grid=(depth, n // tn),
        in_specs=[
            pl.BlockSpec((rows, d), lambda l, j: (0, 0)),
            pl.BlockSpec((1, d, tn), lambda l, j: (l, 0, j)),
            pl.BlockSpec((1, 1, tn), lambda l, j: (l, 0, j)),
        ],
        out_specs=pl.BlockSpec((1, rows, tn), lambda l, j: (l, 0, j)),
        out_shape=jax.ShapeDtypeStruct((depth, rows, n), F32),
        compiler_params=_params("arbitrary", "arbitrary"),
    )(cvec, ada_w, ada_b.reshape(depth, 1, n))


def _segment_mod(mod_ref, idx, d):
    mc = mod_ref[0, 0:1, idx * d:(idx + 1) * d]
    ml = mod_ref[0, 1:2, idx * d:(idx + 1) * d]
    return jnp.where(pl.program_id(1) == 0, mc, ml), ml


def _by_segment(x, ctx_len, head_fn, rest_fn):
    if ctx_len == x.shape[0]:
        return head_fn(x)
    if ctx_len == 0:
        return rest_fn(x)
    return jnp.concatenate([head_fn(x[:ctx_len]), rest_fn(x[ctx_len:])], axis=0)


def _modulated_norm(x, nw, mod_ref, i_shift, i_scale, ctx_len):
    d = x.shape[-1]
    ms = jnp.mean(x * x, axis=-1, keepdims=True)
    xn = x * lax.rsqrt(ms + EPS)
    shifts = _segment_mod(mod_ref, i_shift, d)
    gains = [nw * (1.0 + s) for s in _segment_mod(mod_ref, i_scale, d)]
    return _by_segment(xn, ctx_len, lambda a: a * gains[0] + shifts[0], lambda a: a * gains[1] + shifts[1])


def _gated_residual(x, y, mod_ref, i_gate, ctx_len):
    gates = _segment_mod(mod_ref, i_gate, x.shape[-1])
    return x + _by_segment(y, ctx_len, lambda a: a * gates[0], lambda a: a * gates[1])


def _rotate(x, cos, sin_signed, first_half):
    w = x.shape[-1]
    quarter = NA_HD // 4
    partner = jnp.where(first_half, pltpu.roll(x, w - quarter, 1), pltpu.roll(x, quarter, 1))
    return x * cos + partner * sin_signed


def _inproj_kernel(x_ref, mod_ref, nw_ref, w_ref, cos_ref, sin_ref, qw_ref, kw_ref, bd_ref, lb_ref,
                   z_ref, lf_ref, *, ctx_len):
    xn = _modulated_norm(x_ref[0], nw_ref[...], mod_ref, 0, 1, ctx_len).astype(BF16)
    lane = lax.broadcasted_iota(jnp.int32, (1, NA_W), 1)
    first_half = (lane % (NA_HD // 2)) < (NA_HD // 4)

    def project(u0, n_units):
        return jnp.dot(xn, w_ref[:, u0 * UNIT:(u0 + n_units) * UNIT], preferred_element_type=F32)

    def put(col, val):
        z_ref[0, :, col * UNIT:col * UNIT + val.shape[1]] = val.astype(z_ref.dtype)

    def head_prep(y, sq, w):
        ms = jnp.dot(sq, bd_ref[...], preferred_element_type=F32) * (1.0 / NA_HD)
        return _rotate(y * lax.rsqrt(ms + EPS) * w, cos_ref[...], sin_ref[...], first_half)

    def finish_forget(r):
        lb = lb_ref[...]
        lf_ref[0] = jnp.log(lb + (1.0 - lb) * _sigmoid(r))

    r_qk = project(WU_QK, 2)
    r_forget = project(WU_FORGET, 2)
    sq = (r_qk * r_qk).astype(BF16)
    r = project(WU_HQ, 1)
    finish_forget(r_forget)
    q_ready = head_prep(r_qk[:, :UNIT], sq[:, :UNIT], qw_ref[...]) * (NA_HD ** -0.5 * LOG2E)
    k_ready = head_prep(r_qk[:, UNIT:], sq[:, UNIT:], kw_ref[...])
    plan = [(WU_HG, 1, lambda q: put(COL_HQ, q * _gate_sigmoid(q))),
            (WU_GA, 2, lambda g: put(COL_HG, g * _gate_sigmoid(g))),
            (WU_GB, 2, lambda g: put(COL_GA, _gate_sigmoid(g))),
            (WU_GC, 2, lambda g: put(COL_GB, _gate_sigmoid(g))),
            (WU_CONV, 2, lambda g: put(COL_GC, _gate_sigmoid(g))),
            (WU_HI, 1, lambda cx: put(COL_CU, cx[:, :UNIT] * cx[:, UNIT:])),
            (WU_NV, 1, lambda raw: put(COL_HI, raw)),
            (WU_CB, 1, lambda raw: put(COL_NV, raw))]
    for n, (u0, n_units, finish_previous) in enumerate(plan):
        r_next = project(u0, n_units)
        finish_previous(r)
        if n == 0:
            put(COL_NQ, q_ready)
            put(COL_NK, k_ready)
        r = r_next
    put(COL_CB, r)


def _inproj_call(xs, mod, nw, w, cos, sin, qw, kw, bd, lb, ctx_len):
    b, l, d = xs.shape
    tm = _token_tile(l, PROJ_TILE_ROWS)

    def const(shape):
        return pl.BlockSpec(shape, lambda bi, ti: (0, 0))

    tab = pl.BlockSpec((tm, NA_W), lambda bi, ti: (ti, 0))
    return pl.pallas_call(
        functools.partial(_inproj_kernel, ctx_len=ctx_len),
        grid=(b, l // tm),
        in_specs=[
            pl.BlockSpec((1, tm, d), lambda bi, ti: (bi, ti, 0)),
            pl.BlockSpec((1, 2, N_MOD * d), lambda bi, ti: (bi, 0, 0)),
            const((1, d)),
            pl.BlockSpec(w.shape, lambda bi, ti: (0, 0), pipeline_mode=pl.Buffered(1)),
            tab, tab, const((1, NA_W)), const((1, NA_W)), const((NA_W, NA_W)), const((1, 2 * HG_W)),
        ],
        out_specs=[pl.BlockSpec((1, tm, Z_UNITS * UNIT), lambda bi, ti: (bi, ti, 0)),
                   pl.BlockSpec((1, tm, 2 * HG_W), lambda bi, ti: (bi, ti, 0))],
        out_shape=[jax.ShapeDtypeStruct((b, l, Z_UNITS * UNIT), BF16),
                   jax.ShapeDtypeStruct((b, l, 2 * HG_W), F32)],
        compiler_params=_params("arbitrary", "arbitrary"),
    )(xs, mod, nw, w, cos, sin, qw, kw, bd, lb)


def _boundary_rows(gc, h, reverse):
    t, w = gc.shape
    off = h if reverse else h - 1
    if 2 * h >= SUBLANES:
        g3 = gc.reshape(t // (2 * h), 2 * h, w)
        return jnp.broadcast_to(g3[:, off:off + 1, :], g3.shape).reshape(t, w)
    g3 = gc.reshape(t // SUBLANES, SUBLANES, w)
    sub = lax.broadcasted_iota(jnp.int32, (1, SUBLANES, 1), 1)
    out = None
    for start in range(0, SUBLANES, 2 * h):
        piece = jnp.broadcast_to(g3[:, start + off:start + off + 1, :], g3.shape)
        out = piece if out is None else jnp.where(sub >= start, piece, out)
    return out.reshape(t, w)


def _scan_cumsum(lf, row, reverse):
    t = lf.shape[0]
    within = row & (SUBLANES - 1)
    g = lf
    for dist in (1, 2, 4):
        if reverse:
            g = g + jnp.where(within < SUBLANES - dist, pltpu.roll(g, t - dist, 0), 0.0)
        else:
            g = g + jnp.where(within >= dist, pltpu.roll(g, dist, 0), 0.0)
    groups = list(range(t // SUBLANES))
    pieces = {}
    carry = None
    for gi in (reversed(groups) if reverse else groups):
        piece = g[gi * SUBLANES:(gi + 1) * SUBLANES]
        if carry is not None:
            piece = piece + carry
        pieces[gi] = piece
        carry = piece[0:1] if reverse else piece[SUBLANES - 1:SUBLANES]
    return jnp.concatenate([pieces[gi] for gi in groups], axis=0)


def _level_operand(qs, kk, gc, h, reverse):
    pieces = []
    for s0 in range(0, gc.shape[0], 2 * h):
        first, second = slice(s0, s0 + h), slice(s0 + h, s0 + 2 * h)
        if reverse:
            gb = gc[s0 + h:s0 + h + 1]
            pieces += [qs[first] * jnp.exp(gc[first] - gb), kk[second] * jnp.exp(gb - gc[second])]
        else:
            gb = gc[s0 + h - 1:s0 + h]
            pieces += [kk[first] * jnp.exp(gb - gc[first]), qs[second] * jnp.exp(gc[second] - gb)]
    return jnp.concatenate(pieces, axis=0).astype(BF16)


def _nt_dot(a, b):
    return lax.dot_general(a, b, (((1,), (1,)), ((), ())), preferred_element_type=F32)


def _hg_head(a, hh):
    return a[:, hh * HG_DK:(hh + 1) * HG_DK]


def _chunk_operands(qs, lf, v, reverse):
    t = HG_CHUNK
    forget = jnp.exp(lf)
    kk = 1.0 - forget
    row = lax.broadcasted_iota(jnp.int32, (t, 1), 0)
    gc = _scan_cumsum(lf, row, reverse)
    qbit = 0 if reverse else 1
    levels = [(qs.astype(BF16), kk.astype(BF16))]
    for lvl in range(HG_LEVELS):
        is_q = ((row >> lvl) & 1) == qbit
        if lvl == 0:
            both = jnp.where(is_q, qs * forget, kk).astype(BF16)
        elif (1 << lvl) >= SUBLANES:
            both = _level_operand(qs, kk, gc, 1 << lvl, reverse)
        else:
            e = jnp.exp(-jnp.abs(gc - _boundary_rows(gc, 1 << lvl, reverse)))
            both = (jnp.where(is_q, qs, kk) * e).astype(BF16)
        levels.append((both, both))
    g_end = gc[0:1, :] if reverse else gc[t - 1:t, :]
    return dict(levels=levels,
                qd=(qs * jnp.exp(gc)).astype(BF16),
                kd=(kk * jnp.exp(g_end - gc)).astype(BF16),
                d_end=jnp.exp(g_end),
                v_b=v.astype(BF16),
                vt=[_hg_head(v, hh).T.astype(BF16) for hh in range(HG_HEADS)])


def _chunk_products(ops):
    scores = [[_nt_dot(_hg_head(a, hh), _hg_head(b, hh)) for hh in range(HG_HEADS)] for a, b in ops["levels"]]
    update = [jnp.dot(ops["vt"][hh], _hg_head(ops["kd"], hh), preferred_element_type=F32)
              for hh in range(HG_HEADS)]
    return scores, update


def _chunk_attn(scores, masks):
    attn = []
    for hh in range(HG_HEADS):
        a = jnp.where(masks[0], scores[0][hh], 0.0)
        for lvl in range(HG_LEVELS):
            a = jnp.where(masks[lvl + 1], scores[lvl + 1][hh], a)
        attn.append(a.astype(BF16))
    return attn


def _chunk_intra(attn, ops):
    return [jnp.dot(attn[hh], _hg_head(ops["v_b"], hh), preferred_element_type=F32) for hh in range(HG_HEADS)]


def _hgrn_block(inputs, masks, reverse, between=None):
    n = len(inputs)
    ops, products, attn, intra = ([None] * n for _ in range(4))
    for stage in range(n + 3):
        if between is not None:
            between(stage)
        if 0 <= stage - 3 < n:
            intra[stage - 3] = _chunk_intra(attn[stage - 3], ops[stage - 3])
        if 0 <= stage - 1 < n:
            products[stage - 1] = _chunk_products(ops[stage - 1])
        if 0 <= stage - 2 < n:
            attn[stage - 2] = _chunk_attn(products[stage - 2][0], masks)
        if stage < n:
            ops[stage] = _chunk_operands(*inputs[stage], reverse)
    return [(intra[c], ops[c]["qd"], ops[c]["d_end"], products[c][1]) for c in range(n)]


def _hgrn_scan(parts, subs, st_ref, emit):
    t = HG_CHUNK
    states = [st_ref[hh] for hh in range(HG_HEADS)]
    for sub, (intra, qd, d_end, update) in zip(subs, parts):
        rows = slice(sub * t, (sub + 1) * t)
        for hh in range(HG_HEADS):
            cols = slice(hh * HG_DK, (hh + 1) * HG_DK)
            o_h = intra[hh] + lax.dot_general(qd[:, cols], states[hh].astype(BF16), (((1,), (1,)), ((), ())),
                                              preferred_element_type=F32)
            states[hh] = states[hh] * d_end[:, cols] + update[hh]
            emit(rows, cols, o_h)
    for hh in range(HG_HEADS):
        st_ref[hh] = states[hh]


def _hgrn_kernel(*refs, reverse, n_sub, n_seq):
    if reverse:
        q_ref, lf_ref, i_ref, hm_ref, of_ref, g_ref, onw_ref, o_ref, st_ref = refs
    else:
        q_ref, lf_ref, i_ref, hm_ref, o_ref, st_ref = refs
    t = HG_CHUNK

    @pl.when(pl.program_id(1) == 0)
    def _():
        st_ref[...] = jnp.zeros_like(st_ref)

    subs = list(range(n_sub - 1, -1, -1) if reverse else range(n_sub))
    masks = [hm_ref[m] != 0.0 for m in range(HG_LEVELS + 1)]
    inputs = []
    for sub in subs:
        rows = slice(sub * t, (sub + 1) * t)
        for bb in range(n_seq):
            inputs.append((q_ref[bb, rows, :].astype(F32), lf_ref[bb, rows, :], i_ref[bb, rows, :].astype(F32)))
    parts = _hgrn_block(inputs, masks, reverse)

    for bb in range(n_seq):
        def emit(rows, cols, o_h, bb=bb):
            if reverse:
                o_h = o_h + of_ref[bb, rows, cols]
                ms = jnp.mean(o_h * o_h, axis=-1, keepdims=True)
                y = o_h * lax.rsqrt(ms + EPS) * onw_ref[...]
                o_ref[bb, rows, cols] = (y * g_ref[bb, rows, cols].astype(F32)).astype(o_ref.dtype)
            else:
                o_ref[bb, rows, cols] = o_h

        _hgrn_scan(parts[bb::n_seq], subs, st_ref.at[bb], emit)


def _hgrn_masks(reverse):
    t = HG_CHUNK
    rt = np.arange(t)[:, None]
    rs = np.arange(t)[None, :]
    qbit = 0 if reverse else 1
    masks = [rt == rs]
    for lvl in range(HG_LEVELS):
        masks.append(((rt >> (lvl + 1)) == (rs >> (lvl + 1))) & (((rt >> lvl) & 1) == qbit)
                     & (((rs >> lvl) & 1) != qbit))
    return jnp.asarray(np.stack(masks), F32)


def _hgrn_call(z, lf, ctx_len, reverse, o_fwd=None, onw=None):
    b, l, _ = z.shape
    t = HG_CHUNK
    rows = HG_CHUNKS_PER_STEP * t
    n_blocks = l // rows
    n_ctx = ctx_len // rows

    if reverse:
        def block(i):
            return jnp.where(i < n_ctx, n_ctx - 1 - i, n_blocks - 1 + n_ctx - i)
    else:
        def block(i):
            return i

    n_seq = HG_SEQS_PER_STEP if b % HG_SEQS_PER_STEP == 0 else 1

    def zspec(col):
        return pl.BlockSpec((n_seq, rows, UNIT), lambda bi, i: (bi, block(i), col))

    row_spec = pl.BlockSpec((n_seq, rows, HG_W), lambda bi, i: (bi, block(i), 0))
    in_specs = [zspec(COL_HQ), zspec(1 if reverse else 0), zspec(COL_HI),
                pl.BlockSpec((HG_LEVELS + 1, t, t), lambda bi, i: (0, 0, 0))]
    args = [z, lf, z, _hgrn_masks(reverse)]
    if reverse:
        in_specs += [row_spec, zspec(COL_HG), pl.BlockSpec((1, HG_DK), lambda bi, i: (0, 0))]
        args += [o_fwd, z, onw]
    return pl.pallas_call(
        functools.partial(_hgrn_kernel, reverse=reverse, n_sub=HG_CHUNKS_PER_STEP, n_seq=n_seq),
        grid=(b // n_seq, n_blocks),
        in_specs=in_specs,
        out_specs=row_spec,
        out_shape=jax.ShapeDtypeStruct((b, l, HG_W), BF16 if reverse else F32),
        scratch_shapes=[pltpu.VMEM((n_seq, HG_HEADS, HG_DK, HG_DK), F32)],
        compiler_params=_params("arbitrary", "arbitrary"),
    )(*args)


NA_ROWS_PER_STEP = 4
NA_QUERIES = NA_ROWS_PER_STEP * GRID_W
NA_BAND_ROWS = NA_WIN_R + NA_ROWS_PER_STEP
NA_BAND = NA_BAND_ROWS * GRID_W
NA_PATTERNS = 3


def _band_start(group_row, rows):
    return jnp.clip(group_row - NA_WIN_R // 2, 0, rows - NA_BAND_ROWS)


def _natten_kernel(q_ref, k_ref, v_ref, bm_ref, o_ref, *, ctx_len, rows):
    i = pl.program_id(1)
    n_ctx_blocks = ctx_len // NA_QUERIES
    pair_w = 2 * NA_HD
    lane = lax.broadcasted_iota(jnp.int32, (1, pair_w), 1)
    low = lane < NA_HD

    def nt_dot(a, b):
        return lax.dot_general(a, b, (((1,), (1,)), ((), ())), preferred_element_type=F32)

    def attend(local):
        if local:
            r = (i - n_ctx_blocks) * NA_ROWS_PER_STEP
            start = pl.multiple_of(ctx_len + _band_start(r, rows) * GRID_W, GRID_W)
            pattern = (r - _band_start(r, rows)) // NA_ROWS_PER_STEP

        def pair_cols(ref, row_slice, p):
            return ref[0, row_slice, p * pair_w:(p + 1) * pair_w]

        def gather(ref, p):
            ctx_part = pair_cols(ref, slice(0, ctx_len), p)
            if not local:
                return ctx_part
            return jnp.concatenate([pair_cols(ref, pl.ds(start, NA_BAND), p), ctx_part], axis=0)

        def scores(h):
            p, hh = divmod(h, 2)
            q2 = pair_cols(q_ref, slice(None), p)
            qm = jnp.where(low if hh == 0 else ~low, q2, jnp.zeros_like(q2))
            s = nt_dot(qm, gather(k_ref, p))
            return s + bm_ref[pattern, h] if local else s

        s_next = scores(0)
        halves = []
        for h in range(NA_HEADS):
            s = s_next
            if h + 1 < NA_HEADS:
                s_next = scores(h + 1)
            e = jnp.exp2((s - jnp.max(s, axis=-1, keepdims=True)).astype(BF16))
            vals = gather(v_ref, h // 2)
            if local:
                vals = jnp.where(low if h % 2 == 0 else ~low, vals, jnp.ones_like(vals))
                o = jnp.dot(e, vals, preferred_element_type=F32)
                halves.append(o / pltpu.roll(o, NA_HD, 1))
            else:
                o = jnp.dot(e, vals, preferred_element_type=F32)
                halves.append(o / jnp.sum(e.astype(F32), axis=-1, keepdims=True))
            if h % 2 == 1:
                p = h // 2
                o_ref[0, :, p * pair_w:(p + 1) * pair_w] = jnp.where(low, *halves).astype(o_ref.dtype)
                halves = []

    @pl.when(i < n_ctx_blocks)
    def _():
        attend(False)

    @pl.when(i >= n_ctx_blocks)
    def _():
        attend(True)


def _natten_call(z, bias, ctx_len):
    b, l, _ = z.shape
    rows = (l - ctx_len) // GRID_W

    def full(col):
        return pl.BlockSpec((1, l, NA_W), lambda bi, i: (bi, 0, col))

    return pl.pallas_call(
        functools.partial(_natten_kernel, ctx_len=ctx_len, rows=rows),
        grid=(b, l // NA_QUERIES),
        in_specs=[pl.BlockSpec((1, NA_QUERIES, NA_W), lambda bi, i: (bi, i, COL_NQ)),
                  full(COL_NK), full(COL_NV),
                  pl.BlockSpec(bias.shape, lambda bi, i: (0, 0, 0, 0), pipeline_mode=pl.Buffered(1))],
        out_specs=pl.BlockSpec((1, NA_QUERIES, NA_W), lambda bi, i: (bi, i, 0)),
        out_shape=jax.ShapeDtypeStruct((b, l, NA_W), BF16),
        compiler_params=_params("arbitrary", "arbitrary"),
    )(z, z, z, bias)


def _bias_kernel(cols_ref, o_ref):
    g = pl.program_id(0)
    for a in range(NA_ROWS_PER_STEP):
        r_rel = NA_ROWS_PER_STEP * g + a
        r0_rel = jnp.clip(r_rel - NA_WIN_R // 2, 0, NA_BAND_ROWS - NA_WIN_R)
        for j in range(NA_BAND_ROWS):
            in_window = (j >= r0_rel) & (j < r0_rel + NA_WIN_R)
            plane = cols_ref[0, jnp.clip(j - r_rel + NA_WIN_R - 1, 0, 2 * NA_WIN_R - 2)]
            o_ref[0, 0, a * GRID_W:(a + 1) * GRID_W, j * GRID_W:(j + 1) * GRID_W] = jnp.where(
                in_window, plane, NEG_BIG)
    o_ref[0, 0, :, NA_BAND:] = jnp.zeros((NA_QUERIES, o_ref.shape[3] - NA_BAND), F32)


def _bias_table(rpb, ctx_len):
    qc = np.arange(GRID_W)[:, None]
    kc = np.arange(GRID_W)[None, :]
    c0 = np.clip(qc - NA_WIN_C // 2, 0, GRID_W - NA_WIN_C)
    col_ok = (kc >= c0) & (kc < c0 + NA_WIN_C)
    n_ci = 2 * NA_WIN_C - 1
    n_ri = 2 * NA_WIN_R - 1
    onehot = (kc - qc + NA_WIN_C - 1)[:, :, None] == np.arange(n_ci)
    cols = jnp.einsum('hrc,qkc->hrqk', rpb.astype(F32), jnp.asarray(onehot, F32),
                      precision=lax.Precision.HIGHEST)
    cols = jnp.where(col_ok[None, None], cols * LOG2E, NEG_BIG)
    n_keys = NA_BAND + ctx_len
    return pl.pallas_call(
        _bias_kernel,
        grid=(NA_PATTERNS, NA_HEADS),
        in_specs=[pl.BlockSpec((1, n_ri, GRID_W, GRID_W), lambda g, h: (h, 0, 0, 0))],
        out_specs=pl.BlockSpec((1, 1, NA_QUERIES, n_keys), lambda g, h: (g, h, 0, 0)),
        out_shape=jax.ShapeDtypeStruct((NA_PATTERNS, NA_HEADS, NA_QUERIES, n_keys), F32),
        compiler_params=_params("arbitrary", "arbitrary"),
    )(cols)


def _rope_tables(ctx_len, s):
    tpos = np.arange(s)
    pos = np.stack([tpos // GRID_W, tpos % GRID_W], axis=-1).astype(np.float32)
    half = NA_HD // 2
    inv = (ROPE_THETA ** (-jnp.arange(0, half, 2, dtype=F32) / half))
    ang = jnp.asarray(pos)[:, :, None] * inv
    cos = jnp.cos(ang)
    sin = jnp.sin(ang)
    cos_h = jnp.concatenate([cos, cos], axis=-1).reshape(s, NA_HD)
    sin_h = jnp.concatenate([-sin, sin], axis=-1).reshape(s, NA_HD)
    cos_t = jnp.tile(cos_h, (1, NA_HEADS))
    sin_t = jnp.tile(sin_h, (1, NA_HEADS))
    cos_t = jnp.concatenate([jnp.ones((ctx_len, NA_W), F32), cos_t], axis=0)
    sin_t = jnp.concatenate([jnp.zeros((ctx_len, NA_W), F32), sin_t], axis=0)
    return cos_t, sin_t


def _merge_kernel(x_ref, mod_ref, ya_ref, yb_ref, cb_ref, u_ref, up_ref, un_ref, ga_ref, gb_ref, gc_ref,
                  cw_ref, wa_ref, wb_ref, wc_ref, wo_ref, o_ref, *, ctx_len, seq_len, tm):
    ti = pl.program_id(1)
    row = ti * tm + lax.broadcasted_iota(jnp.int32, (tm, 1), 0)
    local = lax.broadcasted_iota(jnp.int32, (tm, 1), 0)

    u = u_ref[0].astype(F32)
    u_before = up_ref[0, BF16_ROWS - 1:BF16_ROWS, :].astype(F32)
    u_after = un_ref[0, 0:1, :].astype(F32)
    up = jnp.where(local == 0, u_before, pltpu.roll(u, 1, 0))
    un = jnp.where(local == tm - 1, u_after, pltpu.roll(u, tm - 1, 0))
    has_prev = (row != 0) & (row != ctx_len)
    has_next = (row != ctx_len - 1) & (row != seq_len - 1)
    cw = cw_ref[...]
    conv = (jnp.where(has_prev, up, 0.0) * cw[0:1, :] + u * cw[1:2, :]
            + jnp.where(has_next, un, 0.0) * cw[2:3, :])
    yc = (cb_ref[0].astype(F32) * conv).astype(BF16)

    mix = ga_ref[0].astype(F32) * jnp.dot(ya_ref[0], wa_ref[...], preferred_element_type=F32)
    mix = mix + gb_ref[0].astype(F32) * jnp.dot(yb_ref[0], wb_ref[...], preferred_element_type=F32)
    mix = mix + gc_ref[0].astype(F32) * jnp.dot(yc, wc_ref[...], preferred_element_type=F32)
    out = jnp.dot(mix.astype(BF16), wo_ref[...], preferred_element_type=F32)
    o_ref[0] = _gated_residual(x_ref[0], out, mod_ref, 2, ctx_len)


def _merge_call(xs, mod, ya, yb, z, cw, wa, wb, wc, wo, ctx_len):
    b, l, d = xs.shape
    tm = _token_tile(l, TOKEN_TILE_ROWS)
    halo = tm // BF16_ROWS
    n_halo = l // BF16_ROWS

    def zspec(col, width=UNIT):
        return pl.BlockSpec((1, tm, width), lambda bi, ti: (bi, ti, col * UNIT // width))

    def prev(col):
        return pl.BlockSpec((1, BF16_ROWS, UNIT), lambda bi, ti: (bi, jnp.maximum(ti * halo - 1, 0), col))

    def nxt(col):
        return pl.BlockSpec((1, BF16_ROWS, UNIT),
                            lambda bi, ti: (bi, jnp.minimum((ti + 1) * halo, n_halo - 1), col))

    def const(shape):
        return pl.BlockSpec(shape, lambda bi, ti: tuple(0 for _ in shape))

    tok = pl.BlockSpec((1, tm, d), lambda bi, ti: (bi, ti, 0))
    half = pl.BlockSpec((1, tm, UNIT), lambda bi, ti: (bi, ti, 0))
    return pl.pallas_call(
        functools.partial(_merge_kernel, ctx_len=ctx_len, seq_len=l, tm=tm),
        grid=(b, l // tm),
        in_specs=[tok, pl.BlockSpec((1, 2, N_MOD * d), lambda bi, ti: (bi, 0, 0)), half, half,
                  zspec(COL_CB), zspec(COL_CU), prev(COL_CU), nxt(COL_CU),
                  zspec(COL_GA, d), zspec(COL_GB, d), zspec(COL_GC, d),
                  const(cw.shape), const(wa.shape), const(wb.shape), const(wc.shape), const(wo.shape)],
        out_specs=tok,
        out_shape=jax.ShapeDtypeStruct((b, l, d), F32),
        compiler_params=_params("arbitrary", "arbitrary"),
    )(xs, mod, ya, yb, z, z, z, z, z, z, z, cw, wa, wb, wc, wo)


def _mlp_kernel(x_ref, mod_ref, nw_ref, w1_ref, w2_ref, o_ref, *, ctx_len):
    x = x_ref[0]
    h = _modulated_norm(x, nw_ref[...], mod_ref, 3, 4, ctx_len).astype(BF16)
    acc = jnp.zeros(x.shape, F32)
    for c in range(w1_ref.shape[1] // MLP_FF_CHUNK):
        cols = slice(c * MLP_FF_CHUNK, (c + 1) * MLP_FF_CHUNK)
        a = jnp.maximum(jnp.dot(h, w1_ref[:, cols], preferred_element_type=F32), 0.0)
        acc = acc + jnp.dot((a * a).astype(BF16), w2_ref[cols, :], preferred_element_type=F32)
    o_ref[0] = _gated_residual(x, acc, mod_ref, 5, ctx_len)


def _mlp_call(xs, mod, nw, w1, w2, ctx_len, latent_only=False):
    b, l, d = xs.shape
    first = ctx_len if latent_only else 0
    tm = _token_tile(l - first, TOKEN_TILE_ROWS)
    if latent_only:
        x_spec = pl.BlockSpec((pl.Element(1), pl.Element(tm), pl.Element(d)),
                              lambda bi, ti: (bi, pl.multiple_of(first + ti * tm, BF16_ROWS), 0))
    else:
        x_spec = pl.BlockSpec((1, tm, d), lambda bi, ti: (bi, ti, 0))
    return pl.pallas_call(
        functools.partial(_mlp_kernel, ctx_len=0 if latent_only else ctx_len),
        grid=(b, (l - first) // tm),
        in_specs=[x_spec, pl.BlockSpec((1, 2, N_MOD * d), lambda bi, ti: (bi, 0, 0)),
                  pl.BlockSpec((1, d), lambda bi, ti: (0, 0)),
                  pl.BlockSpec(w1.shape, lambda bi, ti: (0, 0), pipeline_mode=pl.Buffered(1)),
                  pl.BlockSpec(w2.shape, lambda bi, ti: (0, 0), pipeline_mode=pl.Buffered(1))],
        out_specs=pl.BlockSpec((1, tm, d), lambda bi, ti: (bi, ti, 0)),
        out_shape=jax.ShapeDtypeStruct((b, l - first, d), F32),
        compiler_params=_params("arbitrary", "arbitrary"),
    )(xs, mod, nw, w1, w2)


def kernel(x, c, ctx, c_ctx, ada_w, ada_b, norm1_w, norm2_w, w_in, hgrn_lb_logits, hgrn_onorm_w,
           q_norm_w, k_norm_w, natten_rpb, conv_w, w_branch_a, w_branch_b, w_branch_c, w_out,
           mlp_w1, mlp_w2):
    b, s, d = x.shape
    ctx_len = ctx.shape[1]
    depth = ada_w.shape[0]
    assert s % NA_QUERIES == 0 and s // GRID_W >= NA_BAND_ROWS and ctx_len % NA_QUERIES == 0
    assert ctx_len % (HG_CHUNKS_PER_STEP * HG_CHUNK) == 0 and s % (HG_CHUNKS_PER_STEP * HG_CHUNK) == 0
    assert w_in.shape[-1] == PROJ_UNITS * UNIT
    assert ctx_len <= _token_tile(ctx_len + s, PROJ_TILE_ROWS)

    lb_p = jax.nn.softmax(hgrn_lb_logits.astype(F32), axis=1)
    lower_bounds = jnp.cumsum(lb_p, axis=1) - lb_p[:, :1]

    pad_rows = -(b + 1) % SUBLANES
    cvec = jnp.concatenate([c_ctx[None, :], c, jnp.zeros((pad_rows, d), F32)], axis=0)
    mod_all = _ada_call(cvec, ada_w, ada_b)
    mod_sel = jnp.stack([jnp.broadcast_to(mod_all[:, 0:1], (depth, b, N_MOD * d)),
                         mod_all[:, 1:b + 1]], axis=2)

    cos_t, sin_t = _rope_tables(ctx_len, s)
    bd = jnp.asarray(np.kron(np.eye(NA_HEADS), np.ones((NA_HD, NA_HD))), BF16)
    w_in_b = w_in.astype(BF16)

    xs = jnp.concatenate([ctx, x], axis=1)
    for l in range(depth):
        mod = mod_sel[l]
        z, lf = _inproj_call(xs, mod, norm1_w[l][None, :], w_in_b[l], cos_t, sin_t,
                             jnp.tile(q_norm_w[l], NA_HEADS)[None, :],
                             jnp.tile(k_norm_w[l], NA_HEADS)[None, :], bd,
                             lower_bounds[:, l].reshape(1, 2 * HG_W), ctx_len)
        o_f = _hgrn_call(z, lf, ctx_len, False)
        ya = _hgrn_call(z, lf, ctx_len, True, o_f, hgrn_onorm_w[l][None, :])
        yb = _natten_call(z, _bias_table(natten_rpb[l], ctx_len), ctx_len)
        xs = _merge_call(xs, mod, ya, yb, z, conv_w[l], w_branch_a[l].astype(BF16),
                         w_branch_b[l].astype(BF16), w_branch_c[l].astype(BF16),
                         w_out[l].astype(BF16), ctx_len)
        xs = _mlp_call(xs, mod, norm2_w[l][None, :], mlp_w1[l].astype(BF16),
                       mlp_w2[l].astype(BF16), ctx_len, latent_only=l == depth - 1)
    return xs
```

```python
import functools

import numpy as np
import jax
import jax.numpy as jnp
from jax import lax
from jax.experimental import pallas as pl
from jax.experimental.pallas import tpu as pltpu

GRID_W = 64
HG_HEADS = 4
HG_DK = 128
HG_W = HG_HEADS * HG_DK
HG_CHUNK = 64
HG_LEVELS = 6
HG_CHUNKS_PER_STEP = 4
HG_SEQS_PER_STEP = 4
NA_HEADS = 8
NA_HD = 64
NA_W = NA_HEADS * NA_HD
NA_WIN_R = 8
NA_WIN_C = 16
ROPE_THETA = 10000.0
N_MOD = 6
EPS = 1e-6
NEG_BIG = -1e30
LOG2E = 1.4426950408889634

LANES = 128
SUBLANES = 8
BF16_ROWS = 16
VMEM_LIMIT = 56 * 1024 * 1024

UNIT = 512
PROJ_UNITS = 17
WU_HQ, WU_FORGET, WU_HI, WU_HG = 0, 1, 3, 4
WU_QK, WU_NV = 5, 7
WU_CB, WU_CONV = 8, 9
WU_GA, WU_GB, WU_GC = 11, 13, 15
COL_GA, COL_GB, COL_GC = 0, 2, 4
COL_HQ, COL_HG = 6, 7
COL_NQ, COL_NK = 8, 9
COL_HI, COL_NV = 10, 11
COL_CB, COL_CU = 12, 13
Z_UNITS = 14

F32 = jnp.float32
BF16 = jnp.bfloat16


def _pick_tile(n, target, mult):
    best = None
    for t in range(mult, min(n, target) + 1, mult):
        if n % t == 0:
            best = t
    if best is None:
        raise ValueError(f"no tile for {n} (target {target}, multiple {mult})")
    return best


PROJ_TILE_ROWS = 544
TOKEN_TILE_ROWS = 1088
MLP_FF_CHUNK = 1024


def _token_tile(l, rows):
    return _pick_tile(l, rows, BF16_ROWS)


def _sigmoid(x):
    return 1.0 / (1.0 + jnp.exp(-x))


def _gate_sigmoid(x):
    return 0.5 * jnp.tanh(0.5 * x) + 0.5


def _layer_spec(stacked, layer):
    zeros = (0,) * (stacked.ndim - 1)
    return pl.BlockSpec((None,) + stacked.shape[1:], lambda *_: (layer,) + zeros, pipeline_mode=pl.Buffered(1))


def _params(*sem):
    return pltpu.CompilerParams(dimension_semantics=sem, vmem_limit_bytes=VMEM_LIMIT)


def _ada_kernel(c_ref, w_ref, b_ref, o_ref):
    c = c_ref[...]
    s = c * _sigmoid(c)
    o_ref[0] = jnp.dot(s, w_ref[0], precision=lax.Precision.HIGHEST,
                       preferred_element_type=F32) + b_ref[0]


def _ada_call(cvec, ada_w, ada_b):
    depth, d, n = ada_w.shape
    rows = cvec.shape[0]
    tn = _pick_tile(n, 1024, LANES)
    return pl.pallas_call(
        _ada_kernel,
        grid=(depth, n // tn),
        in_specs=[
            pl.BlockSpec((rows, d), lambda l, j: (0, 0)),
            pl.BlockSpec((1, d, tn), lambda l, j: (l, 0, j)),
            pl.BlockSpec((1, 1, tn), lambda l, j: (l, 0, j)),
        ],
        out_specs=pl.BlockSpec((1, rows, tn), lambda l, j: (l, 0, j)),
        out_shape=jax.ShapeDtypeStruct((depth, rows, n), F32),
        compiler_params=_params("arbitrary", "arbitrary"),
    )(cvec, ada_w, ada_b.reshape(depth, 1, n))


def _segment_mod(mod_ref, idx, d):
    mc = mod_ref[0, 0:1, idx * d:(idx + 1) * d]
    ml = mod_ref[0, 1:2, idx * d:(idx + 1) * d]
    return jnp.where(pl.program_id(1) == 0, mc, ml), ml


def _by_segment(x, ctx_len, head_fn, rest_fn):
    if ctx_len == x.shape[0]:
        return head_fn(x)
    if ctx_len == 0:
        return rest_fn(x)
    return jnp.concatenate([head_fn(x[:ctx_len]), rest_fn(x[ctx_len:])], axis=0)


def _modulated_norm(x, nw, mod_ref, i_shift, i_scale, ctx_len):
    d = x.shape[-1]
    ms = jnp.mean(x * x, axis=-1, keepdims=True)
    xn = x * lax.rsqrt(ms + EPS)
    shifts = _segment_mod(mod_ref, i_shift, d)
    gains = [nw * (1.0 + s) for s in _segment_mod(mod_ref, i_scale, d)]
    return _by_segment(xn, ctx_len, lambda a: a * gains[0] + shifts[0], lambda a: a * gains[1] + shifts[1])


def _gated_residual(x, y, mod_ref, i_gate, ctx_len):
    gates = _segment_mod(mod_ref, i_gate, x.shape[-1])
    return x + _by_segment(y, ctx_len, lambda a: a * gates[0], lambda a: a * gates[1])


def _rotate(x, cos, sin_signed, first_half):
    w = x.shape[-1]
    quarter = NA_HD // 4
    partner = jnp.where(first_half, pltpu.roll(x, w - quarter, 1), pltpu.roll(x, quarter, 1))
    return x * cos + partner * sin_signed


def _inproj_kernel(x_ref, mod_ref, nw_ref, w_ref, cos_ref, sin_ref, qw_ref, kw_ref, bd_ref, lb_ref,
                   z_ref, lf_ref, *, ctx_len):
    xn = _modulated_norm(x_ref[0], nw_ref[...], mod_ref, 0, 1, ctx_len).astype(BF16)
    lane = lax.broadcasted_iota(jnp.int32, (1, NA_W), 1)
    first_half = (lane % (NA_HD // 2)) < (NA_HD // 4)

    def project(u0, n_units):
        return jnp.dot(xn, w_ref[:, u0 * UNIT:(u0 + n_units) * UNIT], preferred_element_type=F32)

    def put(col, val):
        z_ref[0, :, col * UNIT:col * UNIT + val.shape[1]] = val.astype(z_ref.dtype)

    def head_prep(y, sq, w):
        ms = jnp.dot(sq, bd_ref[...], preferred_element_type=F32) * (1.0 / NA_HD)
        return _rotate(y * lax.rsqrt(ms + EPS) * w, cos_ref[...], sin_ref[...], first_half)

    def finish_forget(r):
        lb = lb_ref[...]
        lf_ref[0] = jnp.log(lb + (1.0 - lb) * _sigmoid(r))

    r_qk = project(WU_QK, 2)
    r_forget = project(WU_FORGET, 2)
    sq = (r_qk * r_qk).astype(BF16)
    r = project(WU_HQ, 1)
    finish_forget(r_forget)
    q_ready = head_prep(r_qk[:, :UNIT], sq[:, :UNIT], qw_ref[...]) * (NA_HD ** -0.5 * LOG2E)
    k_ready = head_prep(r_qk[:, UNIT:], sq[:, UNIT:], kw_ref[...])
    plan = [(WU_HG, 1, lambda q: put(COL_HQ, q * _gate_sigmoid(q))),
            (WU_GA, 2, lambda g: put(COL_HG, g * _gate_sigmoid(g))),
            (WU_GB, 2, lambda g: put(COL_GA, _gate_sigmoid(g))),
            (WU_GC, 2, lambda g: put(COL_GB, _gate_sigmoid(g))),
            (WU_CONV, 2, lambda g: put(COL_GC, _gate_sigmoid(g))),
            (WU_HI, 1, lambda cx: put(COL_CU, cx[:, :UNIT] * cx[:, UNIT:])),
            (WU_NV, 1, lambda raw: put(COL_HI, raw)),
            (WU_CB, 1, lambda raw: put(COL_NV, raw))]
    for n, (u0, n_units, finish_previous) in enumerate(plan):
        r_next = project(u0, n_units)
        finish_previous(r)
        if n == 0:
            put(COL_NQ, q_ready)
            put(COL_NK, k_ready)
        r = r_next
    put(COL_CB, r)


def _inproj_call(xs, mod, nw, w_all, layer, cos, sin, qw, kw, bd, lb, ctx_len):
    b, l, d = xs.shape
    tm = _token_tile(l, PROJ_TILE_ROWS)

    def const(shape):
        return pl.BlockSpec(shape, lambda bi, ti: (0, 0))

    tab = pl.BlockSpec((tm, NA_W), lambda bi, ti: (ti, 0))
    return pl.pallas_call(
        functools.partial(_inproj_kernel, ctx_len=ctx_len),
        grid=(b, l // tm),
        in_specs=[
            pl.BlockSpec((1, tm, d), lambda bi, ti: (bi, ti, 0)),
            pl.BlockSpec((1, 2, N_MOD * d), lambda bi, ti: (bi, 0, 0)),
            const((1, d)),
            _layer_spec(w_all, layer),
            tab, tab, const((1, NA_W)), const((1, NA_W)), const((NA_W, NA_W)), const((1, 2 * HG_W)),
        ],
        out_specs=[pl.BlockSpec((1, tm, Z_UNITS * UNIT), lambda bi, ti: (bi, ti, 0)),
                   pl.BlockSpec((1, tm, 2 * HG_W), lambda bi, ti: (bi, ti, 0))],
        out_shape=[jax.ShapeDtypeStruct((b, l, Z_UNITS * UNIT), BF16),
                   jax.ShapeDtypeStruct((b, l, 2 * HG_W), F32)],
        compiler_params=_params("arbitrary", "arbitrary"),
    )(xs, mod, nw, w_all, cos, sin, qw, kw, bd, lb)


def _boundary_rows(gc, h, reverse):
    t, w = gc.shape
    off = h if reverse else h - 1
    if 2 * h >= SUBLANES:
        g3 = gc.reshape(t // (2 * h), 2 * h, w)
        return jnp.broadcast_to(g3[:, off:off + 1, :], g3.shape).reshape(t, w)
    g3 = gc.reshape(t // SUBLANES, SUBLANES, w)
    sub = lax.broadcasted_iota(jnp.int32, (1, SUBLANES, 1), 1)
    out = None
    for start in range(0, SUBLANES, 2 * h):
        piece = jnp.broadcast_to(g3[:, start + off:start + off + 1, :], g3.shape)
        out = piece if out is None else jnp.where(sub >= start, piece, out)
    return out.reshape(t, w)


def _scan_cumsum(lf, row, reverse):
    t = lf.shape[0]
    within = row & (SUBLANES - 1)
    g = lf
    for dist in (1, 2, 4):
        if reverse:
            g = g + jnp.where(within < SUBLANES - dist, pltpu.roll(g, t - dist, 0), 0.0)
        else:
            g = g + jnp.where(within >= dist, pltpu.roll(g, dist, 0), 0.0)
    groups = list(range(t // SUBLANES))
    pieces = {}
    carry = None
    for gi in (reversed(groups) if reverse else groups):
        piece = g[gi * SUBLANES:(gi + 1) * SUBLANES]
        if carry is not None:
            piece = piece + carry
        pieces[gi] = piece
        carry = piece[0:1] if reverse else piece[SUBLANES - 1:SUBLANES]
    return jnp.concatenate([pieces[gi] for gi in groups], axis=0)


def _level_operand(qs, kk, gc, h, reverse):
    pieces = []
    for s0 in range(0, gc.shape[0], 2 * h):
        first, second = slice(s0, s0 + h), slice(s0 + h, s0 + 2 * h)
        if reverse:
            gb = gc[s0 + h:s0 + h + 1]
            pieces += [qs[first] * jnp.exp(gc[first] - gb), kk[second] * jnp.exp(gb - gc[second])]
        else:
            gb = gc[s0 + h - 1:s0 + h]
            pieces += [kk[first] * jnp.exp(gb - gc[first]), qs[second] * jnp.exp(gc[second] - gb)]
    return jnp.concatenate(pieces, axis=0).astype(BF16)


def _nt_dot(a, b):
    return lax.dot_general(a, b, (((1,), (1,)), ((), ())), preferred_element_type=F32)


def _hg_head(a, hh):
    return a[:, hh * HG_DK:(hh + 1) * HG_DK]


def _chunk_operands(qs, lf, v, reverse):
    t = HG_CHUNK
    forget = jnp.exp(lf)
    kk = 1.0 - forget
    row = lax.broadcasted_iota(jnp.int32, (t, 1), 0)
    gc = _scan_cumsum(lf, row, reverse)
    qbit = 0 if reverse else 1
    levels = [(qs.astype(BF16), kk.astype(BF16))]
    for lvl in range(HG_LEVELS):
        is_q = ((row >> lvl) & 1) == qbit
        if lvl == 0:
            both = jnp.where(is_q, qs * forget, kk).astype(BF16)
        elif (1 << lvl) >= SUBLANES:
            both = _level_operand(qs, kk, gc, 1 << lvl, reverse)
        else:
            e = jnp.exp(-jnp.abs(gc - _boundary_rows(gc, 1 << lvl, reverse)))
            both = (jnp.where(is_q, qs, kk) * e).astype(BF16)
        levels.append((both, both))
    g_end = gc[0:1, :] if reverse else gc[t - 1:t, :]
    return dict(levels=levels,
                qd=(qs * jnp.exp(gc)).astype(BF16),
                kd=(kk * jnp.exp(g_end - gc)).astype(BF16),
                d_end=jnp.exp(g_end),
                v_b=v.astype(BF16),
                vt=[_hg_head(v, hh).T.astype(BF16) for hh in range(HG_HEADS)])


def _chunk_products(ops):
    scores = [[_nt_dot(_hg_head(a, hh), _hg_head(b, hh)) for hh in range(HG_HEADS)] for a, b in ops["levels"]]
    update = [jnp.dot(ops["vt"][hh], _hg_head(ops["kd"], hh), preferred_element_type=F32)
              for hh in range(HG_HEADS)]
    return scores, update


def _chunk_attn(scores, masks):
    attn = []
    for hh in range(HG_HEADS):
        a = jnp.where(masks[0], scores[0][hh], 0.0)
        for lvl in range(HG_LEVELS):
            a = jnp.where(masks[lvl + 1], scores[lvl + 1][hh], a)
        attn.append(a.astype(BF16))
    return attn


def _chunk_intra(attn, ops):
    return [jnp.dot(attn[hh], _hg_head(ops["v_b"], hh), preferred_element_type=F32) for hh in range(HG_HEADS)]


def _hgrn_block(inputs, masks, reverse, between=None):
    n = len(inputs)
    ops, products, attn, intra = ([None] * n for _ in range(4))
    for stage in range(n + 3):
        if between is not None:
            between(stage)
        if 0 <= stage - 3 < n:
            intra[stage - 3] = _chunk_intra(attn[stage - 3], ops[stage - 3])
        if 0 <= stage - 1 < n:
            products[stage - 1] = _chunk_products(ops[stage - 1])
        if 0 <= stage - 2 < n:
            attn[stage - 2] = _chunk_attn(products[stage - 2][0], masks)
        if stage < n:
            ops[stage] = _chunk_operands(*inputs[stage], reverse)
    return [(intra[c], ops[c]["qd"], ops[c]["d_end"], products[c][1]) for c in range(n)]


def _hgrn_scan(parts, subs, st_ref, emit):
    t = HG_CHUNK
    states = [st_ref[hh] for hh in range(HG_HEADS)]
    for sub, (intra, qd, d_end, update) in zip(subs, parts):
        rows = slice(sub * t, (sub + 1) * t)
        for hh in range(HG_HEADS):
            cols = slice(hh * HG_DK, (hh + 1) * HG_DK)
            o_h = intra[hh] + lax.dot_general(qd[:, cols], states[hh].astype(BF16), (((1,), (1,)), ((), ())),
                                              preferred_element_type=F32)
            states[hh] = states[hh] * d_end[:, cols] + update[hh]
            emit(rows, cols, o_h)
    for hh in range(HG_HEADS):
        st_ref[hh] = states[hh]


def _hgrn_kernel(*refs, reverse, n_sub, n_seq):
    if reverse:
        q_ref, lf_ref, i_ref, hm_ref, of_ref, g_ref, onw_ref, o_ref, st_ref = refs
    else:
        q_ref, lf_ref, i_ref, hm_ref, o_ref, st_ref = refs
    t = HG_CHUNK

    @pl.when(pl.program_id(1) == 0)
    def _():
        st_ref[...] = jnp.zeros_like(st_ref)

    subs = list(range(n_sub - 1, -1, -1) if reverse else range(n_sub))
    masks = [hm_ref[m] != 0.0 for m in range(HG_LEVELS + 1)]
    inputs = []
    for sub in subs:
        rows = slice(sub * t, (sub + 1) * t)
        for bb in range(n_seq):
            inputs.append((q_ref[bb, rows, :].astype(F32), lf_ref[bb, rows, :], i_ref[bb, rows, :].astype(F32)))
    parts = _hgrn_block(inputs, masks, reverse)

    for bb in range(n_seq):
        def emit(rows, cols, o_h, bb=bb):
            if reverse:
                o_h = o_h + of_ref[bb, rows, cols]
                ms = jnp.mean(o_h * o_h, axis=-1, keepdims=True)
                y = o_h * lax.rsqrt(ms + EPS) * onw_ref[...]
                o_ref[bb, rows, cols] = (y * g_ref[bb, rows, cols].astype(F32)).astype(o_ref.dtype)
            else:
                o_ref[bb, rows, cols] = o_h

        _hgrn_scan(parts[bb::n_seq], subs, st_ref.at[bb], emit)


def _hgrn_masks(reverse):
    t = HG_CHUNK
    rt = np.arange(t)[:, None]
    rs = np.arange(t)[None, :]
    qbit = 0 if reverse else 1
    masks = [rt == rs]
    for lvl in range(HG_LEVELS):
        masks.append(((rt >> (lvl + 1)) == (rs >> (lvl + 1))) & (((rt >> lvl) & 1) == qbit)
                     & (((rs >> lvl) & 1) != qbit))
    return jnp.asarray(np.stack(masks), F32)


def _hgrn_call(z, lf, ctx_len, reverse, o_fwd=None, onw=None):
    b, l, _ = z.shape
    t = HG_CHUNK
    rows = HG_CHUNKS_PER_STEP * t
    n_blocks = l // rows
    n_ctx = ctx_len // rows

    if reverse:
        def block(i):
            return jnp.where(i < n_ctx, n_ctx - 1 - i, n_blocks - 1 + n_ctx - i)
    else:
        def block(i):
            return i

    n_seq = HG_SEQS_PER_STEP if b % HG_SEQS_PER_STEP == 0 else 1

    def zspec(col):
        return pl.BlockSpec((n_seq, rows, UNIT), lambda bi, i: (bi, block(i), col))

    row_spec = pl.BlockSpec((n_seq, rows, HG_W), lambda bi, i: (bi, block(i), 0))
    in_specs = [zspec(COL_HQ), zspec(1 if reverse else 0), zspec(COL_HI),
                pl.BlockSpec((HG_LEVELS + 1, t, t), lambda bi, i: (0, 0, 0))]
    args = [z, lf, z, _hgrn_masks(reverse)]
    if reverse:
        in_specs += [row_spec, zspec(COL_HG), pl.BlockSpec((1, HG_DK), lambda bi, i: (0, 0))]
        args += [o_fwd, z, onw]
    return pl.pallas_call(
        functools.partial(_hgrn_kernel, reverse=reverse, n_sub=HG_CHUNKS_PER_STEP, n_seq=n_seq),
        grid=(b // n_seq, n_blocks),
        in_specs=in_specs,
        out_specs=row_spec,
        out_shape=jax.ShapeDtypeStruct((b, l, HG_W), BF16 if reverse else F32),
        scratch_shapes=[pltpu.VMEM((n_seq, HG_HEADS, HG_DK, HG_DK), F32)],
        compiler_params=_params("arbitrary", "arbitrary"),
    )(*args)


NA_ROWS_PER_STEP = 4
NA_QUERIES = NA_ROWS_PER_STEP * GRID_W
NA_BAND_ROWS = NA_WIN_R + NA_ROWS_PER_STEP
NA_BAND = NA_BAND_ROWS * GRID_W
NA_PATTERNS = 3


def _band_start(group_row, rows):
    return jnp.clip(group_row - NA_WIN_R // 2, 0, rows - NA_BAND_ROWS)


def _natten_kernel(q_ref, k_ref, v_ref, bm_ref, o_ref, *, ctx_len, rows):
    i = pl.program_id(1)
    n_ctx_blocks = ctx_len // NA_QUERIES
    pair_w = 2 * NA_HD
    lane = lax.broadcasted_iota(jnp.int32, (1, pair_w), 1)
    low = lane < NA_HD

    def nt_dot(a, b):
        return lax.dot_general(a, b, (((1,), (1,)), ((), ())), preferred_element_type=F32)

    def attend(local):
        if local:
            r = (i - n_ctx_blocks) * NA_ROWS_PER_STEP
            start = pl.multiple_of(ctx_len + _band_start(r, rows) * GRID_W, GRID_W)
            pattern = (r - _band_start(r, rows)) // NA_ROWS_PER_STEP

        def pair_cols(ref, row_slice, p):
            return ref[0, row_slice, p * pair_w:(p + 1) * pair_w]

        def gather(ref, p):
            ctx_part = pair_cols(ref, slice(0, ctx_len), p)
            if not local:
                return ctx_part
            return jnp.concatenate([pair_cols(ref, pl.ds(start, NA_BAND), p), ctx_part], axis=0)

        def scores(h):
            p, hh = divmod(h, 2)
            q2 = pair_cols(q_ref, slice(None), p)
            qm = jnp.where(low if hh == 0 else ~low, q2, jnp.zeros_like(q2))
            s = nt_dot(qm, gather(k_ref, p))
            return s + bm_ref[pattern, h] if local else s

        s_next = scores(0)
        halves = []
        for h in range(NA_HEADS):
            s = s_next
            if h + 1 < NA_HEADS:
                s_next = scores(h + 1)
            e = jnp.exp2((s - jnp.max(s, axis=-1, keepdims=True)).astype(BF16))
            vals = gather(v_ref, h // 2)
            if local:
                vals = jnp.where(low if h % 2 == 0 else ~low, vals, jnp.ones_like(vals))
                o = jnp.dot(e, vals, preferred_element_type=F32)
                halves.append(o / pltpu.roll(o, NA_HD, 1))
            else:
                o = jnp.dot(e, vals, preferred_element_type=F32)
                halves.append(o / jnp.sum(e.astype(F32), axis=-1, keepdims=True))
            if h % 2 == 1:
                p = h // 2
                o_ref[0, :, p * pair_w:(p + 1) * pair_w] = jnp.where(low, *halves).astype(o_ref.dtype)
                halves = []

    @pl.when(i < n_ctx_blocks)
    def _():
        attend(False)

    @pl.when(i >= n_ctx_blocks)
    def _():
        attend(True)


def _natten_call(z, bias, ctx_len):
    b, l, _ = z.shape
    rows = (l - ctx_len) // GRID_W

    def full(col):
        return pl.BlockSpec((1, l, NA_W), lambda bi, i: (bi, 0, col))

    return pl.pallas_call(
        functools.partial(_natten_kernel, ctx_len=ctx_len, rows=rows),
        grid=(b, l // NA_QUERIES),
        in_specs=[pl.BlockSpec((1, NA_QUERIES, NA_W), lambda bi, i: (bi, i, COL_NQ)),
                  full(COL_NK), full(COL_NV),
                  pl.BlockSpec(bias.shape, lambda bi, i: (0, 0, 0, 0), pipeline_mode=pl.Buffered(1))],
        out_specs=pl.BlockSpec((1, NA_QUERIES, NA_W), lambda bi, i: (bi, i, 0)),
        out_shape=jax.ShapeDtypeStruct((b, l, NA_W), BF16),
        compiler_params=_params("arbitrary", "arbitrary"),
    )(z, z, z, bias)


def _bias_kernel(cols_ref, o_ref):
    g = pl.program_id(0)
    for a in range(NA_ROWS_PER_STEP):
        r_rel = NA_ROWS_PER_STEP * g + a
        r0_rel = jnp.clip(r_rel - NA_WIN_R // 2, 0, NA_BAND_ROWS - NA_WIN_R)
        for j in range(NA_BAND_ROWS):
            in_window = (j >= r0_rel) & (j < r0_rel + NA_WIN_R)
            plane = cols_ref[0, jnp.clip(j - r_rel + NA_WIN_R - 1, 0, 2 * NA_WIN_R - 2)]
            o_ref[0, 0, a * GRID_W:(a + 1) * GRID_W, j * GRID_W:(j + 1) * GRID_W] = jnp.where(
                in_window, plane, NEG_BIG)
    o_ref[0, 0, :, NA_BAND:] = jnp.zeros((NA_QUERIES, o_ref.shape[3] - NA_BAND), F32)


def _bias_table(rpb, ctx_len):
    qc = np.arange(GRID_W)[:, None]
    kc = np.arange(GRID_W)[None, :]
    c0 = np.clip(qc - NA_WIN_C // 2, 0, GRID_W - NA_WIN_C)
    col_ok = (kc >= c0) & (kc < c0 + NA_WIN_C)
    n_ci = 2 * NA_WIN_C - 1
    n_ri = 2 * NA_WIN_R - 1
    onehot = (kc - qc + NA_WIN_C - 1)[:, :, None] == np.arange(n_ci)
    cols = jnp.einsum('hrc,qkc->hrqk', rpb.astype(F32), jnp.asarray(onehot, F32),
                      precision=lax.Precision.HIGHEST)
    cols = jnp.where(col_ok[None, None], cols * LOG2E, NEG_BIG)
    n_keys = NA_BAND + ctx_len
    return pl.pallas_call(
        _bias_kernel,
        grid=(NA_PATTERNS, NA_HEADS),
        in_specs=[pl.BlockSpec((1, n_ri, GRID_W, GRID_W), lambda g, h: (h, 0, 0, 0))],
        out_specs=pl.BlockSpec((1, 1, NA_QUERIES, n_keys), lambda g, h: (g, h, 0, 0)),
        out_shape=jax.ShapeDtypeStruct((NA_PATTERNS, NA_HEADS, NA_QUERIES, n_keys), F32),
        compiler_params=_params("arbitrary", "arbitrary"),
    )(cols)


def _rope_tables(ctx_len, s):
    tpos = np.arange(s)
    pos = np.stack([tpos // GRID_W, tpos % GRID_W], axis=-1).astype(np.float32)
    half = NA_HD // 2
    inv = (ROPE_THETA ** (-jnp.arange(0, half, 2, dtype=F32) / half))
    ang = jnp.asarray(pos)[:, :, None] * inv
    cos = jnp.cos(ang)
    sin = jnp.sin(ang)
    cos_h = jnp.concatenate([cos, cos], axis=-1).reshape(s, NA_HD)
    sin_h = jnp.concatenate([-sin, sin], axis=-1).reshape(s, NA_HD)
    cos_t = jnp.tile(cos_h, (1, NA_HEADS))
    sin_t = jnp.tile(sin_h, (1, NA_HEADS))
    cos_t = jnp.concatenate([jnp.ones((ctx_len, NA_W), F32), cos_t], axis=0)
    sin_t = jnp.concatenate([jnp.zeros((ctx_len, NA_W), F32), sin_t], axis=0)
    return cos_t, sin_t


def _merge_kernel(x_ref, mod_ref, ya_ref, yb_ref, cb_ref, u_ref, up_ref, un_ref, ga_ref, gb_ref, gc_ref,
                  cw_ref, wa_ref, wb_ref, wc_ref, wo_ref, o_ref, *, ctx_len, seq_len, tm):
    ti = pl.program_id(1)
    row = ti * tm + lax.broadcasted_iota(jnp.int32, (tm, 1), 0)
    local = lax.broadcasted_iota(jnp.int32, (tm, 1), 0)

    u = u_ref[0].astype(F32)
    u_before = up_ref[0, BF16_ROWS - 1:BF16_ROWS, :].astype(F32)
    u_after = un_ref[0, 0:1, :].astype(F32)
    up = jnp.where(local == 0, u_before, pltpu.roll(u, 1, 0))
    un = jnp.where(local == tm - 1, u_after, pltpu.roll(u, tm - 1, 0))
    has_prev = (row != 0) & (row != ctx_len)
    has_next = (row != ctx_len - 1) & (row != seq_len - 1)
    cw = cw_ref[...]
    conv = (jnp.where(has_prev, up, 0.0) * cw[0:1, :] + u * cw[1:2, :]
            + jnp.where(has_next, un, 0.0) * cw[2:3, :])
    yc = (cb_ref[0].astype(F32) * conv).astype(BF16)

    mix = ga_ref[0].astype(F32) * jnp.dot(ya_ref[0], wa_ref[...], preferred_element_type=F32)
    mix = mix + gb_ref[0].astype(F32) * jnp.dot(yb_ref[0], wb_ref[...], preferred_element_type=F32)
    mix = mix + gc_ref[0].astype(F32) * jnp.dot(yc, wc_ref[...], preferred_element_type=F32)
    out = jnp.dot(mix.astype(BF16), wo_ref[...], preferred_element_type=F32)
    o_ref[0] = _gated_residual(x_ref[0], out, mod_ref, 2, ctx_len)


def _merge_call(xs, mod, ya, yb, z, cw, wa, wb, wc, wo, layer, ctx_len):
    b, l, d = xs.shape
    tm = _token_tile(l, TOKEN_TILE_ROWS)
    halo = tm // BF16_ROWS
    n_halo = l // BF16_ROWS

    def zspec(col, width=UNIT):
        return pl.BlockSpec((1, tm, width), lambda bi, ti: (bi, ti, col * UNIT // width))

    def prev(col):
        return pl.BlockSpec((1, BF16_ROWS, UNIT), lambda bi, ti: (bi, jnp.maximum(ti * halo - 1, 0), col))

    def nxt(col):
        return pl.BlockSpec((1, BF16_ROWS, UNIT),
                            lambda bi, ti: (bi, jnp.minimum((ti + 1) * halo, n_halo - 1), col))

    def const(shape):
        return pl.BlockSpec(shape, lambda bi, ti: tuple(0 for _ in shape))

    tok = pl.BlockSpec((1, tm, d), lambda bi, ti: (bi, ti, 0))
    half = pl.BlockSpec((1, tm, UNIT), lambda bi, ti: (bi, ti, 0))
    return pl.pallas_call(
        functools.partial(_merge_kernel, ctx_len=ctx_len, seq_len=l, tm=tm),
        grid=(b, l // tm),
        in_specs=[tok, pl.BlockSpec((1, 2, N_MOD * d), lambda bi, ti: (bi, 0, 0)), half, half,
                  zspec(COL_CB), zspec(COL_CU), prev(COL_CU), nxt(COL_CU),
                  zspec(COL_GA, d), zspec(COL_GB, d), zspec(COL_GC, d),
                  const(cw.shape)] + [_layer_spec(w, layer) for w in (wa, wb, wc, wo)],
        out_specs=tok,
        out_shape=jax.ShapeDtypeStruct((b, l, d), F32),
        compiler_params=_params("arbitrary", "arbitrary"),
    )(xs, mod, ya, yb, z, z, z, z, z, z, z, cw, wa, wb, wc, wo)


def _mlp_kernel(x_ref, mod_ref, nw_ref, w1_ref, w2_ref, o_ref, *, ctx_len):
    x = x_ref[0]
    h = _modulated_norm(x, nw_ref[...], mod_ref, 3, 4, ctx_len).astype(BF16)
    acc = jnp.zeros(x.shape, F32)
    for c in range(w1_ref.shape[1] // MLP_FF_CHUNK):
        cols = slice(c * MLP_FF_CHUNK, (c + 1) * MLP_FF_CHUNK)
        a = jnp.maximum(jnp.dot(h, w1_ref[:, cols], preferred_element_type=F32), 0.0)
        acc = acc + jnp.dot((a * a).astype(BF16), w2_ref[cols, :], preferred_element_type=F32)
    o_ref[0] = _gated_residual(x, acc, mod_ref, 5, ctx_len)


def _mlp_call(xs, mod, nw, w1, w2, layer, ctx_len, latent_only=False):
    b, l, d = xs.shape
    first = ctx_len if latent_only else 0
    tm = _token_tile(l - first, TOKEN_TILE_ROWS)
    if latent_only:
        x_spec = pl.BlockSpec((pl.Element(1), pl.Element(tm), pl.Element(d)),
                              lambda bi, ti: (bi, pl.multiple_of(first + ti * tm, BF16_ROWS), 0))
    else:
        x_spec = pl.BlockSpec((1, tm, d), lambda bi, ti: (bi, ti, 0))
    return pl.pallas_call(
        functools.partial(_mlp_kernel, ctx_len=0 if latent_only else ctx_len),
        grid=(b, (l - first) // tm),
        in_specs=[x_spec, pl.BlockSpec((1, 2, N_MOD * d), lambda bi, ti: (bi, 0, 0)),
                  pl.BlockSpec((1, d), lambda bi, ti: (0, 0)),
                  _layer_spec(w1, layer), _layer_spec(w2, layer)],
        out_specs=pl.BlockSpec((1, tm, d), lambda bi, ti: (bi, ti, 0)),
        out_shape=jax.ShapeDtypeStruct((b, l - first, d), F32),
        compiler_params=_params("arbitrary", "arbitrary"),
    )(xs, mod, nw, w1, w2)


def kernel(x, c, ctx, c_ctx, ada_w, ada_b, norm1_w, norm2_w, w_in, hgrn_lb_logits, hgrn_onorm_w,
           q_norm_w, k_norm_w, natten_rpb, conv_w, w_branch_a, w_branch_b, w_branch_c, w_out,
           mlp_w1, mlp_w2):
    b, s, d = x.shape
    ctx_len = ctx.shape[1]
    depth = ada_w.shape[0]
    assert s % NA_QUERIES == 0 and s // GRID_W >= NA_BAND_ROWS and ctx_len % NA_QUERIES == 0
    assert ctx_len % (HG_CHUNKS_PER_STEP * HG_CHUNK) == 0 and s % (HG_CHUNKS_PER_STEP * HG_CHUNK) == 0
    assert w_in.shape[-1] == PROJ_UNITS * UNIT
    assert ctx_len <= _token_tile(ctx_len + s, PROJ_TILE_ROWS)

    lb_p = jax.nn.softmax(hgrn_lb_logits.astype(F32), axis=1)
    lower_bounds = jnp.cumsum(lb_p, axis=1) - lb_p[:, :1]

    pad_rows = -(b + 1) % SUBLANES
    cvec = jnp.concatenate([c_ctx[None, :], c, jnp.zeros((pad_rows, d), F32)], axis=0)
    mod_all = _ada_call(cvec, ada_w, ada_b)
    mod_sel = jnp.stack([jnp.broadcast_to(mod_all[:, 0:1], (depth, b, N_MOD * d)),
                         mod_all[:, 1:b + 1]], axis=2)

    cos_t, sin_t = _rope_tables(ctx_len, s)
    bd = jnp.asarray(np.kron(np.eye(NA_HEADS), np.ones((NA_HD, NA_HD))), BF16)
    w_in_b, w1_b, w2_b = (w.astype(BF16) for w in (w_in, mlp_w1, mlp_w2))
    wa_b, wb_b, wc_b, wo_b = (w.astype(BF16) for w in (w_branch_a, w_branch_b, w_branch_c, w_out))

    xs = jnp.concatenate([ctx, x], axis=1)
    for l in range(depth):
        mod = mod_sel[l]
        z, lf = _inproj_call(xs, mod, norm1_w[l][None, :], w_in_b, l, cos_t, sin_t,
                             jnp.tile(q_norm_w[l], NA_HEADS)[None, :],
                             jnp.tile(k_norm_w[l], NA_HEADS)[None, :], bd,
                             lower_bounds[:, l].reshape(1, 2 * HG_W), ctx_len)
        o_f = _hgrn_call(z, lf, ctx_len, False)
        ya = _hgrn_call(z, lf, ctx_len, True, o_f, hgrn_onorm_w[l][None, :])
        yb = _natten_call(z, _bias_table(natten_rpb[l], ctx_len), ctx_len)
        xs = _merge_call(xs, mod, ya, yb, z, conv_w[l], wa_b, wb_b, wc_b, wo_b, l, ctx_len)
        xs = _mlp_call(xs, mod, norm2_w[l][None, :], w1_b, w2_b, l, ctx_len, latent_only=l == depth - 1)
    return xs
```

```python
import functools

import numpy as np
import jax
import jax.numpy as jnp
from jax import lax
from jax.experimental import pallas as pl
from jax.experimental.pallas import tpu as pltpu

GRID_W = 64
HG_HEADS = 4
HG_DK = 128
HG_W = HG_HEADS * HG_DK
HG_CHUNK = 64
HG_LEVELS = 6
HG_CHUNKS_PER_STEP = 4
HG_SEQS_PER_STEP = 8
NA_HEADS = 8
NA_HD = 64
NA_W = NA_HEADS * NA_HD
NA_WIN_R = 8
NA_WIN_C = 16
ROPE_THETA = 10000.0
N_MOD = 6
EPS = 1e-6
NEG_BIG = -1e30
LOG2E = 1.4426950408889634

LANES = 128
SUBLANES = 8
BF16_ROWS = 16
VMEM_LIMIT = 56 * 1024 * 1024

UNIT = 512
PROJ_UNITS = 17
WU_HQ, WU_FORGET, WU_HI, WU_HG = 0, 1, 3, 4
WU_QK, WU_NV = 5, 7
WU_CB, WU_CONV = 8, 9
WU_GA, WU_GB, WU_GC = 11, 13, 15
COL_GA, COL_GB, COL_GC = 0, 2, 4
COL_HQ, COL_HG = 6, 7
COL_NQ, COL_NK = 8, 9
COL_HI, COL_NV = 10, 11
COL_CB, COL_CU = 12, 13
Z_UNITS = 14

F32 = jnp.float32
BF16 = jnp.bfloat16


def _pick_tile(n, target, mult):
    best = None
    for t in range(mult, min(n, target) + 1, mult):
        if n % t == 0:
            best = t
    if best is None:
        raise ValueError(f"no tile for {n} (target {target}, multiple {mult})")
    return best


PROJ_TILE_ROWS = 544
TOKEN_TILE_ROWS = 1088
MLP_FF_CHUNK = 1024


def _token_tile(l, rows):
    return _pick_tile(l, rows, BF16_ROWS)


def _sigmoid(x):
    return 1.0 / (1.0 + jnp.exp(-x))


def _gate_sigmoid(x):
    return 0.5 * jnp.tanh(0.5 * x) + 0.5


def _layer_spec(stacked, layer):
    zeros = (0,) * (stacked.ndim - 1)
    return pl.BlockSpec((None,) + stacked.shape[1:], lambda *_: (layer,) + zeros, pipeline_mode=pl.Buffered(1))


def _params(*sem):
    return pltpu.CompilerParams(dimension_semantics=sem, vmem_limit_bytes=VMEM_LIMIT)


def _ada_kernel(c_ref, w_ref, b_ref, o_ref):
    c = c_ref[...]
    s = c * _sigmoid(c)
    o_ref[0] = jnp.dot(s, w_ref[0], precision=lax.Precision.HIGHEST,
                       preferred_element_type=F32) + b_ref[0]


def _ada_call(cvec, ada_w, ada_b):
    depth, d, n = ada_w.shape
    rows = cvec.shape[0]
    tn = _pick_tile(n, 1024, LANES)
    return pl.pallas_call(
        _ada_kernel,
        grid=(depth, n // tn),
        in_specs=[
            pl.BlockSpec((rows, d), lambda l, j: (0, 0)),
            pl.BlockSpec((1, d, tn), lambda l, j: (l, 0, j)),
            pl.BlockSpec((1, 1, tn), lambda l, j: (l, 0, j)),
        ],
        out_specs=pl.BlockSpec((1, rows, tn), lambda l, j: (l, 0, j)),
        out_shape=jax.ShapeDtypeStruct((depth, rows, n), F32),
        compiler_params=_params("arbitrary", "arbitrary"),
    )(cvec, ada_w, ada_b.reshape(depth, 1, n))


def _segment_mod(mod_ref, idx, d):
    mc = mod_ref[0, 0:1, idx * d:(idx + 1) * d]
    ml = mod_ref[0, 1:2, idx * d:(idx + 1) * d]
    return jnp.where(pl.program_id(1) == 0, mc, ml), ml


def _by_segment(x, ctx_len, head_fn, rest_fn):
    if ctx_len == x.shape[0]:
        return head_fn(x)
    if ctx_len == 0:
        return rest_fn(x)
    return jnp.concatenate([head_fn(x[:ctx_len]), rest_fn(x[ctx_len:])], axis=0)


def _modulated_norm(x, nw, mod_ref, i_shift, i_scale, ctx_len):
    d = x.shape[-1]
    ms = jnp.mean(x * x, axis=-1, keepdims=True)
    xn = x * lax.rsqrt(ms + EPS)
    shifts = _segment_mod(mod_ref, i_shift, d)
    gains = [nw * (1.0 + s) for s in _segment_mod(mod_ref, i_scale, d)]
    return _by_segment(xn, ctx_len, lambda a: a * gains[0] + shifts[0], lambda a: a * gains[1] + shifts[1])


def _gated_residual(x, y, mod_ref, i_gate, ctx_len):
    gates = _segment_mod(mod_ref, i_gate, x.shape[-1])
    return x + _by_segment(y, ctx_len, lambda a: a * gates[0], lambda a: a * gates[1])


def _rotate(x, cos, sin_signed, first_half):
    w = x.shape[-1]
    quarter = NA_HD // 4
    partner = jnp.where(first_half, pltpu.roll(x, w - quarter, 1), pltpu.roll(x, quarter, 1))
    return x * cos + partner * sin_signed


def _inproj_kernel(x_ref, mod_ref, nw_ref, w_ref, cos_ref, sin_ref, qw_ref, kw_ref, bd_ref, lb_ref,
                   z_ref, lf_ref, *, ctx_len):
    xn = _modulated_norm(x_ref[0], nw_ref[...], mod_ref, 0, 1, ctx_len).astype(BF16)
    lane = lax.broadcasted_iota(jnp.int32, (1, NA_W), 1)
    first_half = (lane % (NA_HD // 2)) < (NA_HD // 4)

    def project(u0, n_units):
        return jnp.dot(xn, w_ref[:, u0 * UNIT:(u0 + n_units) * UNIT], preferred_element_type=F32)

    def put(col, val):
        z_ref[0, :, col * UNIT:col * UNIT + val.shape[1]] = val.astype(z_ref.dtype)

    def head_prep(y, sq, w):
        ms = jnp.dot(sq, bd_ref[...], preferred_element_type=F32) * (1.0 / NA_HD)
        return _rotate(y * lax.rsqrt(ms + EPS) * w, cos_ref[...], sin_ref[...], first_half)

    def finish_forget(r):
        lb = lb_ref[...]
        lf_ref[0] = jnp.log(lb + (1.0 - lb) * _sigmoid(r))

    r_qk = project(WU_QK, 2)
    r_forget = project(WU_FORGET, 2)
    sq = (r_qk * r_qk).astype(BF16)
    r = project(WU_HQ, 1)
    finish_forget(r_forget)
    q_ready = head_prep(r_qk[:, :UNIT], sq[:, :UNIT], qw_ref[...]) * (NA_HD ** -0.5 * LOG2E)
    k_ready = head_prep(r_qk[:, UNIT:], sq[:, UNIT:], kw_ref[...])
    plan = [(WU_HG, 1, lambda q: put(COL_HQ, q * _gate_sigmoid(q))),
            (WU_GA, 2, lambda g: put(COL_HG, g * _gate_sigmoid(g))),
            (WU_GB, 2, lambda g: put(COL_GA, _gate_sigmoid(g))),
            (WU_GC, 2, lambda g: put(COL_GB, _gate_sigmoid(g))),
            (WU_CONV, 2, lambda g: put(COL_GC, _gate_sigmoid(g))),
            (WU_HI, 1, lambda cx: put(COL_CU, cx[:, :UNIT] * cx[:, UNIT:])),
            (WU_NV, 1, lambda raw: put(COL_HI, raw)),
            (WU_CB, 1, lambda raw: put(COL_NV, raw))]
    for n, (u0, n_units, finish_previous) in enumerate(plan):
        r_next = project(u0, n_units)
        finish_previous(r)
        if n == 0:
            put(COL_NQ, q_ready)
            put(COL_NK, k_ready)
        r = r_next
    put(COL_CB, r)


def _inproj_call(xs, mod, nw, w_all, layer, cos, sin, qw, kw, bd, lb, ctx_len):
    b, l, d = xs.shape
    tm = _token_tile(l, PROJ_TILE_ROWS)

    def const(shape):
        return pl.BlockSpec(shape, lambda bi, ti: (0, 0))

    tab = pl.BlockSpec((tm, NA_W), lambda bi, ti: (ti, 0))
    return pl.pallas_call(
        functools.partial(_inproj_kernel, ctx_len=ctx_len),
        grid=(b, l // tm),
        in_specs=[
            pl.BlockSpec((1, tm, d), lambda bi, ti: (bi, ti, 0)),
            pl.BlockSpec((1, 2, N_MOD * d), lambda bi, ti: (bi, 0, 0)),
            const((1, d)),
            _layer_spec(w_all, layer),
            tab, tab, const((1, NA_W)), const((1, NA_W)), const((NA_W, NA_W)), const((1, 2 * HG_W)),
        ],
        out_specs=[pl.BlockSpec((1, tm, Z_UNITS * UNIT), lambda bi, ti: (bi, ti, 0)),
                   pl.BlockSpec((1, tm, 2 * HG_W), lambda bi, ti: (bi, ti, 0))],
        out_shape=[jax.ShapeDtypeStruct((b, l, Z_UNITS * UNIT), BF16),
                   jax.ShapeDtypeStruct((b, l, 2 * HG_W), F32)],
        compiler_params=_params("arbitrary", "arbitrary"),
    )(xs, mod, nw, w_all, cos, sin, qw, kw, bd, lb)


def _boundary_rows(gc, h, reverse):
    t, w = gc.shape
    off = h if reverse else h - 1
    if 2 * h >= SUBLANES:
        g3 = gc.reshape(t // (2 * h), 2 * h, w)
        return jnp.broadcast_to(g3[:, off:off + 1, :], g3.shape).reshape(t, w)
    g3 = gc.reshape(t // SUBLANES, SUBLANES, w)
    sub = lax.broadcasted_iota(jnp.int32, (1, SUBLANES, 1), 1)
    out = None
    for start in range(0, SUBLANES, 2 * h):
        piece = jnp.broadcast_to(g3[:, start + off:start + off + 1, :], g3.shape)
        out = piece if out is None else jnp.where(sub >= start, piece, out)
    return out.reshape(t, w)


def _scan_cumsum(lf, row, reverse):
    t = lf.shape[0]
    within = row & (SUBLANES - 1)
    g = lf
    for dist in (1, 2, 4):
        if reverse:
            g = g + jnp.where(within < SUBLANES - dist, pltpu.roll(g, t - dist, 0), 0.0)
        else:
            g = g + jnp.where(within >= dist, pltpu.roll(g, dist, 0), 0.0)
    groups = list(range(t // SUBLANES))
    pieces = {}
    carry = None
    for gi in (reversed(groups) if reverse else groups):
        piece = g[gi * SUBLANES:(gi + 1) * SUBLANES]
        if carry is not None:
            piece = piece + carry
        pieces[gi] = piece
        carry = piece[0:1] if reverse else piece[SUBLANES - 1:SUBLANES]
    return jnp.concatenate([pieces[gi] for gi in groups], axis=0)


def _level_operand(qs, kk, gc, h, reverse):
    pieces = []
    for s0 in range(0, gc.shape[0], 2 * h):
        first, second = slice(s0, s0 + h), slice(s0 + h, s0 + 2 * h)
        if reverse:
            gb = gc[s0 + h:s0 + h + 1]
            pieces += [qs[first] * jnp.exp(gc[first] - gb), kk[second] * jnp.exp(gb - gc[second])]
        else:
            gb = gc[s0 + h - 1:s0 + h]
            pieces += [kk[first] * jnp.exp(gb - gc[first]), qs[second] * jnp.exp(gc[second] - gb)]
    return jnp.concatenate(pieces, axis=0).astype(BF16)


def _nt_dot(a, b):
    return lax.dot_general(a, b, (((1,), (1,)), ((), ())), preferred_element_type=F32)


def _hg_head(a, hh):
    return a[:, hh * HG_DK:(hh + 1) * HG_DK]


def _chunk_operands(qs, lf, v, reverse):
    t = HG_CHUNK
    forget = jnp.exp(lf)
    kk = 1.0 - forget
    row = lax.broadcasted_iota(jnp.int32, (t, 1), 0)
    gc = _scan_cumsum(lf, row, reverse)
    qbit = 0 if reverse else 1
    levels = [(qs.astype(BF16), kk.astype(BF16))]
    for lvl in range(HG_LEVELS):
        is_q = ((row >> lvl) & 1) == qbit
        if lvl == 0:
            both = jnp.where(is_q, qs * forget, kk).astype(BF16)
        elif (1 << lvl) >= SUBLANES:
            both = _level_operand(qs, kk, gc, 1 << lvl, reverse)
        else:
            e = jnp.exp(-jnp.abs(gc - _boundary_rows(gc, 1 << lvl, reverse)))
            both = (jnp.where(is_q, qs, kk) * e).astype(BF16)
        levels.append((both, both))
    g_end = gc[0:1, :] if reverse else gc[t - 1:t, :]
    return dict(levels=levels,
                qd=(qs * jnp.exp(gc)).astype(BF16),
                kd=(kk * jnp.exp(g_end - gc)).astype(BF16),
                d_end=jnp.exp(g_end),
                v_b=v.astype(BF16),
                vt=[_hg_head(v, hh).T.astype(BF16) for hh in range(HG_HEADS)])


def _chunk_products(ops):
    scores = [[_nt_dot(_hg_head(a, hh), _hg_head(b, hh)) for hh in range(HG_HEADS)] for a, b in ops["levels"]]
    update = [jnp.dot(ops["vt"][hh], _hg_head(ops["kd"], hh), preferred_element_type=F32)
              for hh in range(HG_HEADS)]
    return scores, update


def _chunk_attn(scores, masks):
    attn = []
    for hh in range(HG_HEADS):
        a = jnp.where(masks[0], scores[0][hh], 0.0)
        for lvl in range(HG_LEVELS):
            a = jnp.where(masks[lvl + 1], scores[lvl + 1][hh], a)
        attn.append(a.astype(BF16))
    return attn


def _chunk_intra(attn, ops):
    return [jnp.dot(attn[hh], _hg_head(ops["v_b"], hh), preferred_element_type=F32) for hh in range(HG_HEADS)]


def _hgrn_block(inputs, masks, reverse, between=None):
    n = len(inputs)
    ops, products, attn, intra = ([None] * n for _ in range(4))
    for stage in range(n + 3):
        if between is not None:
            between(stage)
        if 0 <= stage - 3 < n:
            intra[stage - 3] = _chunk_intra(attn[stage - 3], ops[stage - 3])
        if 0 <= stage - 1 < n:
            products[stage - 1] = _chunk_products(ops[stage - 1])
        if 0 <= stage - 2 < n:
            attn[stage - 2] = _chunk_attn(products[stage - 2][0], masks)
        if stage < n:
            ops[stage] = _chunk_operands(*inputs[stage], reverse)
    return [(intra[c], ops[c]["qd"], ops[c]["d_end"], products[c][1]) for c in range(n)]


def _hgrn_scan(parts, subs, st_ref, emit):
    t = HG_CHUNK
    states = [st_ref[hh] for hh in range(HG_HEADS)]
    for sub, (intra, qd, d_end, update) in zip(subs, parts):
        rows = slice(sub * t, (sub + 1) * t)
        for hh in range(HG_HEADS):
            cols = slice(hh * HG_DK, (hh + 1) * HG_DK)
            o_h = intra[hh] + lax.dot_general(qd[:, cols], states[hh].astype(BF16), (((1,), (1,)), ((), ())),
                                              preferred_element_type=F32)
            states[hh] = states[hh] * d_end[:, cols] + update[hh]
            emit(rows, cols, o_h)
    for hh in range(HG_HEADS):
        st_ref[hh] = states[hh]


def _hgrn_kernel(*refs, reverse, n_sub, n_seq):
    if reverse:
        q_ref, lf_ref, i_ref, hm_ref, of_ref, g_ref, onw_ref, o_ref, st_ref = refs
    else:
        q_ref, lf_ref, i_ref, hm_ref, o_ref, st_ref = refs
    t = HG_CHUNK

    @pl.when(pl.program_id(1) == 0)
    def _():
        st_ref[...] = jnp.zeros_like(st_ref)

    subs = list(range(n_sub - 1, -1, -1) if reverse else range(n_sub))
    masks = [hm_ref[m] != 0.0 for m in range(HG_LEVELS + 1)]
    inputs = []
    for sub in subs:
        rows = slice(sub * t, (sub + 1) * t)
        for bb in range(n_seq):
            inputs.append((q_ref[bb, rows, :].astype(F32), lf_ref[bb, rows, :], i_ref[bb, rows, :].astype(F32)))
    parts = _hgrn_block(inputs, masks, reverse)

    for bb in range(n_seq):
        def emit(rows, cols, o_h, bb=bb):
            if reverse:
                o_h = o_h + of_ref[bb, rows, cols]
                ms = jnp.mean(o_h * o_h, axis=-1, keepdims=True)
                y = o_h * lax.rsqrt(ms + EPS) * onw_ref[...]
                o_ref[bb, rows, cols] = (y * g_ref[bb, rows, cols].astype(F32)).astype(o_ref.dtype)
            else:
                o_ref[bb, rows, cols] = o_h

        _hgrn_scan(parts[bb::n_seq], subs, st_ref.at[bb], emit)


def _hgrn_masks(reverse):
    t = HG_CHUNK
    rt = np.arange(t)[:, None]
    rs = np.arange(t)[None, :]
    qbit = 0 if reverse else 1
    masks = [rt == rs]
    for lvl in range(HG_LEVELS):
        masks.append(((rt >> (lvl + 1)) == (rs >> (lvl + 1))) & (((rt >> lvl) & 1) == qbit)
                     & (((rs >> lvl) & 1) != qbit))
    return jnp.asarray(np.stack(masks), F32)


def _hgrn_call(z, lf, ctx_len, reverse, o_fwd=None, onw=None):
    b, l, _ = z.shape
    t = HG_CHUNK
    rows = HG_CHUNKS_PER_STEP * t
    n_blocks = l // rows
    n_ctx = ctx_len // rows

    if reverse:
        def block(i):
            return jnp.where(i < n_ctx, n_ctx - 1 - i, n_blocks - 1 + n_ctx - i)
    else:
        def block(i):
            return i

    n_seq = HG_SEQS_PER_STEP if b % HG_SEQS_PER_STEP == 0 else 1

    def zspec(col):
        return pl.BlockSpec((n_seq, rows, UNIT), lambda bi, i: (bi, block(i), col))

    row_spec = pl.BlockSpec((n_seq, rows, HG_W), lambda bi, i: (bi, block(i), 0))
    in_specs = [zspec(COL_HQ), zspec(1 if reverse else 0), zspec(COL_HI),
                pl.BlockSpec((HG_LEVELS + 1, t, t), lambda bi, i: (0, 0, 0))]
    args = [z, lf, z, _hgrn_masks(reverse)]
    if reverse:
        in_specs += [row_spec, zspec(COL_HG), pl.BlockSpec((1, HG_DK), lambda bi, i: (0, 0))]
        args += [o_fwd, z, onw]
    return pl.pallas_call(
        functools.partial(_hgrn_kernel, reverse=reverse, n_sub=HG_CHUNKS_PER_STEP, n_seq=n_seq),
        grid=(b // n_seq, n_blocks),
        in_specs=in_specs,
        out_specs=row_spec,
        out_shape=jax.ShapeDtypeStruct((b, l, HG_W), BF16 if reverse else F32),
        scratch_shapes=[pltpu.VMEM((n_seq, HG_HEADS, HG_DK, HG_DK), F32)],
        compiler_params=_params("arbitrary", "arbitrary"),
    )(*args)


NA_ROWS_PER_STEP = 4
NA_QUERIES = NA_ROWS_PER_STEP * GRID_W
NA_BAND_ROWS = NA_WIN_R + NA_ROWS_PER_STEP
NA_BAND = NA_BAND_ROWS * GRID_W
NA_PATTERNS = 3


def _band_start(group_row, rows):
    return jnp.clip(group_row - NA_WIN_R // 2, 0, rows - NA_BAND_ROWS)


def _natten_kernel(q_ref, k_ref, v_ref, bm_ref, o_ref, *, ctx_len, rows):
    i = pl.program_id(1)
    n_ctx_blocks = ctx_len // NA_QUERIES
    pair_w = 2 * NA_HD
    lane = lax.broadcasted_iota(jnp.int32, (1, pair_w), 1)
    low = lane < NA_HD

    def nt_dot(a, b):
        return lax.dot_general(a, b, (((1,), (1,)), ((), ())), preferred_element_type=F32)

    def attend(local):
        if local:
            r = (i - n_ctx_blocks) * NA_ROWS_PER_STEP
            start = pl.multiple_of(ctx_len + _band_start(r, rows) * GRID_W, GRID_W)
            pattern = (r - _band_start(r, rows)) // NA_ROWS_PER_STEP

        def pair_cols(ref, row_slice, p):
            return ref[0, row_slice, p * pair_w:(p + 1) * pair_w]

        def gather(ref, p):
            ctx_part = pair_cols(ref, slice(0, ctx_len), p)
            if not local:
                return ctx_part
            return jnp.concatenate([pair_cols(ref, pl.ds(start, NA_BAND), p), ctx_part], axis=0)

        def scores(h):
            p, hh = divmod(h, 2)
            q2 = pair_cols(q_ref, slice(None), p)
            qm = jnp.where(low if hh == 0 else ~low, q2, jnp.zeros_like(q2))
            s = nt_dot(qm, gather(k_ref, p))
            return s + bm_ref[pattern, h] if local else s

        s_next = scores(0)
        halves = []
        for h in range(NA_HEADS):
            s = s_next
            if h + 1 < NA_HEADS:
                s_next = scores(h + 1)
            e = jnp.exp2((s - jnp.max(s, axis=-1, keepdims=True)).astype(BF16))
            vals = gather(v_ref, h // 2)
            if local:
                vals = jnp.where(low if h % 2 == 0 else ~low, vals, jnp.ones_like(vals))
                o = jnp.dot(e, vals, preferred_element_type=F32)
                halves.append(o / pltpu.roll(o, NA_HD, 1))
            else:
                o = jnp.dot(e, vals, preferred_element_type=F32)
                halves.append(o / jnp.sum(e.astype(F32), axis=-1, keepdims=True))
            if h % 2 == 1:
                p = h // 2
                o_ref[0, :, p * pair_w:(p + 1) * pair_w] = jnp.where(low, *halves).astype(o_ref.dtype)
                halves = []

    @pl.when(i < n_ctx_blocks)
    def _():
        attend(False)

    @pl.when(i >= n_ctx_blocks)
    def _():
        attend(True)


def _natten_call(z, bias_all, layer, ctx_len):
    b, l, _ = z.shape
    rows = (l - ctx_len) // GRID_W

    def full(col):
        return pl.BlockSpec((1, l, NA_W), lambda bi, i: (bi, 0, col))

    return pl.pallas_call(
        functools.partial(_natten_kernel, ctx_len=ctx_len, rows=rows),
        grid=(b, l // NA_QUERIES),
        in_specs=[pl.BlockSpec((1, NA_QUERIES, NA_W), lambda bi, i: (bi, i, COL_NQ)),
                  full(COL_NK), full(COL_NV),
                  _layer_spec(bias_all, layer)],
        out_specs=pl.BlockSpec((1, NA_QUERIES, NA_W), lambda bi, i: (bi, i, 0)),
        out_shape=jax.ShapeDtypeStruct((b, l, NA_W), BF16),
        compiler_params=_params("arbitrary", "arbitrary"),
    )(z, z, z, bias_all)


def _bias_kernel(cols_ref, o_ref):
    g = pl.program_id(1)
    for a in range(NA_ROWS_PER_STEP):
        r_rel = NA_ROWS_PER_STEP * g + a
        r0_rel = jnp.clip(r_rel - NA_WIN_R // 2, 0, NA_BAND_ROWS - NA_WIN_R)
        for j in range(NA_BAND_ROWS):
            in_window = (j >= r0_rel) & (j < r0_rel + NA_WIN_R)
            offset = jnp.clip(j - r_rel + NA_WIN_R - 1, 0, 2 * NA_WIN_R - 2)
            for h in range(NA_HEADS):
                o_ref[h, a * GRID_W:(a + 1) * GRID_W, j * GRID_W:(j + 1) * GRID_W] = jnp.where(
                    in_window, cols_ref[h, offset], NEG_BIG)
    o_ref[:, :, NA_BAND:] = jnp.zeros((NA_HEADS, NA_QUERIES, o_ref.shape[2] - NA_BAND), F32)


def _bias_tables(rpb, ctx_len):
    qc = np.arange(GRID_W)[:, None]
    kc = np.arange(GRID_W)[None, :]
    c0 = np.clip(qc - NA_WIN_C // 2, 0, GRID_W - NA_WIN_C)
    col_ok = (kc >= c0) & (kc < c0 + NA_WIN_C)
    n_ci = 2 * NA_WIN_C - 1
    n_ri = 2 * NA_WIN_R - 1
    onehot = (kc - qc + NA_WIN_C - 1)[:, :, None] == np.arange(n_ci)
    cols = jnp.einsum('lhrc,qkc->lhrqk', rpb.astype(F32), jnp.asarray(onehot, F32),
                      precision=lax.Precision.HIGHEST)
    cols = jnp.where(col_ok, cols * LOG2E, NEG_BIG)
    depth = rpb.shape[0]
    n_keys = NA_BAND + ctx_len
    return pl.pallas_call(
        _bias_kernel,
        grid=(depth, NA_PATTERNS),
        in_specs=[pl.BlockSpec((None, NA_HEADS, n_ri, GRID_W, GRID_W), lambda l, g: (l, 0, 0, 0, 0))],
        out_specs=pl.BlockSpec((None, None, NA_HEADS, NA_QUERIES, n_keys), lambda l, g: (l, g, 0, 0, 0)),
        out_shape=jax.ShapeDtypeStruct((depth, NA_PATTERNS, NA_HEADS, NA_QUERIES, n_keys), F32),
        compiler_params=_params("arbitrary", "arbitrary"),
    )(cols)


def _rope_tables(ctx_len, s):
    tpos = np.arange(s)
    pos = np.stack([tpos // GRID_W, tpos % GRID_W], axis=-1).astype(np.float32)
    half = NA_HD // 2
    inv = (ROPE_THETA ** (-jnp.arange(0, half, 2, dtype=F32) / half))
    ang = jnp.asarray(pos)[:, :, None] * inv
    cos = jnp.cos(ang)
    sin = jnp.sin(ang)
    cos_h = jnp.concatenate([cos, cos], axis=-1).reshape(s, NA_HD)
    sin_h = jnp.concatenate([-sin, sin], axis=-1).reshape(s, NA_HD)
    cos_t = jnp.tile(cos_h, (1, NA_HEADS))
    sin_t = jnp.tile(sin_h, (1, NA_HEADS))
    cos_t = jnp.concatenate([jnp.ones((ctx_len, NA_W), F32), cos_t], axis=0)
    sin_t = jnp.concatenate([jnp.zeros((ctx_len, NA_W), F32), sin_t], axis=0)
    return cos_t, sin_t


def _merge_kernel(x_ref, mod_ref, ya_ref, yb_ref, cb_ref, u_ref, up_ref, un_ref, ga_ref, gb_ref, gc_ref,
                  cw_ref, wa_ref, wb_ref, wc_ref, wo_ref, o_ref, *, ctx_len, seq_len, tm):
    ti = pl.program_id(1)
    row = ti * tm + lax.broadcasted_iota(jnp.int32, (tm, 1), 0)
    local = lax.broadcasted_iota(jnp.int32, (tm, 1), 0)

    u = u_ref[0].astype(F32)
    u_before = up_ref[0, BF16_ROWS - 1:BF16_ROWS, :].astype(F32)
    u_after = un_ref[0, 0:1, :].astype(F32)
    up = jnp.where(local == 0, u_before, pltpu.roll(u, 1, 0))
    un = jnp.where(local == tm - 1, u_after, pltpu.roll(u, tm - 1, 0))
    has_prev = (row != 0) & (row != ctx_len)
    has_next = (row != ctx_len - 1) & (row != seq_len - 1)
    cw = cw_ref[...]
    conv = (jnp.where(has_prev, up, 0.0) * cw[0:1, :] + u * cw[1:2, :]
            + jnp.where(has_next, un, 0.0) * cw[2:3, :])
    yc = (cb_ref[0].astype(F32) * conv).astype(BF16)

    mix = ga_ref[0].astype(F32) * jnp.dot(ya_ref[0], wa_ref[...], preferred_element_type=F32)
    mix = mix + gb_ref[0].astype(F32) * jnp.dot(yb_ref[0], wb_ref[...], preferred_element_type=F32)
    mix = mix + gc_ref[0].astype(F32) * jnp.dot(yc, wc_ref[...], preferred_element_type=F32)
    out = jnp.dot(mix.astype(BF16), wo_ref[...], preferred_element_type=F32)
    o_ref[0] = _gated_residual(x_ref[0], out, mod_ref, 2, ctx_len)


def _merge_call(xs, mod, ya, yb, z, cw, wa, wb, wc, wo, layer, ctx_len):
    b, l, d = xs.shape
    tm = _token_tile(l, TOKEN_TILE_ROWS)
    halo = tm // BF16_ROWS
    n_halo = l // BF16_ROWS

    def zspec(col, width=UNIT):
        return pl.BlockSpec((1, tm, width), lambda bi, ti: (bi, ti, col * UNIT // width))

    def prev(col):
        return pl.BlockSpec((1, BF16_ROWS, UNIT), lambda bi, ti: (bi, jnp.maximum(ti * halo - 1, 0), col))

    def nxt(col):
        return pl.BlockSpec((1, BF16_ROWS, UNIT),
                            lambda bi, ti: (bi, jnp.minimum((ti + 1) * halo, n_halo - 1), col))

    def const(shape):
        return pl.BlockSpec(shape, lambda bi, ti: tuple(0 for _ in shape))

    tok = pl.BlockSpec((1, tm, d), lambda bi, ti: (bi, ti, 0))
    half = pl.BlockSpec((1, tm, UNIT), lambda bi, ti: (bi, ti, 0))
    return pl.pallas_call(
        functools.partial(_merge_kernel, ctx_len=ctx_len, seq_len=l, tm=tm),
        grid=(b, l // tm),
        in_specs=[tok, pl.BlockSpec((1, 2, N_MOD * d), lambda bi, ti: (bi, 0, 0)), half, half,
                  zspec(COL_CB), zspec(COL_CU), prev(COL_CU), nxt(COL_CU),
                  zspec(COL_GA, d), zspec(COL_GB, d), zspec(COL_GC, d),
                  const(cw.shape)] + [_layer_spec(w, layer) for w in (wa, wb, wc, wo)],
        out_specs=tok,
        out_shape=jax.ShapeDtypeStruct((b, l, d), F32),
        compiler_params=_params("arbitrary", "arbitrary"),
    )(xs, mod, ya, yb, z, z, z, z, z, z, z, cw, wa, wb, wc, wo)


def _mlp_kernel(x_ref, mod_ref, nw_ref, w1_ref, w2_ref, o_ref, *, ctx_len):
    x = x_ref[0]
    h = _modulated_norm(x, nw_ref[...], mod_ref, 3, 4, ctx_len).astype(BF16)
    acc = jnp.zeros(x.shape, F32)
    for c in range(w1_ref.shape[1] // MLP_FF_CHUNK):
        cols = slice(c * MLP_FF_CHUNK, (c + 1) * MLP_FF_CHUNK)
        a = jnp.maximum(jnp.dot(h, w1_ref[:, cols], preferred_element_type=F32), 0.0)
        acc = acc + jnp.dot((a * a).astype(BF16), w2_ref[cols, :], preferred_element_type=F32)
    o_ref[0] = _gated_residual(x, acc, mod_ref, 5, ctx_len)


def _mlp_call(xs, mod, nw, w1, w2, layer, ctx_len, latent_only=False):
    b, l, d = xs.shape
    first = ctx_len if latent_only else 0
    tm = _token_tile(l - first, TOKEN_TILE_ROWS)
    if latent_only:
        x_spec = pl.BlockSpec((pl.Element(1), pl.Element(tm), pl.Element(d)),
                              lambda bi, ti: (bi, pl.multiple_of(first + ti * tm, BF16_ROWS), 0))
    else:
        x_spec = pl.BlockSpec((1, tm, d), lambda bi, ti: (bi, ti, 0))
    return pl.pallas_call(
        functools.partial(_mlp_kernel, ctx_len=0 if latent_only else ctx_len),
        grid=(b, (l - first) // tm),
        in_specs=[x_spec, pl.BlockSpec((1, 2, N_MOD * d), lambda bi, ti: (bi, 0, 0)),
                  pl.BlockSpec((1, d), lambda bi, ti: (0, 0)),
                  _layer_spec(w1, layer), _layer_spec(w2, layer)],
        out_specs=pl.BlockSpec((1, tm, d), lambda bi, ti: (bi, ti, 0)),
        out_shape=jax.ShapeDtypeStruct((b, l - first, d), F32),
        compiler_params=_params("arbitrary", "arbitrary"),
    )(xs, mod, nw, w1, w2)


def kernel(x, c, ctx, c_ctx, ada_w, ada_b, norm1_w, norm2_w, w_in, hgrn_lb_logits, hgrn_onorm_w,
           q_norm_w, k_norm_w, natten_rpb, conv_w, w_branch_a, w_branch_b, w_branch_c, w_out,
           mlp_w1, mlp_w2):
    b, s, d = x.shape
    ctx_len = ctx.shape[1]
    depth = ada_w.shape[0]
    assert s % NA_QUERIES == 0 and s // GRID_W >= NA_BAND_ROWS and ctx_len % NA_QUERIES == 0
    assert ctx_len % (HG_CHUNKS_PER_STEP * HG_CHUNK) == 0 and s % (HG_CHUNKS_PER_STEP * HG_CHUNK) == 0
    assert w_in.shape[-1] == PROJ_UNITS * UNIT
    assert ctx_len <= _token_tile(ctx_len + s, PROJ_TILE_ROWS)

    lb_p = jax.nn.softmax(hgrn_lb_logits.astype(F32), axis=1)
    lower_bounds = jnp.cumsum(lb_p, axis=1) - lb_p[:, :1]

    pad_rows = -(b + 1) % SUBLANES
    cvec = jnp.concatenate([c_ctx[None, :], c, jnp.zeros((pad_rows, d), F32)], axis=0)
    mod_all = _ada_call(cvec, ada_w, ada_b)
    mod_sel = jnp.stack([jnp.broadcast_to(mod_all[:, 0:1], (depth, b, N_MOD * d)),
                         mod_all[:, 1:b + 1]], axis=2)

    cos_t, sin_t = _rope_tables(ctx_len, s)
    bd = jnp.asarray(np.kron(np.eye(NA_HEADS), np.ones((NA_HD, NA_HD))), BF16)
    w_in_b, w1_b, w2_b = (w.astype(BF16) for w in (w_in, mlp_w1, mlp_w2))
    wa_b, wb_b, wc_b, wo_b = (w.astype(BF16) for w in (w_branch_a, w_branch_b, w_branch_c, w_out))

    bias_all = _bias_tables(natten_rpb, ctx_len)
    xs = jnp.concatenate([ctx, x], axis=1)
    for l in range(depth):
        mod = mod_sel[l]
        z, lf = _inproj_call(xs, mod, norm1_w[l][None, :], w_in_b, l, cos_t, sin_t,
                             jnp.tile(q_norm_w[l], NA_HEADS)[None, :],
                             jnp.tile(k_norm_w[l], NA_HEADS)[None, :], bd,
                             lower_bounds[:, l].reshape(1, 2 * HG_W), ctx_len)
        o_f = _hgrn_call(z, lf, ctx_len, False)
        ya = _hgrn_call(z, lf, ctx_len, True, o_f, hgrn_onorm_w[l][None, :])
        yb = _natten_call(z, bias_all, l, ctx_len)
        xs = _merge_call(xs, mod, ya, yb, z, conv_w[l], wa_b, wb_b, wc_b, wo_b, l, ctx_len)
        xs = _mlp_call(xs, mod, norm2_w[l][None, :], w1_b, w2_b, l, ctx_len, latent_only=l == depth - 1)
    return xs
```

```python
import functools

import numpy as np
import jax
import jax.numpy as jnp
from jax import lax
from jax.experimental import pallas as pl
from jax.experimental.pallas import tpu as pltpu

GRID_W = 64
HG_HEADS = 4
HG_DK = 128
HG_W = HG_HEADS * HG_DK
HG_CHUNK = 64
HG_LEVELS = 6
HG_CHUNKS_PER_STEP = 4
HG_SEQS_PER_STEP = 4
NA_HEADS = 8
NA_HD = 64
NA_W = NA_HEADS * NA_HD
NA_WIN_R = 8
NA_WIN_C = 16
ROPE_THETA = 10000.0
N_MOD = 6
EPS = 1e-6
NEG_BIG = -1e30
LOG2E = 1.4426950408889634

LANES = 128
SUBLANES = 8
BF16_ROWS = 16
VMEM_LIMIT = 56 * 1024 * 1024

UNIT = 512
PROJ_UNITS = 17
WU_HQ, WU_FORGET, WU_HI, WU_HG = 0, 1, 3, 4
WU_QK, WU_NV = 5, 7
WU_CB, WU_CONV = 8, 9
WU_GA, WU_GB, WU_GC = 11, 13, 15
COL_GA, COL_GB, COL_GC = 0, 2, 4
COL_HQ, COL_HG = 6, 7
COL_NQ, COL_NK = 8, 9
COL_HI, COL_NV = 10, 11
COL_CB, COL_CU = 12, 13
Z_UNITS = 14

F32 = jnp.float32
BF16 = jnp.bfloat16


def _pick_tile(n, target, mult):
    best = None
    for t in range(mult, min(n, target) + 1, mult):
        if n % t == 0:
            best = t
    if best is None:
        raise ValueError(f"no tile for {n} (target {target}, multiple {mult})")
    return best


PROJ_TILE_ROWS = 544
TOKEN_TILE_ROWS = 1088
MLP_FF_CHUNK = 1024


def _token_tile(l, rows):
    return _pick_tile(l, rows, BF16_ROWS)


def _sigmoid(x):
    return 1.0 / (1.0 + jnp.exp(-x))


def _gate_sigmoid(x):
    return 0.5 * jnp.tanh(0.5 * x) + 0.5


def _layer_spec(stacked, layer):
    zeros = (0,) * (stacked.ndim - 1)
    return pl.BlockSpec((None,) + stacked.shape[1:], lambda *_: (layer,) + zeros, pipeline_mode=pl.Buffered(1))


def _params(*sem):
    return pltpu.CompilerParams(dimension_semantics=sem, vmem_limit_bytes=VMEM_LIMIT)


def _ada_kernel(c_ref, w_ref, b_ref, o_ref):
    c = c_ref[...]
    s = c * _sigmoid(c)
    o_ref[0] = jnp.dot(s.astype(BF16), w_ref[0].astype(BF16), preferred_element_type=F32) + b_ref[0]


def _ada_call(cvec, ada_w, ada_b):
    depth, d, n = ada_w.shape
    rows = cvec.shape[0]
    tn = _pick_tile(n, 1024, LANES)
    return pl.pallas_call(
        _ada_kernel,
        grid=(depth, n // tn),
        in_specs=[
            pl.BlockSpec((rows, d), lambda l, j: (0, 0)),
            pl.BlockSpec((1, d, tn), lambda l, j: (l, 0, j)),
            pl.BlockSpec((1, 1, tn), lambda l, j: (l, 0, j)),
        ],
        out_specs=pl.BlockSpec((1, rows, tn), lambda l, j: (l, 0, j)),
        out_shape=jax.ShapeDtypeStruct((depth, rows, n), F32),
        compiler_params=_params("arbitrary", "arbitrary"),
    )(cvec, ada_w, ada_b.reshape(depth, 1, n))


def _segment_mod(mod_ref, idx, d):
    mc = mod_ref[0, 0:1, idx * d:(idx + 1) * d]
    ml = mod_ref[0, 1:2, idx * d:(idx + 1) * d]
    return jnp.where(pl.program_id(1) == 0, mc, ml), ml


def _by_segment(x, ctx_len, head_fn, rest_fn):
    if ctx_len == x.shape[0]:
        return head_fn(x)
    if ctx_len == 0:
        return rest_fn(x)
    return jnp.concatenate([head_fn(x[:ctx_len]), rest_fn(x[ctx_len:])], axis=0)


def _modulated_norm(x, nw, mod_ref, i_shift, i_scale, ctx_len):
    d = x.shape[-1]
    ms = jnp.mean(x * x, axis=-1, keepdims=True)
    xn = x * lax.rsqrt(ms + EPS)
    shifts = _segment_mod(mod_ref, i_shift, d)
    gains = [nw * (1.0 + s) for s in _segment_mod(mod_ref, i_scale, d)]
    return _by_segment(xn, ctx_len, lambda a: a * gains[0] + shifts[0], lambda a: a * gains[1] + shifts[1])


def _gated_residual(x, y, mod_ref, i_gate, ctx_len):
    gates = _segment_mod(mod_ref, i_gate, x.shape[-1])
    return x + _by_segment(y, ctx_len, lambda a: a * gates[0], lambda a: a * gates[1])


def _rotate(x, cos, sin_signed, first_half):
    w = x.shape[-1]
    quarter = NA_HD // 4
    partner = jnp.where(first_half, pltpu.roll(x, w - quarter, 1), pltpu.roll(x, quarter, 1))
    return x * cos + partner * sin_signed


def _inproj_kernel(x_ref, mod_ref, nw_ref, w_ref, cos_ref, sin_ref, qw_ref, kw_ref, bd_ref, lb_ref,
                   z_ref, lf_ref, *, ctx_len):
    xn = _modulated_norm(x_ref[0], nw_ref[...], mod_ref, 0, 1, ctx_len).astype(BF16)
    lane = lax.broadcasted_iota(jnp.int32, (1, NA_W), 1)
    first_half = (lane % (NA_HD // 2)) < (NA_HD // 4)

    def project(u0, n_units):
        return jnp.dot(xn, w_ref[:, u0 * UNIT:(u0 + n_units) * UNIT], preferred_element_type=F32)

    def put(col, val):
        z_ref[0, :, col * UNIT:col * UNIT + val.shape[1]] = val.astype(z_ref.dtype)

    def head_prep(y, sq, w):
        ms = jnp.dot(sq, bd_ref[...], preferred_element_type=F32) * (1.0 / NA_HD)
        return _rotate(y * lax.rsqrt(ms + EPS) * w, cos_ref[...], sin_ref[...], first_half)

    def finish_forget(r):
        lb = lb_ref[...]
        lf_ref[0] = jnp.log(lb + (1.0 - lb) * _sigmoid(r))

    r_qk = project(WU_QK, 2)
    r_forget = project(WU_FORGET, 2)
    sq = (r_qk * r_qk).astype(BF16)
    r = project(WU_HQ, 1)
    finish_forget(r_forget)
    q_ready = head_prep(r_qk[:, :UNIT], sq[:, :UNIT], qw_ref[...]) * (NA_HD ** -0.5 * LOG2E)
    k_ready = head_prep(r_qk[:, UNIT:], sq[:, UNIT:], kw_ref[...])
    plan = [(WU_HG, 1, lambda q: put(COL_HQ, q * _gate_sigmoid(q))),
            (WU_GA, 2, lambda g: put(COL_HG, g * _gate_sigmoid(g))),
            (WU_GB, 2, lambda g: put(COL_GA, _gate_sigmoid(g))),
            (WU_GC, 2, lambda g: put(COL_GB, _gate_sigmoid(g))),
            (WU_CONV, 2, lambda g: put(COL_GC, _gate_sigmoid(g))),
            (WU_HI, 1, lambda cx: put(COL_CU, cx[:, :UNIT] * cx[:, UNIT:])),
            (WU_NV, 1, lambda raw: put(COL_HI, raw)),
            (WU_CB, 1, lambda raw: put(COL_NV, raw))]
    for n, (u0, n_units, finish_previous) in enumerate(plan):
        r_next = project(u0, n_units)
        finish_previous(r)
        if n == 0:
            put(COL_NQ, q_ready)
            put(COL_NK, k_ready)
        r = r_next
    put(COL_CB, r)


def _inproj_call(xs, mod, nw, w_all, layer, cos, sin, qw, kw, bd, lb, ctx_len):
    b, l, d = xs.shape
    tm = _token_tile(l, PROJ_TILE_ROWS)

    def const(shape):
        return pl.BlockSpec(shape, lambda bi, ti: (0, 0))

    tab = pl.BlockSpec((tm, NA_W), lambda bi, ti: (ti, 0))
    return pl.pallas_call(
        functools.partial(_inproj_kernel, ctx_len=ctx_len),
        grid=(b, l // tm),
        in_specs=[
            pl.BlockSpec((1, tm, d), lambda bi, ti: (bi, ti, 0)),
            pl.BlockSpec((1, 2, N_MOD * d), lambda bi, ti: (bi, 0, 0)),
            const((1, d)),
            _layer_spec(w_all, layer),
            tab, tab, const((1, NA_W)), const((1, NA_W)), const((NA_W, NA_W)), const((1, 2 * HG_W)),
        ],
        out_specs=[pl.BlockSpec((1, tm, Z_UNITS * UNIT), lambda bi, ti: (bi, ti, 0)),
                   pl.BlockSpec((1, tm, 2 * HG_W), lambda bi, ti: (bi, ti, 0))],
        out_shape=[jax.ShapeDtypeStruct((b, l, Z_UNITS * UNIT), BF16),
                   jax.ShapeDtypeStruct((b, l, 2 * HG_W), F32)],
        compiler_params=_params("arbitrary", "arbitrary"),
    )(xs, mod, nw, w_all, cos, sin, qw, kw, bd, lb)


def _boundary_rows(gc, h, reverse):
    t, w = gc.shape
    off = h if reverse else h - 1
    if 2 * h >= SUBLANES:
        g3 = gc.reshape(t // (2 * h), 2 * h, w)
        return jnp.broadcast_to(g3[:, off:off + 1, :], g3.shape).reshape(t, w)
    g3 = gc.reshape(t // SUBLANES, SUBLANES, w)
    sub = lax.broadcasted_iota(jnp.int32, (1, SUBLANES, 1), 1)
    out = None
    for start in range(0, SUBLANES, 2 * h):
        piece = jnp.broadcast_to(g3[:, start + off:start + off + 1, :], g3.shape)
        out = piece if out is None else jnp.where(sub >= start, piece, out)
    return out.reshape(t, w)


def _scan_cumsum(lf, row, reverse):
    t = lf.shape[0]
    within = row & (SUBLANES - 1)
    g = lf
    for dist in (1, 2, 4):
        if reverse:
            g = g + jnp.where(within < SUBLANES - dist, pltpu.roll(g, t - dist, 0), 0.0)
        else:
            g = g + jnp.where(within >= dist, pltpu.roll(g, dist, 0), 0.0)
    groups = list(range(t // SUBLANES))
    pieces = {}
    carry = None
    for gi in (reversed(groups) if reverse else groups):
        piece = g[gi * SUBLANES:(gi + 1) * SUBLANES]
        if carry is not None:
            piece = piece + carry
        pieces[gi] = piece
        carry = piece[0:1] if reverse else piece[SUBLANES - 1:SUBLANES]
    return jnp.concatenate([pieces[gi] for gi in groups], axis=0)


def _level_operand(qs, kk, gc, h, reverse):
    pieces = []
    for s0 in range(0, gc.shape[0], 2 * h):
        first, second = slice(s0, s0 + h), slice(s0 + h, s0 + 2 * h)
        if reverse:
            gb = gc[s0 + h:s0 + h + 1]
            pieces += [qs[first] * jnp.exp(gc[first] - gb), kk[second] * jnp.exp(gb - gc[second])]
        else:
            gb = gc[s0 + h - 1:s0 + h]
            pieces += [kk[first] * jnp.exp(gb - gc[first]), qs[second] * jnp.exp(gc[second] - gb)]
    return jnp.concatenate(pieces, axis=0).astype(BF16)


def _nt_dot(a, b):
    return lax.dot_general(a, b, (((1,), (1,)), ((), ())), preferred_element_type=F32)


def _hg_head(a, hh):
    return a[:, hh * HG_DK:(hh + 1) * HG_DK]


def _chunk_operands(qs, lf, v, reverse):
    t = HG_CHUNK
    forget = jnp.exp(lf)
    kk = 1.0 - forget
    row = lax.broadcasted_iota(jnp.int32, (t, 1), 0)
    gc = _scan_cumsum(lf, row, reverse)
    qbit = 0 if reverse else 1
    levels = [(qs.astype(BF16), kk.astype(BF16))]
    for lvl in range(HG_LEVELS):
        is_q = ((row >> lvl) & 1) == qbit
        if lvl == 0:
            both = jnp.where(is_q, qs * forget, kk).astype(BF16)
        elif (1 << lvl) >= SUBLANES:
            both = _level_operand(qs, kk, gc, 1 << lvl, reverse)
        else:
            e = jnp.exp(-jnp.abs(gc - _boundary_rows(gc, 1 << lvl, reverse)))
            both = (jnp.where(is_q, qs, kk) * e).astype(BF16)
        levels.append((both, both))
    g_end = gc[0:1, :] if reverse else gc[t - 1:t, :]
    return dict(levels=levels,
                qd=(qs * jnp.exp(gc)).astype(BF16),
                kd=(kk * jnp.exp(g_end - gc)).astype(BF16),
                d_end=jnp.exp(g_end),
                v_b=v.astype(BF16),
                vt=[_hg_head(v, hh).T.astype(BF16) for hh in range(HG_HEADS)])


def _chunk_products(ops):
    scores = [[_nt_dot(_hg_head(a, hh), _hg_head(b, hh)) for hh in range(HG_HEADS)] for a, b in ops["levels"]]
    update = [jnp.dot(ops["vt"][hh], _hg_head(ops["kd"], hh), preferred_element_type=F32)
              for hh in range(HG_HEADS)]
    return scores, update


def _chunk_attn(scores, masks):
    attn = []
    for hh in range(HG_HEADS):
        a = jnp.where(masks[0], scores[0][hh], 0.0)
        for lvl in range(HG_LEVELS):
            a = jnp.where(masks[lvl + 1], scores[lvl + 1][hh], a)
        attn.append(a.astype(BF16))
    return attn


def _chunk_intra(attn, ops):
    return [jnp.dot(attn[hh], _hg_head(ops["v_b"], hh), preferred_element_type=F32) for hh in range(HG_HEADS)]


def _hgrn_block(inputs, masks, reverse, between=None):
    n = len(inputs)
    ops, products, attn, intra = ([None] * n for _ in range(4))
    for stage in range(n + 3):
        if between is not None:
            between(stage)
        if 0 <= stage - 3 < n:
            intra[stage - 3] = _chunk_intra(attn[stage - 3], ops[stage - 3])
        if 0 <= stage - 1 < n:
            products[stage - 1] = _chunk_products(ops[stage - 1])
        if 0 <= stage - 2 < n:
            attn[stage - 2] = _chunk_attn(products[stage - 2][0], masks)
        if stage < n:
            ops[stage] = _chunk_operands(*inputs[stage], reverse)
    return [(intra[c], ops[c]["qd"], ops[c]["d_end"], products[c][1]) for c in range(n)]


def _hgrn_scan(parts, subs, st_ref, emit):
    t = HG_CHUNK
    states = [st_ref[hh] for hh in range(HG_HEADS)]
    for sub, (intra, qd, d_end, update) in zip(subs, parts):
        rows = slice(sub * t, (sub + 1) * t)
        for hh in range(HG_HEADS):
            cols = slice(hh * HG_DK, (hh + 1) * HG_DK)
            o_h = intra[hh] + lax.dot_general(qd[:, cols], states[hh].astype(BF16), (((1,), (1,)), ((), ())),
                                              preferred_element_type=F32)
            states[hh] = states[hh] * d_end[:, cols] + update[hh]
            emit(rows, cols, o_h)
    for hh in range(HG_HEADS):
        st_ref[hh] = states[hh]


def _hgrn_kernel(*refs, reverse, n_sub, n_seq):
    if reverse:
        q_ref, lf_ref, i_ref, hm_ref, of_ref, g_ref, onw_ref, o_ref, st_ref = refs
    else:
        q_ref, lf_ref, i_ref, hm_ref, o_ref, st_ref = refs
    t = HG_CHUNK

    @pl.when(pl.program_id(1) == 0)
    def _():
        st_ref[...] = jnp.zeros_like(st_ref)

    subs = list(range(n_sub - 1, -1, -1) if reverse else range(n_sub))
    masks = [hm_ref[m] != 0.0 for m in range(HG_LEVELS + 1)]
    inputs = []
    for sub in subs:
        rows = slice(sub * t, (sub + 1) * t)
        for bb in range(n_seq):
            inputs.append((q_ref[bb, rows, :].astype(F32), lf_ref[bb, rows, :], i_ref[bb, rows, :].astype(F32)))
    parts = _hgrn_block(inputs, masks, reverse)

    for bb in range(n_seq):
        def emit(rows, cols, o_h, bb=bb):
            if reverse:
                o_h = o_h + of_ref[bb, rows, cols]
                ms = jnp.mean(o_h * o_h, axis=-1, keepdims=True)
                y = o_h * lax.rsqrt(ms + EPS) * onw_ref[...]
                o_ref[bb, rows, cols] = (y * g_ref[bb, rows, cols].astype(F32)).astype(o_ref.dtype)
            else:
                o_ref[bb, rows, cols] = o_h

        _hgrn_scan(parts[bb::n_seq], subs, st_ref.at[bb], emit)


def _hgrn_masks(reverse):
    t = HG_CHUNK
    rt = np.arange(t)[:, None]
    rs = np.arange(t)[None, :]
    qbit = 0 if reverse else 1
    masks = [rt == rs]
    for lvl in range(HG_LEVELS):
        masks.append(((rt >> (lvl + 1)) == (rs >> (lvl + 1))) & (((rt >> lvl) & 1) == qbit)
                     & (((rs >> lvl) & 1) != qbit))
    return jnp.asarray(np.stack(masks), F32)


def _hgrn_call(z, lf, ctx_len, reverse, o_fwd=None, onw=None):
    b, l, _ = z.shape
    t = HG_CHUNK
    rows = HG_CHUNKS_PER_STEP * t
    n_blocks = l // rows
    n_ctx = ctx_len // rows

    if reverse:
        def block(i):
            return jnp.where(i < n_ctx, n_ctx - 1 - i, n_blocks - 1 + n_ctx - i)
    else:
        def block(i):
            return i

    n_seq = HG_SEQS_PER_STEP if b % HG_SEQS_PER_STEP == 0 else 1

    def zspec(col):
        return pl.BlockSpec((n_seq, rows, UNIT), lambda bi, i: (bi, block(i), col))

    row_spec = pl.BlockSpec((n_seq, rows, HG_W), lambda bi, i: (bi, block(i), 0))
    in_specs = [zspec(COL_HQ), zspec(1 if reverse else 0), zspec(COL_HI),
                pl.BlockSpec((HG_LEVELS + 1, t, t), lambda bi, i: (0, 0, 0))]
    args = [z, lf, z, _hgrn_masks(reverse)]
    if reverse:
        in_specs += [row_spec, zspec(COL_HG), pl.BlockSpec((1, HG_DK), lambda bi, i: (0, 0))]
        args += [o_fwd, z, onw]
    return pl.pallas_call(
        functools.partial(_hgrn_kernel, reverse=reverse, n_sub=HG_CHUNKS_PER_STEP, n_seq=n_seq),
        grid=(b // n_seq, n_blocks),
        in_specs=in_specs,
        out_specs=row_spec,
        out_shape=jax.ShapeDtypeStruct((b, l, HG_W), BF16 if reverse else F32),
        scratch_shapes=[pltpu.VMEM((n_seq, HG_HEADS, HG_DK, HG_DK), F32)],
        compiler_params=_params("arbitrary", "arbitrary"),
    )(*args)


NA_ROWS_PER_STEP = 4
NA_QUERIES = NA_ROWS_PER_STEP * GRID_W
NA_BAND_ROWS = NA_WIN_R + NA_ROWS_PER_STEP
NA_BAND = NA_BAND_ROWS * GRID_W
NA_PATTERNS = 3


def _band_start(group_row, rows):
    return jnp.clip(group_row - NA_WIN_R // 2, 0, rows - NA_BAND_ROWS)


def _natten_kernel(q_ref, k_ref, v_ref, bm_ref, o_ref, *, ctx_len, rows):
    i = pl.program_id(1)
    n_ctx_blocks = ctx_len // NA_QUERIES
    pair_w = 2 * NA_HD
    lane = lax.broadcasted_iota(jnp.int32, (1, pair_w), 1)
    low = lane < NA_HD

    def nt_dot(a, b):
        return lax.dot_general(a, b, (((1,), (1,)), ((), ())), preferred_element_type=F32)

    def attend(local):
        if local:
            r = (i - n_ctx_blocks) * NA_ROWS_PER_STEP
            start = pl.multiple_of(ctx_len + _band_start(r, rows) * GRID_W, GRID_W)
            pattern = (r - _band_start(r, rows)) // NA_ROWS_PER_STEP

        def pair_cols(ref, row_slice, p):
            return ref[0, row_slice, p * pair_w:(p + 1) * pair_w]

        def gather(ref, p):
            ctx_part = pair_cols(ref, slice(0, ctx_len), p)
            if not local:
                return ctx_part
            return jnp.concatenate([pair_cols(ref, pl.ds(start, NA_BAND), p), ctx_part], axis=0)

        def scores(h):
            p, hh = divmod(h, 2)
            q2 = pair_cols(q_ref, slice(None), p)
            qm = jnp.where(low if hh == 0 else ~low, q2, jnp.zeros_like(q2))
            s = nt_dot(qm, gather(k_ref, p))
            return s + bm_ref[pattern, h] if local else s

        s_next = scores(0)
        halves = []
        for h in range(NA_HEADS):
            s = s_next
            if h + 1 < NA_HEADS:
                s_next = scores(h + 1)
            e = jnp.exp2((s - jnp.max(s, axis=-1, keepdims=True)).astype(BF16))
            vals = gather(v_ref, h // 2)
            if local:
                vals = jnp.where(low if h % 2 == 0 else ~low, vals, jnp.ones_like(vals))
                o = jnp.dot(e, vals, preferred_element_type=F32)
                halves.append(o / pltpu.roll(o, NA_HD, 1))
            else:
                o = jnp.dot(e, vals, preferred_element_type=F32)
                halves.append(o / jnp.sum(e.astype(F32), axis=-1, keepdims=True))
            if h % 2 == 1:
                p = h // 2
                o_ref[0, :, p * pair_w:(p + 1) * pair_w] = jnp.where(low, *halves).astype(o_ref.dtype)
                halves = []

    @pl.when(i < n_ctx_blocks)
    def _():
        attend(False)

    @pl.when(i >= n_ctx_blocks)
    def _():
        attend(True)


def _natten_call(z, bias_all, layer, ctx_len):
    b, l, _ = z.shape
    rows = (l - ctx_len) // GRID_W

    def full(col):
        return pl.BlockSpec((1, l, NA_W), lambda bi, i: (bi, 0, col))

    return pl.pallas_call(
        functools.partial(_natten_kernel, ctx_len=ctx_len, rows=rows),
        grid=(b, l // NA_QUERIES),
        in_specs=[pl.BlockSpec((1, NA_QUERIES, NA_W), lambda bi, i: (bi, i, COL_NQ)),
                  full(COL_NK), full(COL_NV),
                  _layer_spec(bias_all, layer)],
        out_specs=pl.BlockSpec((1, NA_QUERIES, NA_W), lambda bi, i: (bi, i, 0)),
        out_shape=jax.ShapeDtypeStruct((b, l, NA_W), BF16),
        compiler_params=_params("arbitrary", "arbitrary"),
    )(z, z, z, bias_all)


def _bias_kernel(cols_ref, o_ref):
    g = pl.program_id(1)
    for a in range(NA_ROWS_PER_STEP):
        r_rel = NA_ROWS_PER_STEP * g + a
        r0_rel = jnp.clip(r_rel - NA_WIN_R // 2, 0, NA_BAND_ROWS - NA_WIN_R)
        for j in range(NA_BAND_ROWS):
            in_window = (j >= r0_rel) & (j < r0_rel + NA_WIN_R)
            offset = jnp.clip(j - r_rel + NA_WIN_R - 1, 0, 2 * NA_WIN_R - 2)
            for h in range(NA_HEADS):
                o_ref[h, a * GRID_W:(a + 1) * GRID_W, j * GRID_W:(j + 1) * GRID_W] = jnp.where(
                    in_window, cols_ref[h, offset], NEG_BIG)
    o_ref[:, :, NA_BAND:] = jnp.zeros((NA_HEADS, NA_QUERIES, o_ref.shape[2] - NA_BAND), F32)


def _bias_tables(rpb, ctx_len):
    qc = np.arange(GRID_W)[:, None]
    kc = np.arange(GRID_W)[None, :]
    c0 = np.clip(qc - NA_WIN_C // 2, 0, GRID_W - NA_WIN_C)
    col_ok = (kc >= c0) & (kc < c0 + NA_WIN_C)
    n_ci = 2 * NA_WIN_C - 1
    n_ri = 2 * NA_WIN_R - 1
    onehot = (kc - qc + NA_WIN_C - 1)[:, :, None] == np.arange(n_ci)
    cols = jnp.einsum('lhrc,qkc->lhrqk', rpb.astype(F32), jnp.asarray(onehot, F32),
                      precision=lax.Precision.HIGHEST)
    cols = jnp.where(col_ok, cols * LOG2E, NEG_BIG)
    depth = rpb.shape[0]
    n_keys = NA_BAND + ctx_len
    return pl.pallas_call(
        _bias_kernel,
        grid=(depth, NA_PATTERNS),
        in_specs=[pl.BlockSpec((None, NA_HEADS, n_ri, GRID_W, GRID_W), lambda l, g: (l, 0, 0, 0, 0))],
        out_specs=pl.BlockSpec((None, None, NA_HEADS, NA_QUERIES, n_keys), lambda l, g: (l, g, 0, 0, 0)),
        out_shape=jax.ShapeDtypeStruct((depth, NA_PATTERNS, NA_HEADS, NA_QUERIES, n_keys), F32),
        compiler_params=_params("arbitrary", "arbitrary"),
    )(cols)


def _rope_tables(ctx_len, s):
    tpos = np.arange(s)
    pos = np.stack([tpos // GRID_W, tpos % GRID_W], axis=-1).astype(np.float32)
    half = NA_HD // 2
    inv = (ROPE_THETA ** (-jnp.arange(0, half, 2, dtype=F32) / half))
    ang = jnp.asarray(pos)[:, :, None] * inv
    cos = jnp.cos(ang)
    sin = jnp.sin(ang)
    cos_h = jnp.concatenate([cos, cos], axis=-1).reshape(s, NA_HD)
    sin_h = jnp.concatenate([-sin, sin], axis=-1).reshape(s, NA_HD)
    cos_t = jnp.tile(cos_h, (1, NA_HEADS))
    sin_t = jnp.tile(sin_h, (1, NA_HEADS))
    cos_t = jnp.concatenate([jnp.ones((ctx_len, NA_W), F32), cos_t], axis=0)
    sin_t = jnp.concatenate([jnp.zeros((ctx_len, NA_W), F32), sin_t], axis=0)
    return cos_t, sin_t


def _merge_kernel(x_ref, mod_ref, ya_ref, yb_ref, cb_ref, u_ref, up_ref, un_ref, ga_ref, gb_ref, gc_ref,
                  cw_ref, wa_ref, wb_ref, wc_ref, wo_ref, o_ref, *, ctx_len, seq_len, tm):
    ti = pl.program_id(1)
    row = ti * tm + lax.broadcasted_iota(jnp.int32, (tm, 1), 0)
    local = lax.broadcasted_iota(jnp.int32, (tm, 1), 0)

    u = u_ref[0].astype(F32)
    u_before = up_ref[0, BF16_ROWS - 1:BF16_ROWS, :].astype(F32)
    u_after = un_ref[0, 0:1, :].astype(F32)
    up = jnp.where(local == 0, u_before, pltpu.roll(u, 1, 0))
    un = jnp.where(local == tm - 1, u_after, pltpu.roll(u, tm - 1, 0))
    has_prev = (row != 0) & (row != ctx_len)
    has_next = (row != ctx_len - 1) & (row != seq_len - 1)
    cw = cw_ref[...]
    conv = (jnp.where(has_prev, up, 0.0) * cw[0:1, :] + u * cw[1:2, :]
            + jnp.where(has_next, un, 0.0) * cw[2:3, :])
    yc = (cb_ref[0].astype(F32) * conv).astype(BF16)

    mix = ga_ref[0].astype(F32) * jnp.dot(ya_ref[0], wa_ref[...], preferred_element_type=F32)
    mix = mix + gb_ref[0].astype(F32) * jnp.dot(yb_ref[0], wb_ref[...], preferred_element_type=F32)
    mix = mix + gc_ref[0].astype(F32) * jnp.dot(yc, wc_ref[...], preferred_element_type=F32)
    out = jnp.dot(mix.astype(BF16), wo_ref[...], preferred_element_type=F32)
    o_ref[0] = _gated_residual(x_ref[0], out, mod_ref, 2, ctx_len)


def _merge_call(xs, mod, ya, yb, z, cw, wa, wb, wc, wo, layer, ctx_len):
    b, l, d = xs.shape
    tm = _token_tile(l, TOKEN_TILE_ROWS)
    halo = tm // BF16_ROWS
    n_halo = l // BF16_ROWS

    def zspec(col, width=UNIT):
        return pl.BlockSpec((1, tm, width), lambda bi, ti: (bi, ti, col * UNIT // width))

    def prev(col):
        return pl.BlockSpec((1, BF16_ROWS, UNIT), lambda bi, ti: (bi, jnp.maximum(ti * halo - 1, 0), col))

    def nxt(col):
        return pl.BlockSpec((1, BF16_ROWS, UNIT),
                            lambda bi, ti: (bi, jnp.minimum((ti + 1) * halo, n_halo - 1), col))

    def const(shape):
        return pl.BlockSpec(shape, lambda bi, ti: tuple(0 for _ in shape))

    tok = pl.BlockSpec((1, tm, d), lambda bi, ti: (bi, ti, 0))
    half = pl.BlockSpec((1, tm, UNIT), lambda bi, ti: (bi, ti, 0))
    return pl.pallas_call(
        functools.partial(_merge_kernel, ctx_len=ctx_len, seq_len=l, tm=tm),
        grid=(b, l // tm),
        in_specs=[tok, pl.BlockSpec((1, 2, N_MOD * d), lambda bi, ti: (bi, 0, 0)), half, half,
                  zspec(COL_CB), zspec(COL_CU), prev(COL_CU), nxt(COL_CU),
                  zspec(COL_GA, d), zspec(COL_GB, d), zspec(COL_GC, d),
                  const(cw.shape)] + [_layer_spec(w, layer) for w in (wa, wb, wc, wo)],
        out_specs=tok,
        out_shape=jax.ShapeDtypeStruct((b, l, d), F32),
        compiler_params=_params("arbitrary", "arbitrary"),
    )(xs, mod, ya, yb, z, z, z, z, z, z, z, cw, wa, wb, wc, wo)


def _mlp_kernel(x_ref, mod_ref, nw_ref, w1_ref, w2_ref, o_ref, *, ctx_len):
    x = x_ref[0]
    h = _modulated_norm(x, nw_ref[...], mod_ref, 3, 4, ctx_len).astype(BF16)
    acc = jnp.zeros(x.shape, F32)
    for c in range(w1_ref.shape[1] // MLP_FF_CHUNK):
        cols = slice(c * MLP_FF_CHUNK, (c + 1) * MLP_FF_CHUNK)
        a = jnp.maximum(jnp.dot(h, w1_ref[:, cols], preferred_element_type=F32), 0.0)
        acc = acc + jnp.dot((a * a).astype(BF16), w2_ref[cols, :], preferred_element_type=F32)
    o_ref[0] = _gated_residual(x, acc, mod_ref, 5, ctx_len)


def _mlp_call(xs, mod, nw, w1, w2, layer, ctx_len, latent_only=False):
    b, l, d = xs.shape
    first = ctx_len if latent_only else 0
    tm = _token_tile(l - first, TOKEN_TILE_ROWS)
    if latent_only:
        x_spec = pl.BlockSpec((pl.Element(1), pl.Element(tm), pl.Element(d)),
                              lambda bi, ti: (bi, pl.multiple_of(first + ti * tm, BF16_ROWS), 0))
    else:
        x_spec = pl.BlockSpec((1, tm, d), lambda bi, ti: (bi, ti, 0))
    return pl.pallas_call(
        functools.partial(_mlp_kernel, ctx_len=0 if latent_only else ctx_len),
        grid=(b, (l - first) // tm),
        in_specs=[x_spec, pl.BlockSpec((1, 2, N_MOD * d), lambda bi, ti: (bi, 0, 0)),
                  pl.BlockSpec((1, d), lambda bi, ti: (0, 0)),
                  _layer_spec(w1, layer), _layer_spec(w2, layer)],
        out_specs=pl.BlockSpec((1, tm, d), lambda bi, ti: (bi, ti, 0)),
        out_shape=jax.ShapeDtypeStruct((b, l - first, d), F32),
        compiler_params=_params("arbitrary", "arbitrary"),
    )(xs, mod, nw, w1, w2)


def kernel(x, c, ctx, c_ctx, ada_w, ada_b, norm1_w, norm2_w, w_in, hgrn_lb_logits, hgrn_onorm_w,
           q_norm_w, k_norm_w, natten_rpb, conv_w, w_branch_a, w_branch_b, w_branch_c, w_out,
           mlp_w1, mlp_w2):
    b, s, d = x.shape
    ctx_len = ctx.shape[1]
    depth = ada_w.shape[0]
    assert s % NA_QUERIES == 0 and s // GRID_W >= NA_BAND_ROWS and ctx_len % NA_QUERIES == 0
    assert ctx_len % (HG_CHUNKS_PER_STEP * HG_CHUNK) == 0 and s % (HG_CHUNKS_PER_STEP * HG_CHUNK) == 0
    assert w_in.shape[-1] == PROJ_UNITS * UNIT
    assert ctx_len <= _token_tile(ctx_len + s, PROJ_TILE_ROWS)

    lb_p = jax.nn.softmax(hgrn_lb_logits.astype(F32), axis=1)
    lower_bounds = jnp.cumsum(lb_p, axis=1) - lb_p[:, :1]

    pad_rows = -(b + 1) % SUBLANES
    cvec = jnp.concatenate([c_ctx[None, :], c, jnp.zeros((pad_rows, d), F32)], axis=0)
    mod_all = _ada_call(cvec, ada_w, ada_b)
    mod_sel = jnp.stack([jnp.broadcast_to(mod_all[:, 0:1], (depth, b, N_MOD * d)),
                         mod_all[:, 1:b + 1]], axis=2)

    cos_t, sin_t = _rope_tables(ctx_len, s)
    bd = jnp.asarray(np.kron(np.eye(NA_HEADS), np.ones((NA_HD, NA_HD))), BF16)
    w_in_b, w1_b, w2_b = (w.astype(BF16) for w in (w_in, mlp_w1, mlp_w2))
    wa_b, wb_b, wc_b, wo_b = (w.astype(BF16) for w in (w_branch_a, w_branch_b, w_branch_c, w_out))

    bias_all = _bias_tables(natten_rpb, ctx_len)
    xs = jnp.concatenate([ctx, x], axis=1)
    for l in range(depth):
        mod = mod_sel[l]
        z, lf = _inproj_call(xs, mod, norm1_w[l][None, :], w_in_b, l, cos_t, sin_t,
                             jnp.tile(q_norm_w[l], NA_HEADS)[None, :],
                             jnp.tile(k_norm_w[l], NA_HEADS)[None, :], bd,
                             lower_bounds[:, l].reshape(1, 2 * HG_W), ctx_len)
        o_f = _hgrn_call(z, lf, ctx_len, False)
        ya = _hgrn_call(z, lf, ctx_len, True, o_f, hgrn_onorm_w[l][None, :])
        yb = _natten_call(z, bias_all, l, ctx_len)
        xs = _merge_call(xs, mod, ya, yb, z, conv_w[l], wa_b, wb_b, wc_b, wo_b, l, ctx_len)
        xs = _mlp_call(xs, mod, norm2_w[l][None, :], w1_b, w2_b, l, ctx_len, latent_only=l == depth - 1)
    return xs
```
